```python
import math
import jax, jax.numpy as jnp
from jax import lax
import numpy as np

D_MODEL = 1024
BATCH = 8
SEQ = 2048
DEPTH = 1
DEC_BATCH = 32
DEC_SEQ = 4
PAST_LEN = 16384
PAGE_SIZE = 128

MIX_WIDTH = D_MODEL
RET_HEADS = 4
RET_HEAD_DIM = MIX_WIDTH // 2 // RET_HEADS
RET_WIDTH = RET_HEADS * RET_HEAD_DIM
DIFF_HEADS = 4
DIFF_VDIM = (MIX_WIDTH - RET_WIDTH) // DIFF_HEADS
DIFF_QK_DIM = DIFF_VDIM // 2
DIFF_WIDTH = DIFF_HEADS * DIFF_VDIM
PROJ_WIDTH = 4 * RET_WIDTH + 3 * DIFF_WIDTH
D_FF = -(-8 * D_MODEL // (3 * 256)) * 256
PLE_DIM = 256
RET_CHUNK = 128
Q_BLOCK = 128
EPS = 1e-6

kernel_name = 'hybrid_retention_diffattn_step'


def rmsnorm(x, w):
    xf = x.astype(jnp.float32)
    y = xf * lax.rsqrt(jnp.mean(xf * xf, axis=-1, keepdims=True) + EPS)
    return (y * w.astype(jnp.float32)).astype(x.dtype)


def retention_log_decay():
    return jnp.log1p(-jnp.exp2(-5.0 - jnp.arange(RET_HEADS, dtype=jnp.float32)))


def alibi_slopes():
    return jnp.exp2(-8.0 / DIFF_HEADS * jnp.arange(1, DIFF_HEADS + 1, dtype=jnp.float32))


def project_heads(x, ln1_w, w_in, q_norm_w, k_norm_w):
    b, t, _ = x.shape
    xn = rmsnorm(x, ln1_w)
    proj = xn @ w_in
    r, d = RET_WIDTH, DIFF_WIDTH
    rq, rk, rv, rg, dq, dk, dv = jnp.split(proj, [r, 2 * r, 3 * r, 4 * r, 4 * r + d, 4 * r + 2 * d], axis=-1)
    rq = rq.reshape(b, t, RET_HEADS, RET_HEAD_DIM).astype(jnp.float32)
    rk = rk.reshape(b, t, RET_HEADS, RET_HEAD_DIM).astype(jnp.float32) * (RET_HEAD_DIM ** -0.5)
    rv = rv.reshape(b, t, RET_HEADS, RET_HEAD_DIM).astype(jnp.float32)
    dq = rmsnorm(dq.reshape(b, t, DIFF_HEADS, 2, DIFF_QK_DIM), q_norm_w)
    dk = rmsnorm(dk.reshape(b, t, DIFF_HEADS, 2, DIFF_QK_DIM), k_norm_w)
    dv = dv.reshape(b, t, DIFF_HEADS, DIFF_VDIM)
    return rq, rk, rv, rg, dq, dk, dv


def retention_chunk(q, k, v, state, log_g):
    L = q.shape[1]
    pos = jnp.arange(L, dtype=jnp.float32)
    rel = pos[:, None] - pos[None, :]
    decay = jnp.where(rel >= 0, jnp.exp(log_g[:, None, None] * jnp.maximum(rel, 0.0)), 0.0)
    scores = jnp.einsum('blhd,bmhd->bhlm', q, k) * decay
    o = jnp.einsum('bhlm,bmhe->blhe', scores, v)
    cross_w = jnp.exp(log_g[None, :] * (pos[:, None] + 1.0))
    o = o + jnp.einsum('blhd,bhde->blhe', q, state) * cross_w[None, :, :, None]
    k_w = jnp.exp(log_g[None, :] * (L - 1.0 - pos[:, None]))
    new_state = jnp.exp(log_g * L)[None, :, None, None] * state + jnp.einsum('blhd,blhe,lh->bhde', k, v, k_w)
    return o, new_state


def prompt_retention(rq, rk, rv, log_g):
    b, s = rq.shape[:2]
    n = s // RET_CHUNK

    def to_chunks(t):
        return t.reshape(b, n, RET_CHUNK, RET_HEADS, RET_HEAD_DIM).swapaxes(0, 1)

    def step(carry, xs):
        q, k, v = xs
        o, new_carry = retention_chunk(q, k, v, carry, log_g)
        return new_carry, o

    init = jnp.zeros((b, RET_HEADS, RET_HEAD_DIM, RET_HEAD_DIM), jnp.float32)
    final, o = lax.scan(step, init, (to_chunks(rq), to_chunks(rk), to_chunks(rv)))
    return o.swapaxes(0, 1).reshape(b, s, RET_HEADS, RET_HEAD_DIM), final


def diff_scores(q, k, q_pos, k_pos):
    s = jnp.einsum('bqhcd,bkhcd->bhcqk', q.astype(jnp.float32), k.astype(jnp.float32)) * (DIFF_QK_DIM ** -0.5)
    dist = (q_pos[:, None] - k_pos[None, :]).astype(jnp.float32)
    s = s - alibi_slopes()[:, None, None, None] * dist
    return jnp.where(dist >= 0, s, -jnp.inf)


def diff_weights(s, lam):
    a = jax.nn.softmax(s, axis=-1)
    return a[:, :, 0] - lam * a[:, :, 1]


def prompt_diff_attention(dq, dk, dv, lam):
    b, s = dq.shape[:2]
    nb = s // Q_BLOCK
    q_blocks = dq.reshape(b, nb, Q_BLOCK, DIFF_HEADS, 2, DIFF_QK_DIM).swapaxes(0, 1)
    k_pos = jnp.arange(s, dtype=jnp.int32)
    vf = dv.astype(jnp.float32)

    def block(args):
        q_blk, bi = args
        q_pos = bi * Q_BLOCK + jnp.arange(Q_BLOCK, dtype=jnp.int32)
        w = diff_weights(diff_scores(q_blk, dk, q_pos, k_pos), lam)
        return jnp.einsum('bhqk,bkhe->bqhe', w, vf)

    o = lax.map(block, (q_blocks, jnp.arange(nb, dtype=jnp.int32)))
    return o.swapaxes(0, 1).reshape(b, s, DIFF_HEADS, DIFF_VDIM)


def merge_heads(ret_o, rg, diff_o, ret_gn_w, ret_gn_b, diff_subln_w, lam_init, w_o):
    b, t = ret_o.shape[:2]
    mu = jnp.mean(ret_o, axis=-1, keepdims=True)
    var = jnp.mean(jnp.square(ret_o - mu), axis=-1, keepdims=True)
    r = (ret_o - mu) * lax.rsqrt(var + EPS)
    r = r * ret_gn_w.astype(jnp.float32).reshape(RET_HEADS, RET_HEAD_DIM) + ret_gn_b.astype(jnp.float32).reshape(RET_HEADS, RET_HEAD_DIM)
    r = r.reshape(b, t, RET_WIDTH) * jax.nn.silu(rg.astype(jnp.float32))
    dn = diff_o * lax.rsqrt(jnp.mean(diff_o * diff_o, axis=-1, keepdims=True) + EPS)
    dn = dn * diff_subln_w.astype(jnp.float32).reshape(DIFF_HEADS, DIFF_VDIM) * (1.0 - lam_init)
    mixed = jnp.concatenate([r, dn.reshape(b, t, DIFF_WIDTH)], axis=-1)
    return mixed.astype(w_o.dtype) @ w_o


def channel_and_ple(h, ln2_w, w_ffn_in, w_ffn_out, p, ln_ple_w, w_ple_gate, w_ple_proj):
    gu = rmsnorm(h, ln2_w) @ w_ffn_in
    g, u = jnp.split(gu, 2, axis=-1)
    h = h + (jax.nn.silu(g) * u) @ w_ffn_out
    gate = jax.nn.sigmoid((rmsnorm(h, ln_ple_w) @ w_ple_gate).astype(jnp.float32))
    return h + (gate * (p @ w_ple_proj).astype(jnp.float32)).astype(h.dtype)


def setup_inputs(seed: int = 0) -> dict:
    key = jax.random.key(seed)
    ks = jax.random.split(key, 32)
    f32 = jnp.float32
    n_pages = PAST_LEN // PAGE_SIZE
    n_used = DEC_BATCH * n_pages
    n_phys = n_used + n_used // 4

    def nrm(k, shape, scale):
        return jax.random.normal(k, shape, f32) * scale

    def gain(k, shape):
        return 1.0 + 0.1 * jax.random.normal(k, shape, f32)

    page_table = jax.random.permutation(ks[5], n_phys)[:n_used].reshape(DEC_BATCH, n_pages).astype(jnp.int32)
    return {
        'x_prompt': nrm(ks[0], (BATCH, SEQ, D_MODEL), 1.0),
        'x_sample': nrm(ks[1], (DEC_BATCH, DEC_SEQ, D_MODEL), 1.0),
        'cache_k': nrm(ks[2], (DEPTH, n_phys, PAGE_SIZE, DIFF_HEADS, 2 * DIFF_QK_DIM), 1.0),
        'cache_v': nrm(ks[3], (DEPTH, n_phys, PAGE_SIZE, DIFF_HEADS, DIFF_VDIM), 1.0),
        'state_ret': nrm(ks[4], (DEPTH, DEC_BATCH, RET_HEADS, RET_HEAD_DIM, RET_HEAD_DIM), 0.5),
        'page_table': page_table,
        'p_prompt': nrm(ks[6], (DEPTH, BATCH, SEQ, PLE_DIM), 1.0),
        'p_sample': nrm(ks[7], (DEPTH, DEC_BATCH, DEC_SEQ, PLE_DIM), 1.0),
        'ln1_w': gain(ks[8], (DEPTH, D_MODEL)),
        'w_in': nrm(ks[9], (DEPTH, D_MODEL, PROJ_WIDTH), D_MODEL ** -0.5),
        'q_norm_w': gain(ks[10], (DEPTH, DIFF_QK_DIM)),
        'k_norm_w': gain(ks[11], (DEPTH, DIFF_QK_DIM)),
        'lambda_q1': nrm(ks[12], (DEPTH, DIFF_QK_DIM), 0.1),
        'lambda_k1': nrm(ks[13], (DEPTH, DIFF_QK_DIM), 0.1),
        'lambda_q2': nrm(ks[14], (DEPTH, DIFF_QK_DIM), 0.1),
        'lambda_k2': nrm(ks[15], (DEPTH, DIFF_QK_DIM), 0.1),
        'ret_gn_w': gain(ks[16], (DEPTH, RET_WIDTH)),
        'ret_gn_b': nrm(ks[17], (DEPTH, RET_WIDTH), 0.02),
        'diff_subln_w': gain(ks[18], (DEPTH, DIFF_WIDTH)),
        'w_o': nrm(ks[19], (DEPTH, MIX_WIDTH, D_MODEL), MIX_WIDTH ** -0.5),
        'ln2_w': gain(ks[20], (DEPTH, D_MODEL)),
        'w_ffn_in': nrm(ks[21], (DEPTH, D_MODEL, 2 * D_FF), D_MODEL ** -0.5),
        'w_ffn_out': nrm(ks[22], (DEPTH, D_FF, D_MODEL), D_FF ** -0.5),
        'ln_ple_w': gain(ks[23], (DEPTH, D_MODEL)),
        'w_ple_gate': nrm(ks[24], (DEPTH, D_MODEL, D_MODEL), D_MODEL ** -0.5),
        'w_ple_proj': nrm(ks[25], (DEPTH, PLE_DIM, D_MODEL), PLE_DIM ** -0.5),
    }


def reference(x_prompt, x_sample, cache_k, cache_v, state_ret, page_table, p_prompt, p_sample,
              ln1_w, w_in, q_norm_w, k_norm_w, lambda_q1, lambda_k1, lambda_q2, lambda_k2,
              ret_gn_w, ret_gn_b, diff_subln_w, w_o, ln2_w, w_ffn_in, w_ffn_out,
              ln_ple_w, w_ple_gate, w_ple_proj):
    f32 = jnp.float32
    log_g = retention_log_decay()
    db, ds = x_sample.shape[:2]
    past = page_table.shape[1] * cache_k.shape[2]
    q_pos_s = past + jnp.arange(ds, dtype=jnp.int32)
    k_pos_past = jnp.arange(past, dtype=jnp.int32)
    hp, hs = x_prompt, x_sample
    kp_l, vp_l, rp_l, ks_l, vs_l, rs_l = [], [], [], [], [], []
    for i in range(DEPTH):
        lam_init = 0.8 - 0.6 * math.exp(-0.3 * i)
        lam = (jnp.exp(jnp.sum(lambda_q1[i].astype(f32) * lambda_k1[i].astype(f32)))
               - jnp.exp(jnp.sum(lambda_q2[i].astype(f32) * lambda_k2[i].astype(f32))) + lam_init)

        b, s = hp.shape[:2]
        rq, rk, rv, rg, dq, dk, dv = project_heads(hp, ln1_w[i], w_in[i], q_norm_w[i], k_norm_w[i])
        ret_o, ret_fin = prompt_retention(rq, rk, rv, log_g)
        diff_o = prompt_diff_attention(dq, dk, dv, lam)
        hp = hp + merge_heads(ret_o, rg, diff_o, ret_gn_w[i], ret_gn_b[i], diff_subln_w[i], lam_init, w_o[i])
        hp = channel_and_ple(hp, ln2_w[i], w_ffn_in[i], w_ffn_out[i], p_prompt[i], ln_ple_w[i], w_ple_gate[i], w_ple_proj[i])
        kp_l.append(dk.reshape(b, s, DIFF_HEADS, 2 * DIFF_QK_DIM))
        vp_l.append(dv)
        rp_l.append(ret_fin)

        rq, rk, rv, rg, dq, dk, dv = project_heads(hs, ln1_w[i], w_in[i], q_norm_w[i], k_norm_w[i])
        ret_o, ret_new = retention_chunk(rq, rk, rv, state_ret[i].astype(f32), log_g)
        k_past = cache_k[i][page_table].reshape(db, past, DIFF_HEADS, 2, DIFF_QK_DIM)
        v_past = cache_v[i][page_table].reshape(db, past, DIFF_HEADS, DIFF_VDIM)
        sc = jnp.concatenate([diff_scores(dq, k_past, q_pos_s, k_pos_past),
                              diff_scores(dq, dk, q_pos_s, q_pos_s)], axis=-1)
        w = diff_weights(sc, lam)
        diff_o = (jnp.einsum('bhqk,bkhe->bqhe', w[..., :past], v_past.astype(f32))
                  + jnp.einsum('bhqk,bkhe->bqhe', w[..., past:], dv.astype(f32)))
        hs = hs + merge_heads(ret_o, rg, diff_o, ret_gn_w[i], ret_gn_b[i], diff_subln_w[i], lam_init, w_o[i])
        hs = channel_and_ple(hs, ln2_w[i], w_ffn_in[i], w_ffn_out[i], p_sample[i], ln_ple_w[i], w_ple_gate[i], w_ple_proj[i])
        ks_l.append(dk.reshape(db, ds, DIFF_HEADS, 2 * DIFF_QK_DIM))
        vs_l.append(dv)
        rs_l.append(ret_new)

    return (hp, hs, jnp.stack(kp_l), jnp.stack(vp_l), jnp.stack(rp_l), jnp.stack(ks_l), jnp.stack(vs_l), jnp.stack(rs_l))
```

```python
import functools
import math

import jax
import jax.numpy as jnp
from jax import lax
from jax.experimental import pallas as pl
from jax.experimental.pallas import tpu as pltpu

F32 = jnp.float32
BF16 = jnp.bfloat16

D_MODEL = 1024
RET_HEADS = 4
HEAD_DIM = 128
DIFF_HEADS = 4
QK_DIM = 64
HALF = RET_HEADS * HEAD_DIM
PROJ_WIDTH = 7 * HALF
D_FF = 2816
PLE_DIM = 256
RET_CHUNK = 128
PAGE = 128
EPS = 1e-6
LAM_INIT = 0.8 - 0.6 * math.exp(-0.3 * 0)
LOG_G = tuple(math.log1p(-(2.0 ** (-5.0 - h))) for h in range(RET_HEADS))
SLOPES = tuple(2.0 ** (-8.0 / DIFF_HEADS * (h + 1)) for h in range(DIFF_HEADS))
NEG_BIG = -1e30

LANES = 128
VMEM_LIMIT = 56 * 1024 * 1024

PROJ_TM = 256
CHAN_TM = 256
FF_CHUNK = 256
DIFF_TQ = 256
DIFF_TK = 512
PAGES_PER_STEP = 16


def _sigmoid(x):
    return 1.0 / (1.0 + jnp.exp(-x))


def _rms(x, w):
    return x * lax.rsqrt(jnp.mean(x * x, axis=-1, keepdims=True) + EPS) * w


def _dot(a, b):
    return jnp.dot(a, b, preferred_element_type=F32)


def _dot_nt(a, b):
    return lax.dot_general(a, b, (((1,), (1,)), ((), ())), preferred_element_type=F32)


def _lambda(lq1, lk1, lq2, lk2):
    a = jnp.exp(jnp.sum(lq1 * lk1, axis=-1, keepdims=True))
    b = jnp.exp(jnp.sum(lq2 * lk2, axis=-1, keepdims=True))
    return a - b + LAM_INIT


def _const_spec(shape):
    nd = len(shape)
    return pl.BlockSpec(shape, lambda *_: (0,) * nd, pipeline_mode=pl.Buffered(1))


def _seg_rms(y, w):
    lo_mask = lax.broadcasted_iota(jnp.int32, (1, LANES), 1) < QK_DIM
    outs = []
    for g in range(HALF // LANES):
        yg = y[:, g * LANES:(g + 1) * LANES]
        t = yg * yg
        lo = jnp.sum(jnp.where(lo_mask, t, 0.0), axis=-1, keepdims=True)
        hi = jnp.sum(jnp.where(lo_mask, 0.0, t), axis=-1, keepdims=True)
        ms = jnp.where(lo_mask, lo, hi) * (1.0 / QK_DIM)
        outs.append(yg * lax.rsqrt(ms + EPS))
    return jnp.concatenate(outs, axis=-1) * w


def _proj_kernel(x_ref, ln1_ref, w_ref, qn_ref, kn_ref,
                 rq_ref, rk_ref, rv_ref, rg_ref, dq_ref, dk_ref, dv_ref, dkb_ref, dvb_ref):
    xn = _rms(x_ref[...], ln1_ref[...]).astype(BF16)

    def col(i):
        return _dot(xn, w_ref[:, i * HALF:(i + 1) * HALF])

    rq_ref[...] = col(0).astype(BF16)
    rk_ref[...] = (col(1) * (HEAD_DIM ** -0.5)).astype(BF16)
    rv_ref[...] = col(2).astype(BF16)
    rg_ref[...] = col(3)
    dq = _seg_rms(col(4), qn_ref[...])
    dq_ref[...] = (dq * (QK_DIM ** -0.5)).astype(BF16)
    dk = _seg_rms(col(5), kn_ref[...])
    dk_ref[...] = dk
    dkb_ref[...] = dk.astype(BF16)
    dv = col(6)
    dv_ref[...] = dv
    dvb_ref[...] = dv.astype(BF16)


def _project(x, ln1_w, w_in, qn_w, kn_w, tm):
    m = x.shape[0]
    tok = lambda width: pl.BlockSpec((tm, width), lambda i: (i, 0))
    half = lambda dt: jax.ShapeDtypeStruct((m, HALF), dt)
    return pl.pallas_call(
        _proj_kernel,
        grid=(m // tm,),
        in_specs=[tok(D_MODEL), _const_spec((1, D_MODEL)), _const_spec((D_MODEL, PROJ_WIDTH)),
                  _const_spec((1, HALF)), _const_spec((1, HALF))],
        out_specs=[tok(HALF)] * 9,
        out_shape=[half(BF16), half(BF16), half(BF16), half(F32), half(BF16),
                   half(F32), half(F32), half(BF16), half(BF16)],
        compiler_params=pltpu.CompilerParams(dimension_semantics=("parallel",),
                                             vmem_limit_bytes=VMEM_LIMIT),
        name="proj",
    )(x, ln1_w, w_in, qn_w, kn_w)


def _group_norm_gate(o, g, gn_w, gn_b):
    mu = jnp.mean(o, axis=-1, keepdims=True)
    d = o - mu
    var = jnp.mean(d * d, axis=-1, keepdims=True)
    r = d * lax.rsqrt(var + EPS) * gn_w + gn_b
    return r * (g * _sigmoid(g))


def _ret_kernel(rq_ref, rk_ref, rv_ref, rg_ref, gnw_ref, gnb_ref, r_ref, fin_ref, state):
    c = pl.program_id(1)

    @pl.when(c == 0)
    def _():
        state[...] = jnp.zeros_like(state)

    n = RET_CHUNK
    row = lax.broadcasted_iota(jnp.int32, (n, n), 0).astype(F32)
    col = lax.broadcasted_iota(jnp.int32, (n, n), 1).astype(F32)
    rel = row - col
    for h in range(RET_HEADS):
        hs = slice(h * HEAD_DIM, (h + 1) * HEAD_DIM)
        lg = LOG_G[h]
        decay = jnp.where(rel >= 0, jnp.exp(lg * jnp.maximum(rel, 0.0)), 0.0)
        cross_w = jnp.exp(lg * (row + 1.0))
        k_w = jnp.exp(lg * (n - 1.0 - row))
        q, k, v = rq_ref[:, hs], rk_ref[:, hs], rv_ref[:, hs]
        st = state[h]
        s = _dot_nt(q, k) * decay
        o = _dot(s.astype(BF16), v) + _dot(q, st.astype(BF16)) * cross_w
        kw_t = (k.astype(F32) * k_w).T.astype(BF16)
        state[h] = math.exp(lg * n) * st + _dot(kw_t, v)
        r = _group_norm_gate(o, rg_ref[:, hs], gnw_ref[:, hs], gnb_ref[:, hs])
        r_ref[:, hs] = r.astype(BF16)

    @pl.when(c == pl.num_programs(1) - 1)
    def _():
        fin_ref[0] = state[...]


def _prompt_retention(rq, rk, rv, rg, gn_w, gn_b, batch, seq):
    n_chunks = seq // RET_CHUNK
    tok = pl.BlockSpec((RET_CHUNK, HALF), lambda b, c: (b * n_chunks + c, 0))
    return pl.pallas_call(
        _ret_kernel,
        grid=(batch, n_chunks),
        in_specs=[tok, tok, tok, tok, _const_spec((1, HALF)), _const_spec((1, HALF))],
        out_specs=[tok, pl.BlockSpec((1, RET_HEADS, HEAD_DIM, HEAD_DIM), lambda b, c: (b, 0, 0, 0))],
        out_shape=[jax.ShapeDtypeStruct((batch * seq, HALF), BF16),
                   jax.ShapeDtypeStruct((batch, RET_HEADS, HEAD_DIM, HEAD_DIM), F32)],
        scratch_shapes=[pltpu.VMEM((RET_HEADS, HEAD_DIM, HEAD_DIM), F32)],
        compiler_params=pltpu.CompilerParams(dimension_semantics=("parallel", "arbitrary"),
                                             vmem_limit_bytes=VMEM_LIMIT),
        name="prompt_retention",
    )(rq, rk, rv, rg, gn_w, gn_b)


def _ret_step_kernel(rq_ref, rk_ref, rv_ref, rg_ref, st_ref, gnw_ref, gnb_ref, r_ref, new_ref):
    n = rq_ref.shape[1]
    row = lax.broadcasted_iota(jnp.int32, (n, LANES), 0).astype(F32)
    for h in range(RET_HEADS):
        hs = slice(h * HEAD_DIM, (h + 1) * HEAD_DIM)
        lg = LOG_G[h]
        q = rq_ref[0, :, hs].astype(F32)
        k = rk_ref[0, :, hs].astype(F32)
        v = rv_ref[0, :, hs].astype(F32)
        st = st_ref[0, h]
        q_pad = jnp.concatenate([q, jnp.zeros((16 - n, HEAD_DIM), F32)], axis=0).astype(BF16)
        o = _dot(q_pad, st.astype(BF16))[:n] * jnp.exp(lg * (row + 1.0))
        new = math.exp(lg * n) * st
        k_t = jnp.concatenate([k, jnp.zeros((8 - n, HEAD_DIM), F32)], axis=0).T
        for m in range(n):
            s_m = jnp.sum(q * k[m:m + 1, :], axis=-1, keepdims=True)
            decay = jnp.where(row >= m, jnp.exp(lg * jnp.maximum(row - m, 0.0)), 0.0)
            o = o + (s_m * decay) * v[m:m + 1, :]
            new = new + math.exp(lg * (n - 1.0 - m)) * (k_t[:, m:m + 1] * v[m:m + 1, :])
        new_ref[0, h] = new
        r = _group_norm_gate(o, rg_ref[0, :, hs], gnw_ref[:, hs], gnb_ref[:, hs])
        r_ref[0, :, hs] = r.astype(BF16)


def _sample_retention(rq, rk, rv, rg, state, gn_w, gn_b):
    db, ds = rq.shape[:2]
    tok = pl.BlockSpec((1, ds, HALF), lambda b: (b, 0, 0))
    st = pl.BlockSpec((1, RET_HEADS, HEAD_DIM, HEAD_DIM), lambda b: (b, 0, 0, 0))
    return pl.pallas_call(
        _ret_step_kernel,
        grid=(db,),
        in_specs=[tok, tok, tok, tok, st, _const_spec((1, HALF)), _const_spec((1, HALF))],
        out_specs=[tok, st],
        out_shape=[jax.ShapeDtypeStruct((db, ds, HALF), BF16),
                   jax.ShapeDtypeStruct(state.shape, F32)],
        compiler_params=pltpu.CompilerParams(dimension_semantics=("parallel",),
                                             vmem_limit_bytes=VMEM_LIMIT),
        name="sample_retention",
    )(rq, rk, rv, rg, state, gn_w, gn_b)


def _split_components(q):
    lo_mask = lax.broadcasted_iota(jnp.int32, q.shape, 1) < QK_DIM
    zero = jnp.zeros_like(q)
    return jnp.concatenate([jnp.where(lo_mask, q, zero), jnp.where(lo_mask, zero, q)], axis=0)


def _sub_norm(acc, l, lam, w):
    t = acc.shape[0] // 2
    o = acc / l
    d = o[:t] - lam * o[t:]
    return d * lax.rsqrt(jnp.mean(d * d, axis=-1, keepdims=True) + EPS) * w * (1.0 - LAM_INIT)


def _diff_kernel(q_ref, k_ref, v_ref, lq1_ref, lk1_ref, lq2_ref, lk2_ref, sub_ref, o_ref):
    i = pl.program_id(1)
    tq, tk = DIFF_TQ, DIFF_TK
    lam = _lambda(lq1_ref[...], lk1_ref[...], lq2_ref[...], lk2_ref[...])
    n_kv = ((i + 1) * tq + tk - 1) // tk
    row = lax.broadcasted_iota(jnp.int32, (2 * tq, tk), 0)
    col = lax.broadcasted_iota(jnp.int32, (2 * tq, tk), 1)
    rel = jnp.where(row >= tq, row - tq, row) - col + i * tq
    for h in range(DIFF_HEADS):
        hs = slice(h * HEAD_DIM, (h + 1) * HEAD_DIM)
        qq = _split_components(q_ref[:, hs])

        def body(j, carry, hs=hs, qq=qq, slope=SLOPES[h]):
            m, l, acc = carry
            start = pl.multiple_of(j * tk, tk)
            k = k_ref[pl.ds(start, tk), hs]
            v = v_ref[pl.ds(start, tk), hs]
            dist = rel - j * tk
            s = jnp.where(dist >= 0, _dot_nt(qq, k) - slope * dist.astype(F32), -jnp.inf)
            m_new = jnp.maximum(m, jnp.max(s, axis=-1, keepdims=True))
            alpha = jnp.exp(m - m_new)
            p = jnp.exp(s - m_new)
            l = alpha * l + jnp.sum(p, axis=-1, keepdims=True)
            acc = alpha * acc + _dot(p.astype(BF16), v)
            return m_new, l, acc

        init = (jnp.full((2 * tq, 1), NEG_BIG, F32), jnp.zeros((2 * tq, 1), F32),
                jnp.zeros((2 * tq, HEAD_DIM), F32))
        _, l, acc = lax.fori_loop(0, n_kv, body, init)
        o_ref[:, hs] = _sub_norm(acc, l, lam, sub_ref[:, hs]).astype(BF16)


def _prompt_diff_attention(dq, dk, dv, lams, sub_w, batch, seq):
    nq = seq // DIFF_TQ
    qspec = pl.BlockSpec((DIFF_TQ, HALF), lambda b, i: (b * nq + i, 0))
    kvspec = pl.BlockSpec((seq, HALF), lambda b, i: (b, 0))
    lspec = _const_spec((1, QK_DIM))
    return pl.pallas_call(
        _diff_kernel,
        grid=(batch, nq),
        in_specs=[qspec, kvspec, kvspec, lspec, lspec, lspec, lspec, _const_spec((1, HALF))],
        out_specs=qspec,
        out_shape=jax.ShapeDtypeStruct((batch * seq, HALF), BF16),
        compiler_params=pltpu.CompilerParams(dimension_semantics=("parallel", "arbitrary"),
                                             vmem_limit_bytes=VMEM_LIMIT),
        name="prompt_diff_attention",
    )(dq, dk, dv, *lams, sub_w)


def _decode_kernel(pt_ref, q_ref, kn_ref, vn_ref, lq1_ref, lk1_ref, lq2_ref, lk2_ref, sub_ref, *rest):
    npg = PAGES_PER_STEP
    k_refs, v_refs = rest[:npg], rest[npg:2 * npg]
    o_ref, m_s, l_s, acc_s = rest[2 * npg:]
    j = pl.program_id(1)
    n_steps = pl.num_programs(1)
    n = q_ref.shape[1]
    past = n_steps * npg * PAGE

    @pl.when(j == 0)
    def _():
        m_s[...] = jnp.full_like(m_s, NEG_BIG)
        l_s[...] = jnp.zeros_like(l_s)
        acc_s[...] = jnp.zeros_like(acc_s)

    row = lax.broadcasted_iota(jnp.int32, (2 * n, npg * PAGE), 0)
    q_pos = past + jnp.where(row >= n, row - n, row)
    k_pos = j * (npg * PAGE) + lax.broadcasted_iota(jnp.int32, (2 * n, npg * PAGE), 1)
    dist = (q_pos - k_pos).astype(F32)

    for h in range(DIFF_HEADS):
        hs = slice(h * HEAD_DIM, (h + 1) * HEAD_DIM)
        qq = _split_components(q_ref[0, :, hs].astype(F32)).astype(BF16)
        s = jnp.concatenate([_dot_nt(qq, k_refs[p][0, :, hs].astype(BF16)) for p in range(npg)], axis=-1)
        s = s - SLOPES[h] * dist
        m = m_s[h][:, :1]
        m_new = jnp.maximum(m, jnp.max(s, axis=-1, keepdims=True))
        alpha = jnp.exp(m - m_new)
        p = jnp.exp(s - m_new)
        l_new = alpha * l_s[h][:, :1] + jnp.sum(p, axis=-1, keepdims=True)
        pb = p.astype(BF16)
        pv = _dot(pb[:, :PAGE], v_refs[0][0, :, hs].astype(BF16))
        for pg in range(1, npg):
            pv = pv + _dot(pb[:, pg * PAGE:(pg + 1) * PAGE], v_refs[pg][0, :, hs].astype(BF16))
        acc = alpha * acc_s[h] + pv
        m_s[h] = jnp.broadcast_to(m_new, (2 * n, LANES))
        l_s[h] = jnp.broadcast_to(l_new, (2 * n, LANES))
        acc_s[h] = acc

    @pl.when(j == n_steps - 1)
    def _():
        lam = _lambda(lq1_ref[...], lk1_ref[...], lq2_ref[...], lk2_ref[...])
        r8 = lax.broadcasted_iota(jnp.int32, (2 * n, 1), 0)
        qi = jnp.where(r8 >= n, r8 - n, r8)
        for h in range(DIFF_HEADS):
            hs = slice(h * HEAD_DIM, (h + 1) * HEAD_DIM)
            qq = _split_components(q_ref[0, :, hs].astype(F32))
            kn = kn_ref[0, :, hs]
            vn = vn_ref[0, :, hs]
            m, l, acc = m_s[h][:, :1], l_s[h][:, :1], acc_s[h]
            s_cols = []
            for t in range(n):
                s_t = jnp.sum(qq * kn[t:t + 1, :], axis=-1, keepdims=True)
                s_t = s_t - SLOPES[h] * (qi - t).astype(F32)
                s_cols.append(jnp.where(qi >= t, s_t, -jnp.inf))
            m_new = m
            for s_t in s_cols:
                m_new = jnp.maximum(m_new, s_t)
            alpha = jnp.exp(m - m_new)
            l = alpha * l
            acc = alpha * acc
            for t in range(n):
                p_t = jnp.exp(s_cols[t] - m_new)
                l = l + p_t
                acc = acc + p_t * vn[t:t + 1, :]
            o_ref[0, :, hs] = _sub_norm(acc, l, lam, sub_ref[:, hs]).astype(BF16)


def _sample_diff_attention(page_table, dq, dk_new, dv_new, cache_k, cache_v, lams, sub_w):
    db, ds = dq.shape[:2]
    n_pages = page_table.shape[1]
    npg = PAGES_PER_STEP
    tok = pl.BlockSpec((1, ds, HALF), lambda b, j, pt: (b, 0, 0))
    const = lambda shape: pl.BlockSpec(shape, lambda b, j, pt: (0,) * len(shape))
    page_specs = [pl.BlockSpec((1, PAGE, HALF), functools.partial(
        lambda b, j, pt, p: (pt[b, j * npg + p], 0, 0), p=p)) for p in range(npg)]
    state = pltpu.VMEM((DIFF_HEADS, 2 * ds, LANES), F32)
    return pl.pallas_call(
        _decode_kernel,
        grid_spec=pltpu.PrefetchScalarGridSpec(
            num_scalar_prefetch=1,
            grid=(db, n_pages // npg),
            in_specs=[tok, tok, tok] + [const((1, QK_DIM))] * 4 + [const((1, HALF))] + page_specs * 2,
            out_specs=tok,
            scratch_shapes=[state, state, state],
        ),
        out_shape=jax.ShapeDtypeStruct((db, ds, HALF), BF16),
        compiler_params=pltpu.CompilerParams(dimension_semantics=("parallel", "arbitrary"),
                                             vmem_limit_bytes=VMEM_LIMIT),
        name="sample_diff_attention",
    )(page_table, dq, dk_new, dv_new, *lams, sub_w, *([cache_k] * npg), *([cache_v] * npg))


def _channel_kernel(x_ref, r_ref, dn_ref, p_ref, wo_ref, ln2_ref, wfi_ref, wfo_ref,
                    lnp_ref, wpg_ref, wpp_ref, y_ref):
    mixed = jnp.concatenate([r_ref[...], dn_ref[...]], axis=-1)
    h = x_ref[...] + _dot(mixed, wo_ref[...])
    hn = _rms(h, ln2_ref[...]).astype(BF16)
    ff = jnp.zeros_like(h)
    for c in range(D_FF // FF_CHUNK):
        g = _dot(hn, wfi_ref[:, c * FF_CHUNK:(c + 1) * FF_CHUNK])
        u = _dot(hn, wfi_ref[:, D_FF + c * FF_CHUNK:D_FF + (c + 1) * FF_CHUNK])
        act = (g * _sigmoid(g) * u).astype(BF16)
        ff = ff + _dot(act, wfo_ref[c * FF_CHUNK:(c + 1) * FF_CHUNK, :])
    h = h + ff
    gate = _sigmoid(_dot(_rms(h, lnp_ref[...]).astype(BF16), wpg_ref[...]))
    y_ref[...] = h + gate * _dot(p_ref[...].astype(BF16), wpp_ref[...])


def _channel(x, r, dn, p, w_o, ln2_w, w_ffn_in, w_ffn_out, ln_ple_w, w_ple_gate, w_ple_proj, tm):
    m = x.shape[0]
    tok = lambda width: pl.BlockSpec((tm, width), lambda i: (i, 0))
    return pl.pallas_call(
        _channel_kernel,
        grid=(m // tm,),
        in_specs=[tok(D_MODEL), tok(HALF), tok(HALF), tok(PLE_DIM),
                  _const_spec((D_MODEL, D_MODEL)), _const_spec((1, D_MODEL)),
                  _const_spec((D_MODEL, 2 * D_FF)), _const_spec((D_FF, D_MODEL)),
                  _const_spec((1, D_MODEL)), _const_spec((D_MODEL, D_MODEL)),
                  _const_spec((PLE_DIM, D_MODEL))],
        out_specs=tok(D_MODEL),
        out_shape=jax.ShapeDtypeStruct((m, D_MODEL), F32),
        compiler_params=pltpu.CompilerParams(dimension_semantics=("parallel",),
                                             vmem_limit_bytes=VMEM_LIMIT),
        name="channel",
    )(x, r, dn, p, w_o, ln2_w, w_ffn_in, w_ffn_out, ln_ple_w, w_ple_gate, w_ple_proj)


def kernel(x_prompt, x_sample, cache_k, cache_v, state_ret, page_table, p_prompt, p_sample, ln1_w, w_in, q_norm_w, k_norm_w, lambda_q1, lambda_k1, lambda_q2, lambda_k2, ret_gn_w, ret_gn_b, diff_subln_w, w_o, ln2_w, w_ffn_in, w_ffn_out, ln_ple_w, w_ple_gate, w_ple_proj):
    depth = w_in.shape[0]
    assert depth == 1, "single-layer trunk"
    b, s, _ = x_prompt.shape
    db, ds, _ = x_sample.shape
    n_phys = cache_k.shape[1]

    row = lambda a: a[0].reshape(1, -1)
    wb = lambda a: a[0].astype(BF16)
    qn_w = jnp.tile(row(q_norm_w), (1, HALF // QK_DIM))
    kn_w = jnp.tile(row(k_norm_w), (1, HALF // QK_DIM))
    lams = (row(lambda_q1), row(lambda_k1), row(lambda_q2), row(lambda_k2))
    proj_w = (row(ln1_w), wb(w_in), qn_w, kn_w)
    chan_w = (wb(w_o), row(ln2_w), wb(w_ffn_in), wb(w_ffn_out), row(ln_ple_w), wb(w_ple_gate), wb(w_ple_proj))
    gn_w, gn_b, sub_w = row(ret_gn_w), row(ret_gn_b), row(diff_subln_w)

    xp = x_prompt.reshape(b * s, D_MODEL)
    rq, rk, rv, rg, dq, dk, dv, dkb, dvb = _project(xp, *proj_w, tm=PROJ_TM)
    r, ret_fin = _prompt_retention(rq, rk, rv, rg, gn_w, gn_b, b, s)
    dn = _prompt_diff_attention(dq, dkb, dvb, lams, sub_w, b, s)
    y_prompt = _channel(xp, r, dn, p_prompt[0].reshape(b * s, PLE_DIM), *chan_w, tm=CHAN_TM)

    xs = x_sample.reshape(db * ds, D_MODEL)
    rq_s, rk_s, rv_s, rg_s, dq_s, dk_s, dv_s, _, _ = _project(xs, *proj_w, tm=db * ds)
    tok3 = lambda a: a.reshape(db, ds, HALF)
    r_s, ret_new = _sample_retention(tok3(rq_s), tok3(rk_s), tok3(rv_s), tok3(rg_s), state_ret[0], gn_w, gn_b)
    dn_s = _sample_diff_attention(page_table, tok3(dq_s), tok3(dk_s), tok3(dv_s),
                                  cache_k[0].reshape(n_phys, PAGE, HALF), cache_v[0].reshape(n_phys, PAGE, HALF),
                                  lams, sub_w)
    y_sample = _channel(xs, r_s.reshape(db * ds, HALF), dn_s.reshape(db * ds, HALF),
                        p_sample[0].reshape(db * ds, PLE_DIM), *chan_w, tm=db * ds)

    heads = lambda a, n: a.reshape(1, n, -1, DIFF_HEADS, HEAD_DIM)
    return (y_prompt.reshape(b, s, D_MODEL), y_sample.reshape(db, ds, D_MODEL),
            heads(dk, b), heads(dv, b), ret_fin[None],
            heads(dk_s, db), heads(dv_s, db), ret_new[None])
```

```python
import functools
import math

import jax
import jax.numpy as jnp
from jax import lax
from jax.experimental import pallas as pl
from jax.experimental.pallas import tpu as pltpu

F32 = jnp.float32
BF16 = jnp.bfloat16

D_MODEL = 1024
RET_HEADS = 4
HEAD_DIM = 128
DIFF_HEADS = 4
QK_DIM = 64
HALF = RET_HEADS * HEAD_DIM
PROJ_WIDTH = 7 * HALF
D_FF = 2816
PLE_DIM = 256
RET_CHUNK = 128
PAGE = 128
EPS = 1e-6
LAM_INIT = 0.8 - 0.6 * math.exp(-0.3 * 0)
LOG_G = tuple(math.log1p(-(2.0 ** (-5.0 - h))) for h in range(RET_HEADS))
SLOPES = tuple(2.0 ** (-8.0 / DIFF_HEADS * (h + 1)) for h in range(DIFF_HEADS))
NEG_BIG = -1e30

LANES = 128
VMEM_LIMIT = 56 * 1024 * 1024

PROJ_TM = 256
CHAN_TM = 256
FF_CHUNK = 256
DIFF_TQ = 256
DIFF_TK = 512
PAGES_PER_STEP = 16


def _sigmoid(x):
    return 1.0 / (1.0 + jnp.exp(-x))


def _rms(x, w):
    return x * lax.rsqrt(jnp.mean(x * x, axis=-1, keepdims=True) + EPS) * w


def _dot(a, b):
    return jnp.dot(a, b, preferred_element_type=F32)


def _dot_nt(a, b):
    return lax.dot_general(a, b, (((1,), (1,)), ((), ())), preferred_element_type=F32)


def _lambda(lq1, lk1, lq2, lk2):
    a = jnp.exp(jnp.sum(lq1 * lk1, axis=-1, keepdims=True))
    b = jnp.exp(jnp.sum(lq2 * lk2, axis=-1, keepdims=True))
    return a - b + LAM_INIT


def _const_spec(shape):
    nd = len(shape)
    return pl.BlockSpec(shape, lambda *_: (0,) * nd, pipeline_mode=pl.Buffered(1))


def _seg_rms(y, w):
    lo_mask = lax.broadcasted_iota(jnp.int32, (1, LANES), 1) < QK_DIM
    outs = []
    for g in range(HALF // LANES):
        yg = y[:, g * LANES:(g + 1) * LANES]
        t = yg * yg
        lo = jnp.sum(jnp.where(lo_mask, t, 0.0), axis=-1, keepdims=True)
        hi = jnp.sum(jnp.where(lo_mask, 0.0, t), axis=-1, keepdims=True)
        ms = jnp.where(lo_mask, lo, hi) * (1.0 / QK_DIM)
        outs.append(yg * lax.rsqrt(ms + EPS))
    return jnp.concatenate(outs, axis=-1) * w


def _proj_kernel(x_ref, ln1_ref, w_ref, qn_ref, kn_ref,
                 rq_ref, rk_ref, rv_ref, rg_ref, dq_ref, dk_ref, dv_ref, dkb_ref, dvb_ref):
    xn = _rms(x_ref[...], ln1_ref[...]).astype(BF16)

    def col(i):
        return _dot(xn, w_ref[:, i * HALF:(i + 1) * HALF])

    rq_ref[...] = col(0).astype(BF16)
    rk_ref[...] = (col(1) * (HEAD_DIM ** -0.5)).astype(BF16)
    rv_ref[...] = col(2).astype(BF16)
    rg_ref[...] = col(3)
    dq = _seg_rms(col(4), qn_ref[...])
    dq_ref[...] = (dq * (QK_DIM ** -0.5)).astype(BF16)
    dk = _seg_rms(col(5), kn_ref[...])
    dkb_ref[...] = dk.astype(BF16)
    dv = col(6)
    dvb_ref[...] = dv.astype(BF16)
    for h in range(DIFF_HEADS):
        dk_ref[:, h, :] = dk[:, h * HEAD_DIM:(h + 1) * HEAD_DIM]
        dv_ref[:, h, :] = dv[:, h * HEAD_DIM:(h + 1) * HEAD_DIM]


def _project(x, ln1_w, w_in, qn_w, kn_w, tm):
    m = x.shape[0]
    tok = lambda width: pl.BlockSpec((tm, width), lambda i: (i, 0))
    half = lambda dt: jax.ShapeDtypeStruct((m, HALF), dt)
    cache_fmt = pl.BlockSpec((tm, DIFF_HEADS, HEAD_DIM), lambda i: (i, 0, 0))
    cache_shape = jax.ShapeDtypeStruct((m, DIFF_HEADS, HEAD_DIM), F32)
    return pl.pallas_call(
        _proj_kernel,
        grid=(m // tm,),
        in_specs=[tok(D_MODEL), _const_spec((1, D_MODEL)), _const_spec((D_MODEL, PROJ_WIDTH)),
                  _const_spec((1, HALF)), _const_spec((1, HALF))],
        out_specs=[tok(HALF)] * 5 + [cache_fmt] * 2 + [tok(HALF)] * 2,
        out_shape=[half(BF16), half(BF16), half(BF16), half(F32), half(BF16),
                   cache_shape, cache_shape, half(BF16), half(BF16)],
        compiler_params=pltpu.CompilerParams(dimension_semantics=("parallel",),
                                             vmem_limit_bytes=VMEM_LIMIT),
        name="proj",
    )(x, ln1_w, w_in, qn_w, kn_w)


def _group_norm_gate(o, g, gn_w, gn_b):
    mu = jnp.mean(o, axis=-1, keepdims=True)
    d = o - mu
    var = jnp.mean(d * d, axis=-1, keepdims=True)
    r = d * lax.rsqrt(var + EPS) * gn_w + gn_b
    return r * (g * _sigmoid(g))


def _ret_kernel(rq_ref, rk_ref, rv_ref, rg_ref, gnw_ref, gnb_ref, r_ref, fin_ref, state):
    c = pl.program_id(1)

    @pl.when(c == 0)
    def _():
        state[...] = jnp.zeros_like(state)

    n = RET_CHUNK
    row = lax.broadcasted_iota(jnp.int32, (n, n), 0).astype(F32)
    col = lax.broadcasted_iota(jnp.int32, (n, n), 1).astype(F32)
    rel = row - col
    for h in range(RET_HEADS):
        hs = slice(h * HEAD_DIM, (h + 1) * HEAD_DIM)
        lg = LOG_G[h]
        decay = jnp.where(rel >= 0, jnp.exp(lg * jnp.maximum(rel, 0.0)), 0.0)
        cross_w = jnp.exp(lg * (row + 1.0))
        k_w = jnp.exp(lg * (n - 1.0 - row))
        q, k, v = rq_ref[:, hs], rk_ref[:, hs], rv_ref[:, hs]
        st = state[h]
        s = _dot_nt(q, k) * decay
        o = _dot(s.astype(BF16), v) + _dot(q, st.astype(BF16)) * cross_w
        kw_t = (k.astype(F32) * k_w).T.astype(BF16)
        state[h] = math.exp(lg * n) * st + _dot(kw_t, v)
        r = _group_norm_gate(o, rg_ref[:, hs], gnw_ref[:, hs], gnb_ref[:, hs])
        r_ref[:, hs] = r.astype(BF16)

    @pl.when(c == pl.num_programs(1) - 1)
    def _():
        fin_ref[0] = state[...]


def _prompt_retention(rq, rk, rv, rg, gn_w, gn_b, batch, seq):
    n_chunks = seq // RET_CHUNK
    tok = pl.BlockSpec((RET_CHUNK, HALF), lambda b, c: (b * n_chunks + c, 0))
    return pl.pallas_call(
        _ret_kernel,
        grid=(batch, n_chunks),
        in_specs=[tok, tok, tok, tok, _const_spec((1, HALF)), _const_spec((1, HALF))],
        out_specs=[tok, pl.BlockSpec((1, RET_HEADS, HEAD_DIM, HEAD_DIM), lambda b, c: (b, 0, 0, 0))],
        out_shape=[jax.ShapeDtypeStruct((batch * seq, HALF), BF16),
                   jax.ShapeDtypeStruct((batch, RET_HEADS, HEAD_DIM, HEAD_DIM), F32)],
        scratch_shapes=[pltpu.VMEM((RET_HEADS, HEAD_DIM, HEAD_DIM), F32)],
        compiler_params=pltpu.CompilerParams(dimension_semantics=("parallel", "arbitrary"),
                                             vmem_limit_bytes=VMEM_LIMIT),
        name="prompt_retention",
    )(rq, rk, rv, rg, gn_w, gn_b)


def _ret_step_kernel(rq_ref, rk_ref, rv_ref, rg_ref, st_ref, gnw_ref, gnb_ref, r_ref, new_ref):
    n = rq_ref.shape[1]
    row = lax.broadcasted_iota(jnp.int32, (n, LANES), 0).astype(F32)
    for h in range(RET_HEADS):
        hs = slice(h * HEAD_DIM, (h + 1) * HEAD_DIM)
        lg = LOG_G[h]
        q = rq_ref[0, :, hs].astype(F32)
        k = rk_ref[0, :, hs].astype(F32)
        v = rv_ref[0, :, hs].astype(F32)
        st = st_ref[0, h]
        q_pad = jnp.concatenate([q, jnp.zeros((16 - n, HEAD_DIM), F32)], axis=0).astype(BF16)
        o = _dot(q_pad, st.astype(BF16))[:n] * jnp.exp(lg * (row + 1.0))
        new = math.exp(lg * n) * st
        k_t = jnp.concatenate([k, jnp.zeros((8 - n, HEAD_DIM), F32)], axis=0).T
        for m in range(n):
            s_m = jnp.sum(q * k[m:m + 1, :], axis=-1, keepdims=True)
            decay = jnp.where(row >= m, jnp.exp(lg * jnp.maximum(row - m, 0.0)), 0.0)
            o = o + (s_m * decay) * v[m:m + 1, :]
            new = new + math.exp(lg * (n - 1.0 - m)) * (k_t[:, m:m + 1] * v[m:m + 1, :])
        new_ref[0, h] = new
        r = _group_norm_gate(o, rg_ref[0, :, hs], gnw_ref[:, hs], gnb_ref[:, hs])
        r_ref[0, :, hs] = r.astype(BF16)


def _sample_retention(rq, rk, rv, rg, state, gn_w, gn_b):
    db, ds = rq.shape[:2]
    tok = pl.BlockSpec((1, ds, HALF), lambda b: (b, 0, 0))
    st = pl.BlockSpec((1, RET_HEADS, HEAD_DIM, HEAD_DIM), lambda b: (b, 0, 0, 0))
    return pl.pallas_call(
        _ret_step_kernel,
        grid=(db,),
        in_specs=[tok, tok, tok, tok, st, _const_spec((1, HALF)), _const_spec((1, HALF))],
        out_specs=[tok, st],
        out_shape=[jax.ShapeDtypeStruct((db, ds, HALF), BF16),
                   jax.ShapeDtypeStruct(state.shape, F32)],
        compiler_params=pltpu.CompilerParams(dimension_semantics=("parallel",),
                                             vmem_limit_bytes=VMEM_LIMIT),
        name="sample_retention",
    )(rq, rk, rv, rg, state, gn_w, gn_b)


def _split_components(q):
    lo_mask = lax.broadcasted_iota(jnp.int32, q.shape, 1) < QK_DIM
    zero = jnp.zeros_like(q)
    return jnp.concatenate([jnp.where(lo_mask, q, zero), jnp.where(lo_mask, zero, q)], axis=0)


def _sub_norm(acc, l, lam, w):
    t = acc.shape[0] // 2
    o = acc / l
    d = o[:t] - lam * o[t:]
    return d * lax.rsqrt(jnp.mean(d * d, axis=-1, keepdims=True) + EPS) * w * (1.0 - LAM_INIT)


def _diff_kernel(q_ref, k_ref, v_ref, lq1_ref, lk1_ref, lq2_ref, lk2_ref, sub_ref, o_ref):
    i = pl.program_id(1)
    tq, tk = DIFF_TQ, DIFF_TK
    lam = _lambda(lq1_ref[...], lk1_ref[...], lq2_ref[...], lk2_ref[...])
    n_kv = ((i + 1) * tq + tk - 1) // tk
    row = lax.broadcasted_iota(jnp.int32, (2 * tq, tk), 0)
    col = lax.broadcasted_iota(jnp.int32, (2 * tq, tk), 1)
    rel = jnp.where(row >= tq, row - tq, row) - col + i * tq
    for h in range(DIFF_HEADS):
        hs = slice(h * HEAD_DIM, (h + 1) * HEAD_DIM)
        qq = _split_components(q_ref[:, hs])

        def body(j, carry, hs=hs, qq=qq, slope=SLOPES[h]):
            m, l, acc = carry
            start = pl.multiple_of(j * tk, tk)
            k = k_ref[pl.ds(start, tk), hs]
            v = v_ref[pl.ds(start, tk), hs]
            dist = rel - j * tk
            s = jnp.where(dist >= 0, _dot_nt(qq, k) - slope * dist.astype(F32), -jnp.inf)
            m_new = jnp.maximum(m, jnp.max(s, axis=-1, keepdims=True))
            alpha = jnp.exp(m - m_new)
            p = jnp.exp(s - m_new)
            l = alpha * l + jnp.sum(p, axis=-1, keepdims=True)
            acc = alpha * acc + _dot(p.astype(BF16), v)
            return m_new, l, acc

        init = (jnp.full((2 * tq, 1), NEG_BIG, F32), jnp.zeros((2 * tq, 1), F32),
                jnp.zeros((2 * tq, HEAD_DIM), F32))
        _, l, acc = lax.fori_loop(0, n_kv, body, init)
        o_ref[:, hs] = _sub_norm(acc, l, lam, sub_ref[:, hs]).astype(BF16)


def _prompt_diff_attention(dq, dk, dv, lams, sub_w, batch, seq):
    nq = seq // DIFF_TQ
    qspec = pl.BlockSpec((DIFF_TQ, HALF), lambda b, i: (b * nq + i, 0))
    kvspec = pl.BlockSpec((seq, HALF), lambda b, i: (b, 0))
    lspec = _const_spec((1, QK_DIM))
    return pl.pallas_call(
        _diff_kernel,
        grid=(batch, nq),
        in_specs=[qspec, kvspec, kvspec, lspec, lspec, lspec, lspec, _const_spec((1, HALF))],
        out_specs=qspec,
        out_shape=jax.ShapeDtypeStruct((batch * seq, HALF), BF16),
        compiler_params=pltpu.CompilerParams(dimension_semantics=("parallel", "arbitrary"),
                                             vmem_limit_bytes=VMEM_LIMIT),
        name="prompt_diff_attention",
    )(dq, dk, dv, *lams, sub_w)


def _decode_kernel(pt_ref, q_ref, kn_ref, vn_ref, lq1_ref, lk1_ref, lq2_ref, lk2_ref, sub_ref, *rest):
    npg = PAGES_PER_STEP
    k_refs, v_refs = rest[:npg], rest[npg:2 * npg]
    o_ref, bias_s, m_s, l_s, acc_s = rest[2 * npg:]
    j = pl.program_id(1)
    n_steps = pl.num_programs(1)
    n = q_ref.shape[1]
    rows = DIFF_HEADS * 2 * n
    page_rows = PAGE * DIFF_HEADS
    past = n_steps * npg * PAGE

    r1 = lax.broadcasted_iota(jnp.int32, (rows, 1), 0)
    head = r1 // (2 * n)
    qi = r1 % n
    slope = jnp.zeros((rows, 1), F32)
    for h in range(DIFF_HEADS):
        slope = jnp.where(head == h, SLOPES[h], slope)

    @pl.when(j == 0)
    def _():
        m_s[...] = jnp.full_like(m_s, NEG_BIG)
        l_s[...] = jnp.zeros_like(l_s)
        acc_s[...] = jnp.zeros_like(acc_s)
        col = lax.broadcasted_iota(jnp.int32, (rows, npg * page_rows), 1)
        dist = (past + qi - col // DIFF_HEADS).astype(F32)
        bias_s[...] = jnp.where(col % DIFF_HEADS == head, -slope * dist, -jnp.inf)

    qq = jnp.concatenate([_split_components(q_ref[0, :, h * HEAD_DIM:(h + 1) * HEAD_DIM].astype(F32))
                          for h in range(DIFF_HEADS)], axis=0).astype(BF16)
    s = jnp.concatenate([_dot_nt(qq, k_refs[p][...].astype(BF16)) for p in range(npg)], axis=-1)
    s = s + bias_s[...] + slope * (j * (npg * PAGE)).astype(F32)
    m = m_s[:, :1]
    m_new = jnp.maximum(m, jnp.max(s, axis=-1, keepdims=True))
    alpha = jnp.exp(m - m_new)
    p = jnp.exp(s - m_new)
    l_new = alpha * l_s[:, :1] + jnp.sum(p, axis=-1, keepdims=True)
    pb = p.astype(BF16)
    pv = _dot(pb[:, :page_rows], v_refs[0][...].astype(BF16))
    for pg in range(1, npg):
        pv = pv + _dot(pb[:, pg * page_rows:(pg + 1) * page_rows], v_refs[pg][...].astype(BF16))
    m_s[...] = jnp.broadcast_to(m_new, (rows, LANES))
    l_s[...] = jnp.broadcast_to(l_new, (rows, LANES))
    acc_s[...] = alpha * acc_s[...] + pv

    @pl.when(j == n_steps - 1)
    def _():
        lam = _lambda(lq1_ref[...], lk1_ref[...], lq2_ref[...], lk2_ref[...])
        r8 = lax.broadcasted_iota(jnp.int32, (2 * n, 1), 0)
        qi = jnp.where(r8 >= n, r8 - n, r8)
        for h in range(DIFF_HEADS):
            hs = slice(h * HEAD_DIM, (h + 1) * HEAD_DIM)
            hr = slice(h * 2 * n, (h + 1) * 2 * n)
            qq = _split_components(q_ref[0, :, hs].astype(F32))
            kn = kn_ref[:, h, :]
            vn = vn_ref[:, h, :]
            m, l, acc = m_s[hr, :1], l_s[hr, :1], acc_s[hr, :]
            s_cols = []
            for t in range(n):
                s_t = jnp.sum(qq * kn[t:t + 1, :], axis=-1, keepdims=True)
                s_t = s_t - SLOPES[h] * (qi - t).astype(F32)
                s_cols.append(jnp.where(qi >= t, s_t, -jnp.inf))
            m_new = m
            for s_t in s_cols:
                m_new = jnp.maximum(m_new, s_t)
            alpha = jnp.exp(m - m_new)
            l = alpha * l
            acc = alpha * acc
            for t in range(n):
                p_t = jnp.exp(s_cols[t] - m_new)
                l = l + p_t
                acc = acc + p_t * vn[t:t + 1, :]
            o_ref[0, :, hs] = _sub_norm(acc, l, lam, sub_ref[:, hs]).astype(BF16)


def _sample_diff_attention(page_table, dq, dk_new, dv_new, cache_k, cache_v, lams, sub_w):
    db, ds = dq.shape[:2]
    n_pages = page_table.shape[1]
    npg = PAGES_PER_STEP
    tok = pl.BlockSpec((1, ds, HALF), lambda b, j, pt: (b, 0, 0))
    new = pl.BlockSpec((None, ds, DIFF_HEADS, HEAD_DIM), lambda b, j, pt: (b, 0, 0, 0))
    const = lambda shape: pl.BlockSpec(shape, lambda b, j, pt: (0,) * len(shape))
    page_rows = PAGE * DIFF_HEADS
    cache_k, cache_v = (c.reshape(-1, page_rows, HEAD_DIM) for c in (cache_k, cache_v))
    page_specs = [pl.BlockSpec((None, page_rows, HEAD_DIM), functools.partial(
        lambda b, j, pt, p: (pt[b, j * npg + p], 0, 0), p=p)) for p in range(npg)]
    state = pltpu.VMEM((DIFF_HEADS * 2 * ds, LANES), F32)
    bias = pltpu.VMEM((DIFF_HEADS * 2 * ds, npg * page_rows), F32)
    return pl.pallas_call(
        _decode_kernel,
        grid_spec=pltpu.PrefetchScalarGridSpec(
            num_scalar_prefetch=1,
            grid=(db, n_pages // npg),
            in_specs=[tok, new, new] + [const((1, QK_DIM))] * 4 + [const((1, HALF))] + page_specs * 2,
            out_specs=tok,
            scratch_shapes=[bias, state, state, state],
        ),
        out_shape=jax.ShapeDtypeStruct((db, ds, HALF), BF16),
        compiler_params=pltpu.CompilerParams(dimension_semantics=("parallel", "arbitrary"),
                                             vmem_limit_bytes=VMEM_LIMIT),
        name="sample_diff_attention",
    )(page_table, dq, dk_new, dv_new, *lams, sub_w, *([cache_k] * npg), *([cache_v] * npg))


def _channel_kernel(x_ref, r_ref, dn_ref, p_ref, wo_ref, ln2_ref, wfi_ref, wfo_ref,
                    lnp_ref, wpg_ref, wpp_ref, y_ref):
    mixed = jnp.concatenate([r_ref[...], dn_ref[...]], axis=-1)
    h = x_ref[...] + _dot(mixed, wo_ref[...])
    hn = _rms(h, ln2_ref[...]).astype(BF16)
    ff = jnp.zeros_like(h)
    for c in range(D_FF // FF_CHUNK):
        g = _dot(hn, wfi_ref[:, c * FF_CHUNK:(c + 1) * FF_CHUNK])
        u = _dot(hn, wfi_ref[:, D_FF + c * FF_CHUNK:D_FF + (c + 1) * FF_CHUNK])
        act = (g * _sigmoid(g) * u).astype(BF16)
        ff = ff + _dot(act, wfo_ref[c * FF_CHUNK:(c + 1) * FF_CHUNK, :])
    h = h + ff
    gate = _sigmoid(_dot(_rms(h, lnp_ref[...]).astype(BF16), wpg_ref[...]))
    y_ref[...] = h + gate * _dot(p_ref[...].astype(BF16), wpp_ref[...])


def _channel(x, r, dn, p, w_o, ln2_w, w_ffn_in, w_ffn_out, ln_ple_w, w_ple_gate, w_ple_proj, tm):
    m = x.shape[0]
    tok = lambda width: pl.BlockSpec((tm, width), lambda i: (i, 0))
    return pl.pallas_call(
        _channel_kernel,
        grid=(m // tm,),
        in_specs=[tok(D_MODEL), tok(HALF), tok(HALF), tok(PLE_DIM),
                  _const_spec((D_MODEL, D_MODEL)), _const_spec((1, D_MODEL)),
                  _const_spec((D_MODEL, 2 * D_FF)), _const_spec((D_FF, D_MODEL)),
                  _const_spec((1, D_MODEL)), _const_spec((D_MODEL, D_MODEL)),
                  _const_spec((PLE_DIM, D_MODEL))],
        out_specs=tok(D_MODEL),
        out_shape=jax.ShapeDtypeStruct((m, D_MODEL), F32),
        compiler_params=pltpu.CompilerParams(dimension_semantics=("parallel",),
                                             vmem_limit_bytes=VMEM_LIMIT),
        name="channel",
    )(x, r, dn, p, w_o, ln2_w, w_ffn_in, w_ffn_out, ln_ple_w, w_ple_gate, w_ple_proj)


def kernel(x_prompt, x_sample, cache_k, cache_v, state_ret, page_table, p_prompt, p_sample, ln1_w, w_in, q_norm_w, k_norm_w, lambda_q1, lambda_k1, lambda_q2, lambda_k2, ret_gn_w, ret_gn_b, diff_subln_w, w_o, ln2_w, w_ffn_in, w_ffn_out, ln_ple_w, w_ple_gate, w_ple_proj):
    depth = w_in.shape[0]
    assert depth == 1, "single-layer trunk"
    b, s, _ = x_prompt.shape
    db, ds, _ = x_sample.shape

    row = lambda a: a[0].reshape(1, -1)
    wb = lambda a: a[0].astype(BF16)
    qn_w = jnp.tile(row(q_norm_w), (1, HALF // QK_DIM))
    kn_w = jnp.tile(row(k_norm_w), (1, HALF // QK_DIM))
    lams = (row(lambda_q1), row(lambda_k1), row(lambda_q2), row(lambda_k2))
    proj_w = (row(ln1_w), wb(w_in), qn_w, kn_w)
    chan_w = (wb(w_o), row(ln2_w), wb(w_ffn_in), wb(w_ffn_out), row(ln_ple_w), wb(w_ple_gate), wb(w_ple_proj))
    gn_w, gn_b, sub_w = row(ret_gn_w), row(ret_gn_b), row(diff_subln_w)

    xp = x_prompt.reshape(b * s, D_MODEL)
    rq, rk, rv, rg, dq, dk, dv, dkb, dvb = _project(xp, *proj_w, tm=PROJ_TM)
    r, ret_fin = _prompt_retention(rq, rk, rv, rg, gn_w, gn_b, b, s)
    dn = _prompt_diff_attention(dq, dkb, dvb, lams, sub_w, b, s)
    y_prompt = _channel(xp, r, dn, p_prompt[0].reshape(b * s, PLE_DIM), *chan_w, tm=CHAN_TM)

    xs = x_sample.reshape(db * ds, D_MODEL)
    rq_s, rk_s, rv_s, rg_s, dq_s, dk_s, dv_s, _, _ = _project(xs, *proj_w, tm=db * ds)
    tok3 = lambda a: a.reshape(db, ds, HALF)
    r_s, ret_new = _sample_retention(tok3(rq_s), tok3(rk_s), tok3(rv_s), tok3(rg_s), state_ret[0], gn_w, gn_b)
    new4 = lambda a: a.reshape(db, ds, DIFF_HEADS, HEAD_DIM)
    dn_s = _sample_diff_attention(page_table, tok3(dq_s), new4(dk_s), new4(dv_s), cache_k, cache_v, lams, sub_w)
    y_sample = _channel(xs, r_s.reshape(db * ds, HALF), dn_s.reshape(db * ds, HALF),
                        p_sample[0].reshape(db * ds, PLE_DIM), *chan_w, tm=db * ds)

    heads = lambda a, n: a.reshape(1, n, -1, DIFF_HEADS, HEAD_DIM)
    return (y_prompt.reshape(b, s, D_MODEL), y_sample.reshape(db, ds, D_MODEL),
            heads(dk, b), heads(dv, b), ret_fin[None],
            heads(dk_s, db), heads(dv_s, db), ret_new[None])
```

```python
import functools
import math

import jax
import jax.numpy as jnp
from jax import lax
from jax.experimental import pallas as pl
from jax.experimental.pallas import tpu as pltpu

F32 = jnp.float32
BF16 = jnp.bfloat16

D_MODEL = 1024
RET_HEADS = 4
HEAD_DIM = 128
DIFF_HEADS = 4
QK_DIM = 64
HALF = RET_HEADS * HEAD_DIM
PROJ_WIDTH = 7 * HALF
D_FF = 2816
PLE_DIM = 256
RET_CHUNK = 128
PAGE = 128
EPS = 1e-6
LAM_INIT = 0.8 - 0.6 * math.exp(-0.3 * 0)
LOG_G = tuple(math.log1p(-(2.0 ** (-5.0 - h))) for h in range(RET_HEADS))
SLOPES = tuple(2.0 ** (-8.0 / DIFF_HEADS * (h + 1)) for h in range(DIFF_HEADS))
LOG2E = math.log2(math.e)
NEG_BIG = -1e30

LANES = 128
VMEM_LIMIT = 56 * 1024 * 1024

PROJ_TM = 512
CHAN_TM = 512
FF_CHUNK = 256
DIFF_T = 512
PAGES_PER_STEP = 16
DECODE_CHAINS = 2


def _sigmoid(x):
    return 1.0 / (1.0 + jnp.exp(-x))


def _rms(x, w):
    return x * lax.rsqrt(jnp.mean(x * x, axis=-1, keepdims=True) + EPS) * w


def _dot(a, b):
    return jnp.dot(a, b, preferred_element_type=F32)


def _dot_nt(a, b):
    return lax.dot_general(a, b, (((1,), (1,)), ((), ())), preferred_element_type=F32)


def _lambda(lq1, lk1, lq2, lk2):
    a = jnp.exp(jnp.sum(lq1 * lk1, axis=-1, keepdims=True))
    b = jnp.exp(jnp.sum(lq2 * lk2, axis=-1, keepdims=True))
    return a - b + LAM_INIT


def _const_spec(shape):
    nd = len(shape)
    return pl.BlockSpec(shape, lambda *_: (0,) * nd, pipeline_mode=pl.Buffered(1))


def _seg_rms(y, w):
    lo_mask = lax.broadcasted_iota(jnp.int32, (1, LANES), 1) < QK_DIM
    outs = []
    for g in range(HALF // LANES):
        yg = y[:, g * LANES:(g + 1) * LANES]
        t = yg * yg
        lo = jnp.sum(jnp.where(lo_mask, t, 0.0), axis=-1, keepdims=True)
        hi = jnp.sum(jnp.where(lo_mask, 0.0, t), axis=-1, keepdims=True)
        ms = jnp.where(lo_mask, lo, hi) * (1.0 / QK_DIM)
        outs.append(yg * lax.rsqrt(ms + EPS))
    return jnp.concatenate(outs, axis=-1) * w


def _proj_kernel(x_ref, ln1_ref, w_ref, qn_ref, kn_ref,
                 rq_ref, rk_ref, rv_ref, rg_ref, dq_ref, dk_ref, dv_ref, dkb_ref, dvb_ref):
    xn = _rms(x_ref[...], ln1_ref[...]).astype(BF16)

    def col(i):
        return _dot(xn, w_ref[:, i * HALF:(i + 1) * HALF])

    rq_ref[...] = col(0).astype(BF16)
    rk_ref[...] = (col(1) * (HEAD_DIM ** -0.5)).astype(BF16)
    rv_ref[...] = col(2).astype(BF16)
    rg_ref[...] = col(3)
    dq = _seg_rms(col(4), qn_ref[...])
    dq_ref[...] = (dq * (LOG2E * QK_DIM ** -0.5)).astype(BF16)
    dk = _seg_rms(col(5), kn_ref[...])
    dkb_ref[...] = dk.astype(BF16)
    dv = col(6)
    dvb_ref[...] = dv.astype(BF16)
    for h in range(DIFF_HEADS):
        dk_ref[:, h, :] = dk[:, h * HEAD_DIM:(h + 1) * HEAD_DIM]
        dv_ref[:, h, :] = dv[:, h * HEAD_DIM:(h + 1) * HEAD_DIM]


def _project(x, ln1_w, w_in, qn_w, kn_w, tm):
    m = x.shape[0]
    tok = lambda width: pl.BlockSpec((tm, width), lambda i: (i, 0))
    half = lambda dt: jax.ShapeDtypeStruct((m, HALF), dt)
    cache_fmt = pl.BlockSpec((tm, DIFF_HEADS, HEAD_DIM), lambda i: (i, 0, 0))
    cache_shape = jax.ShapeDtypeStruct((m, DIFF_HEADS, HEAD_DIM), F32)
    return pl.pallas_call(
        _proj_kernel,
        grid=(m // tm,),
        in_specs=[tok(D_MODEL), _const_spec((1, D_MODEL)), _const_spec((D_MODEL, PROJ_WIDTH)),
                  _const_spec((1, HALF)), _const_spec((1, HALF))],
        out_specs=[tok(HALF)] * 5 + [cache_fmt] * 2 + [tok(HALF)] * 2,
        out_shape=[half(BF16), half(BF16), half(BF16), half(F32), half(BF16),
                   cache_shape, cache_shape, half(BF16), half(BF16)],
        compiler_params=pltpu.CompilerParams(dimension_semantics=("parallel",),
                                             vmem_limit_bytes=VMEM_LIMIT),
        name="proj",
    )(x, ln1_w, w_in, qn_w, kn_w)


def _group_norm_gate(o, g, gn_w, gn_b):
    mu = jnp.mean(o, axis=-1, keepdims=True)
    d = o - mu
    var = jnp.mean(d * d, axis=-1, keepdims=True)
    r = d * lax.rsqrt(var + EPS) * gn_w + gn_b
    return r * (g * _sigmoid(g))


def _ret_kernel(rq_ref, rk_ref, rv_ref, rg_ref, gnw_ref, gnb_ref, r_ref, fin_ref, state):
    c = pl.program_id(1)

    @pl.when(c == 0)
    def _():
        state[...] = jnp.zeros_like(state)

    n = RET_CHUNK
    row = lax.broadcasted_iota(jnp.int32, (n, n), 0).astype(F32)
    col = lax.broadcasted_iota(jnp.int32, (n, n), 1).astype(F32)
    rel = row - col
    for h in range(RET_HEADS):
        hs = slice(h * HEAD_DIM, (h + 1) * HEAD_DIM)
        lg = LOG_G[h]
        decay = jnp.where(rel >= 0, jnp.exp(lg * jnp.maximum(rel, 0.0)), 0.0)
        cross_w = jnp.exp(lg * (row + 1.0))
        k_w = jnp.exp(lg * (n - 1.0 - row))
        q, k, v = rq_ref[:, hs], rk_ref[:, hs], rv_ref[:, hs]
        st = state[h]
        s = _dot_nt(q, k) * decay
        o = _dot(s.astype(BF16), v) + _dot(q, st.astype(BF16)) * cross_w
        kw_t = (k.astype(F32) * k_w).T.astype(BF16)
        state[h] = math.exp(lg * n) * st + _dot(kw_t, v)
        r = _group_norm_gate(o, rg_ref[:, hs], gnw_ref[:, hs], gnb_ref[:, hs])
        r_ref[:, hs] = r.astype(BF16)

    @pl.when(c == pl.num_programs(1) - 1)
    def _():
        fin_ref[0] = state[...]


def _prompt_retention(rq, rk, rv, rg, gn_w, gn_b, batch, seq):
    n_chunks = seq // RET_CHUNK
    tok = pl.BlockSpec((RET_CHUNK, HALF), lambda b, c: (b * n_chunks + c, 0))
    return pl.pallas_call(
        _ret_kernel,
        grid=(batch, n_chunks),
        in_specs=[tok, tok, tok, tok, _const_spec((1, HALF)), _const_spec((1, HALF))],
        out_specs=[tok, pl.BlockSpec((1, RET_HEADS, HEAD_DIM, HEAD_DIM), lambda b, c: (b, 0, 0, 0))],
        out_shape=[jax.ShapeDtypeStruct((batch * seq, HALF), BF16),
                   jax.ShapeDtypeStruct((batch, RET_HEADS, HEAD_DIM, HEAD_DIM), F32)],
        scratch_shapes=[pltpu.VMEM((RET_HEADS, HEAD_DIM, HEAD_DIM), F32)],
        compiler_params=pltpu.CompilerParams(dimension_semantics=("parallel", "arbitrary"),
                                             vmem_limit_bytes=VMEM_LIMIT),
        name="prompt_retention",
    )(rq, rk, rv, rg, gn_w, gn_b)


def _ret_step_kernel(rq_ref, rk_ref, rv_ref, rg_ref, st_ref, gnw_ref, gnb_ref, r_ref, new_ref):
    n = rq_ref.shape[1]
    row = lax.broadcasted_iota(jnp.int32, (n, LANES), 0).astype(F32)
    for h in range(RET_HEADS):
        hs = slice(h * HEAD_DIM, (h + 1) * HEAD_DIM)
        lg = LOG_G[h]
        q = rq_ref[0, :, hs].astype(F32)
        k = rk_ref[0, :, hs].astype(F32)
        v = rv_ref[0, :, hs].astype(F32)
        st = st_ref[0, h]
        q_pad = jnp.concatenate([q, jnp.zeros((16 - n, HEAD_DIM), F32)], axis=0).astype(BF16)
        o = _dot(q_pad, st.astype(BF16))[:n] * jnp.exp(lg * (row + 1.0))
        new = math.exp(lg * n) * st
        k_t = jnp.concatenate([k, jnp.zeros((8 - n, HEAD_DIM), F32)], axis=0).T
        for m in range(n):
            s_m = jnp.sum(q * k[m:m + 1, :], axis=-1, keepdims=True)
            decay = jnp.where(row >= m, jnp.exp(lg * jnp.maximum(row - m, 0.0)), 0.0)
            o = o + (s_m * decay) * v[m:m + 1, :]
            new = new + math.exp(lg * (n - 1.0 - m)) * (k_t[:, m:m + 1] * v[m:m + 1, :])
        new_ref[0, h] = new
        r = _group_norm_gate(o, rg_ref[0, :, hs], gnw_ref[:, hs], gnb_ref[:, hs])
        r_ref[0, :, hs] = r.astype(BF16)


def _sample_retention(rq, rk, rv, rg, state, gn_w, gn_b):
    db, ds = rq.shape[:2]
    tok = pl.BlockSpec((1, ds, HALF), lambda b: (b, 0, 0))
    st = pl.BlockSpec((1, RET_HEADS, HEAD_DIM, HEAD_DIM), lambda b: (b, 0, 0, 0))
    return pl.pallas_call(
        _ret_step_kernel,
        grid=(db,),
        in_specs=[tok, tok, tok, tok, st, _const_spec((1, HALF)), _const_spec((1, HALF))],
        out_specs=[tok, st],
        out_shape=[jax.ShapeDtypeStruct((db, ds, HALF), BF16),
                   jax.ShapeDtypeStruct(state.shape, F32)],
        compiler_params=pltpu.CompilerParams(dimension_semantics=("parallel",),
                                             vmem_limit_bytes=VMEM_LIMIT),
        name="sample_retention",
    )(rq, rk, rv, rg, state, gn_w, gn_b)


def _split_components(q):
    lo_mask = lax.broadcasted_iota(jnp.int32, q.shape, 1) < QK_DIM
    zero = jnp.zeros_like(q)
    return jnp.concatenate([jnp.where(lo_mask, q, zero), jnp.where(lo_mask, zero, q)], axis=0)


def _sub_norm(acc, l, lam, w):
    t = acc.shape[0] // 2
    o = acc / l
    d = o[:t] - lam * o[t:]
    return d * lax.rsqrt(jnp.mean(d * d, axis=-1, keepdims=True) + EPS) * w * (1.0 - LAM_INIT)


def _diff_kernel(q_ref, k_ref, v_ref, lq1_ref, lk1_ref, lq2_ref, lk2_ref, sub_ref, o_ref):
    i = pl.program_id(1)
    t = DIFF_T
    lam = _lambda(lq1_ref[...], lk1_ref[...], lq2_ref[...], lk2_ref[...])
    k_idx = lax.broadcasted_iota(jnp.int32, (1, t), 1).astype(F32)
    row = lax.broadcasted_iota(jnp.int32, (2 * t, t), 0)
    col = lax.broadcasted_iota(jnp.int32, (2 * t, t), 1)
    causal = jnp.where(row >= t, row - t, row) >= col
    for h in range(DIFF_HEADS):
        hs = slice(h * HEAD_DIM, (h + 1) * HEAD_DIM)
        qq = _split_components(q_ref[:, hs])
        slope = SLOPES[h] * LOG2E

        def block(j, carry, masked, hs=hs, qq=qq, slope=slope):
            m, l, acc = carry
            start = pl.multiple_of(j * t, t)
            k = k_ref[pl.ds(start, t), hs]
            v = v_ref[pl.ds(start, t), hs]
            s = _dot_nt(qq, k) + slope * (k_idx + (j * t).astype(F32))
            if masked:
                s = jnp.where(causal, s, -jnp.inf)
            m_new = jnp.maximum(m, jnp.max(s, axis=-1, keepdims=True))
            alpha = jnp.exp2(m - m_new)
            p = jnp.exp2(s - m_new)
            l = alpha * l + jnp.sum(p, axis=-1, keepdims=True)
            acc = alpha * acc + _dot(p.astype(BF16), v)
            return m_new, l, acc

        init = (jnp.full((2 * t, 1), NEG_BIG, F32), jnp.zeros((2 * t, 1), F32),
                jnp.zeros((2 * t, HEAD_DIM), F32))
        carry = lax.fori_loop(0, i, functools.partial(block, masked=False), init)
        _, l, acc = block(i, carry, masked=True)
        o_ref[:, hs] = _sub_norm(acc, l, lam, sub_ref[:, hs]).astype(BF16)


def _prompt_diff_attention(dq, dk, dv, lams, sub_w, batch, seq):
    nq = seq // DIFF_T
    qspec = pl.BlockSpec((DIFF_T, HALF), lambda b, i: (b * nq + i, 0))
    kvspec = pl.BlockSpec((seq, HALF), lambda b, i: (b, 0))
    lspec = _const_spec((1, QK_DIM))
    return pl.pallas_call(
        _diff_kernel,
        grid=(batch, nq),
        in_specs=[qspec, kvspec, kvspec, lspec, lspec, lspec, lspec, _const_spec((1, HALF))],
        out_specs=qspec,
        out_shape=jax.ShapeDtypeStruct((batch * seq, HALF), BF16),
        compiler_params=pltpu.CompilerParams(dimension_semantics=("parallel", "arbitrary"),
                                             vmem_limit_bytes=VMEM_LIMIT),
        name="prompt_diff_attention",
    )(dq, dk, dv, *lams, sub_w)


def _decode_kernel(pt_ref, q_ref, kn_ref, vn_ref, lq1_ref, lk1_ref, lq2_ref, lk2_ref, sub_ref, *rest):
    npg = PAGES_PER_STEP
    k_refs, v_refs = rest[:npg], rest[npg:2 * npg]
    o_ref, bias_s, m_s, l_s, acc_s = rest[2 * npg:]
    j = pl.program_id(1)
    n_steps = pl.num_programs(1)
    n = q_ref.shape[1]
    rows = DIFF_HEADS * 2 * n
    page_rows = PAGE * DIFF_HEADS
    past = n_steps * npg * PAGE

    r1 = lax.broadcasted_iota(jnp.int32, (rows, 1), 0)
    head = r1 // (2 * n)
    qi = r1 % n
    slope = jnp.zeros((rows, 1), F32)
    for h in range(DIFF_HEADS):
        slope = jnp.where(head == h, SLOPES[h] * LOG2E, slope)

    @pl.when(j == 0)
    def _():
        m_s[...] = jnp.full_like(m_s, NEG_BIG)
        l_s[...] = jnp.zeros_like(l_s)
        acc_s[...] = jnp.zeros_like(acc_s)
        col = lax.broadcasted_iota(jnp.int32, (rows, npg * page_rows), 1)
        dist = (past + qi - col // DIFF_HEADS).astype(F32)
        bias_s[...] = jnp.where(col % DIFF_HEADS == head, -slope * dist, -jnp.inf)

    qq = jnp.concatenate([_split_components(q_ref[0, :, h * HEAD_DIM:(h + 1) * HEAD_DIM].astype(F32))
                          for h in range(DIFF_HEADS)], axis=0).astype(BF16)
    group_bias = slope * (j * (npg * PAGE)).astype(F32)
    per_chain = npg // DECODE_CHAINS
    chains = []
    for c in range(DECODE_CHAINS):
        pages = range(c * per_chain, (c + 1) * per_chain)
        s = jnp.concatenate([_dot_nt(qq, k_refs[p][...].astype(BF16)) for p in pages], axis=-1)
        s = s + bias_s[:, c * per_chain * page_rows:(c + 1) * per_chain * page_rows] + group_bias
        m_c = jnp.max(s, axis=-1, keepdims=True)
        pb = jnp.exp2(s - m_c)
        l_c = jnp.sum(pb, axis=-1, keepdims=True)
        pb = pb.astype(BF16)
        pv_c = _dot(pb[:, :page_rows], v_refs[pages[0]][...].astype(BF16))
        for idx, pg in enumerate(pages[1:], start=1):
            pv_c = pv_c + _dot(pb[:, idx * page_rows:(idx + 1) * page_rows], v_refs[pg][...].astype(BF16))
        chains.append((m_c, l_c, pv_c))
    m = m_s[:, :1]
    m_new = m
    for m_c, _, _ in chains:
        m_new = jnp.maximum(m_new, m_c)
    alpha = jnp.exp2(m - m_new)
    l_new = alpha * l_s[:, :1]
    acc = alpha * acc_s[...]
    for m_c, l_c, pv_c in chains:
        w = jnp.exp2(m_c - m_new)
        l_new = l_new + w * l_c
        acc = acc + w * pv_c
    m_s[...] = jnp.broadcast_to(m_new, (rows, LANES))
    l_s[...] = jnp.broadcast_to(l_new, (rows, LANES))
    acc_s[...] = acc

    @pl.when(j == n_steps - 1)
    def _():
        lam = _lambda(lq1_ref[...], lk1_ref[...], lq2_ref[...], lk2_ref[...])
        r8 = lax.broadcasted_iota(jnp.int32, (2 * n, 1), 0)
        qi = jnp.where(r8 >= n, r8 - n, r8)
        for h in range(DIFF_HEADS):
            hs = slice(h * HEAD_DIM, (h + 1) * HEAD_DIM)
            hr = slice(h * 2 * n, (h + 1) * 2 * n)
            qq = _split_components(q_ref[0, :, hs].astype(F32))
            kn = kn_ref[:, h, :]
            vn = vn_ref[:, h, :]
            m, l, acc = m_s[hr, :1], l_s[hr, :1], acc_s[hr, :]
            s_cols = []
            for t in range(n):
                s_t = jnp.sum(qq * kn[t:t + 1, :], axis=-1, keepdims=True)
                s_t = s_t - SLOPES[h] * LOG2E * (qi - t).astype(F32)
                s_cols.append(jnp.where(qi >= t, s_t, -jnp.inf))
            m_new = m
            for s_t in s_cols:
                m_new = jnp.maximum(m_new, s_t)
            alpha = jnp.exp2(m - m_new)
            l = alpha * l
            acc = alpha * acc
            for t in range(n):
                p_t = jnp.exp2(s_cols[t] - m_new)
                l = l + p_t
                acc = acc + p_t * vn[t:t + 1, :]
            o_ref[0, :, hs] = _sub_norm(acc, l, lam, sub_ref[:, hs]).astype(BF16)


def _sample_diff_attention(page_table, dq, dk_new, dv_new, cache_k, cache_v, lams, sub_w):
    db, ds = dq.shape[:2]
    n_pages = page_table.shape[1]
    npg = PAGES_PER_STEP
    tok = pl.BlockSpec((1, ds, HALF), lambda b, j, pt: (b, 0, 0))
    new = pl.BlockSpec((None, ds, DIFF_HEADS, HEAD_DIM), lambda b, j, pt: (b, 0, 0, 0))
    const = lambda shape: pl.BlockSpec(shape, lambda b, j, pt: (0,) * len(shape))
    page_rows = PAGE * DIFF_HEADS
    cache_k, cache_v = (c.reshape(-1, page_rows, HEAD_DIM) for c in (cache_k, cache_v))
    page_specs = [pl.BlockSpec((None, page_rows, HEAD_DIM), functools.partial(
        lambda b, j, pt, p: (pt[b, j * npg + p], 0, 0), p=p)) for p in range(npg)]
    state = pltpu.VMEM((DIFF_HEADS * 2 * ds, LANES), F32)
    bias = pltpu.VMEM((DIFF_HEADS * 2 * ds, npg * page_rows), F32)
    return pl.pallas_call(
        _decode_kernel,
        grid_spec=pltpu.PrefetchScalarGridSpec(
            num_scalar_prefetch=1,
            grid=(db, n_pages // npg),
            in_specs=[tok, new, new] + [const((1, QK_DIM))] * 4 + [const((1, HALF))] + page_specs * 2,
            out_specs=tok,
            scratch_shapes=[bias, state, state, state],
        ),
        out_shape=jax.ShapeDtypeStruct((db, ds, HALF), BF16),
        compiler_params=pltpu.CompilerParams(dimension_semantics=("parallel", "arbitrary"),
                                             vmem_limit_bytes=VMEM_LIMIT),
        name="sample_diff_attention",
    )(page_table, dq, dk_new, dv_new, *lams, sub_w, *([cache_k] * npg), *([cache_v] * npg))


def _channel_kernel(x_ref, r_ref, dn_ref, p_ref, wo_ref, ln2_ref, wfi_ref, wfo_ref,
                    lnp_ref, wpg_ref, wpp_ref, y_ref):
    mixed = jnp.concatenate([r_ref[...], dn_ref[...]], axis=-1)
    h = x_ref[...] + _dot(mixed, wo_ref[...])
    hn = _rms(h, ln2_ref[...]).astype(BF16)
    ff = jnp.zeros_like(h)
    for c in range(D_FF // FF_CHUNK):
        g = _dot(hn, wfi_ref[:, c * FF_CHUNK:(c + 1) * FF_CHUNK])
        u = _dot(hn, wfi_ref[:, D_FF + c * FF_CHUNK:D_FF + (c + 1) * FF_CHUNK])
        act = (g * _sigmoid(g) * u).astype(BF16)
        ff = ff + _dot(act, wfo_ref[c * FF_CHUNK:(c + 1) * FF_CHUNK, :])
    h = h + ff
    gate = _sigmoid(_dot(_rms(h, lnp_ref[...]).astype(BF16), wpg_ref[...]))
    y_ref[...] = h + gate * _dot(p_ref[...].astype(BF16), wpp_ref[...])


def _channel(x, r, dn, p, w_o, ln2_w, w_ffn_in, w_ffn_out, ln_ple_w, w_ple_gate, w_ple_proj, tm):
    m = x.shape[0]
    tok = lambda width: pl.BlockSpec((tm, width), lambda i: (i, 0))
    return pl.pallas_call(
        _channel_kernel,
        grid=(m // tm,),
        in_specs=[tok(D_MODEL), tok(HALF), tok(HALF), tok(PLE_DIM),
                  _const_spec((D_MODEL, D_MODEL)), _const_spec((1, D_MODEL)),
                  _const_spec((D_MODEL, 2 * D_FF)), _const_spec((D_FF, D_MODEL)),
                  _const_spec((1, D_MODEL)), _const_spec((D_MODEL, D_MODEL)),
                  _const_spec((PLE_DIM, D_MODEL))],
        out_specs=tok(D_MODEL),
        out_shape=jax.ShapeDtypeStruct((m, D_MODEL), F32),
        compiler_params=pltpu.CompilerParams(dimension_semantics=("parallel",),
                                             vmem_limit_bytes=VMEM_LIMIT),
        name="channel",
    )(x, r, dn, p, w_o, ln2_w, w_ffn_in, w_ffn_out, ln_ple_w, w_ple_gate, w_ple_proj)


def kernel(x_prompt, x_sample, cache_k, cache_v, state_ret, page_table, p_prompt, p_sample, ln1_w, w_in, q_norm_w, k_norm_w, lambda_q1, lambda_k1, lambda_q2, lambda_k2, ret_gn_w, ret_gn_b, diff_subln_w, w_o, ln2_w, w_ffn_in, w_ffn_out, ln_ple_w, w_ple_gate, w_ple_proj):
    depth = w_in.shape[0]
    assert depth == 1, "single-layer trunk"
    b, s, _ = x_prompt.shape
    db, ds, _ = x_sample.shape

    row = lambda a: a[0].reshape(1, -1)
    wb = lambda a: a[0].astype(BF16)
    qn_w = jnp.tile(row(q_norm_w), (1, HALF // QK_DIM))
    kn_w = jnp.tile(row(k_norm_w), (1, HALF // QK_DIM))
    lams = (row(lambda_q1), row(lambda_k1), row(lambda_q2), row(lambda_k2))
    proj_w = (row(ln1_w), wb(w_in), qn_w, kn_w)
    chan_w = (wb(w_o), row(ln2_w), wb(w_ffn_in), wb(w_ffn_out), row(ln_ple_w), wb(w_ple_gate), wb(w_ple_proj))
    gn_w, gn_b, sub_w = row(ret_gn_w), row(ret_gn_b), row(diff_subln_w)

    xp = x_prompt.reshape(b * s, D_MODEL)
    rq, rk, rv, rg, dq, dk, dv, dkb, dvb = _project(xp, *proj_w, tm=PROJ_TM)
    r, ret_fin = _prompt_retention(rq, rk, rv, rg, gn_w, gn_b, b, s)
    dn = _prompt_diff_attention(dq, dkb, dvb, lams, sub_w, b, s)
    y_prompt = _channel(xp, r, dn, p_prompt[0].reshape(b * s, PLE_DIM), *chan_w, tm=CHAN_TM)

    xs = x_sample.reshape(db * ds, D_MODEL)
    rq_s, rk_s, rv_s, rg_s, dq_s, dk_s, dv_s, _, _ = _project(xs, *proj_w, tm=db * ds)
    tok3 = lambda a: a.reshape(db, ds, HALF)
    r_s, ret_new = _sample_retention(tok3(rq_s), tok3(rk_s), tok3(rv_s), tok3(rg_s), state_ret[0], gn_w, gn_b)
    new4 = lambda a: a.reshape(db, ds, DIFF_HEADS, HEAD_DIM)
    dn_s = _sample_diff_attention(page_table, tok3(dq_s), new4(dk_s), new4(dv_s), cache_k, cache_v, lams, sub_w)
    y_sample = _channel(xs, r_s.reshape(db * ds, HALF), dn_s.reshape(db * ds, HALF),
                        p_sample[0].reshape(db * ds, PLE_DIM), *chan_w, tm=db * ds)

    heads = lambda a, n: a.reshape(1, n, -1, DIFF_HEADS, HEAD_DIM)
    return (y_prompt.reshape(b, s, D_MODEL), y_sample.reshape(db, ds, D_MODEL),
            heads(dk, b), heads(dv, b), ret_fin[None],
            heads(dk_s, db), heads(dv_s, db), ret_new[None])
```

```python
import functools
import math

import jax
import jax.numpy as jnp
from jax import lax
from jax.experimental import pallas as pl
from jax.experimental.pallas import tpu as pltpu

F32 = jnp.float32
BF16 = jnp.bfloat16

D_MODEL = 1024
RET_HEADS = 4
HEAD_DIM = 128
DIFF_HEADS = 4
QK_DIM = 64
HALF = RET_HEADS * HEAD_DIM
PROJ_WIDTH = 7 * HALF
D_FF = 2816
PLE_DIM = 256
RET_CHUNK = 128
PAGE = 128
EPS = 1e-6
LAM_INIT = 0.8 - 0.6 * math.exp(-0.3 * 0)
LOG_G = tuple(math.log1p(-(2.0 ** (-5.0 - h))) for h in range(RET_HEADS))
SLOPES = tuple(2.0 ** (-8.0 / DIFF_HEADS * (h + 1)) for h in range(DIFF_HEADS))
LOG2E = math.log2(math.e)
NEG_BIG = -1e30

LANES = 128
VMEM_LIMIT = 56 * 1024 * 1024

PROJ_TM = 512
CHAN_TM = 512
FF_CHUNK = 256
RET_SEQS_PER_STEP = 4
DIFF_T = 512
PAGES_PER_STEP = 16
DECODE_CHAINS = 2
DECODE_SLOTS = 3


def _sigmoid(x):
    return 1.0 / (1.0 + jnp.exp(-x))


def _rms(x, w):
    return x * lax.rsqrt(jnp.mean(x * x, axis=-1, keepdims=True) + EPS) * w


def _dot(a, b):
    return jnp.dot(a, b, preferred_element_type=F32)


def _dot_nt(a, b):
    return lax.dot_general(a, b, (((1,), (1,)), ((), ())), preferred_element_type=F32)


def _lambda(lq1, lk1, lq2, lk2):
    a = jnp.exp(jnp.sum(lq1 * lk1, axis=-1, keepdims=True))
    b = jnp.exp(jnp.sum(lq2 * lk2, axis=-1, keepdims=True))
    return a - b + LAM_INIT


def _const_spec(shape):
    nd = len(shape)
    return pl.BlockSpec(shape, lambda *_: (0,) * nd, pipeline_mode=pl.Buffered(1))


def _seg_rms(y, w):
    lo_mask = lax.broadcasted_iota(jnp.int32, (1, LANES), 1) < QK_DIM
    outs = []
    for g in range(HALF // LANES):
        yg = y[:, g * LANES:(g + 1) * LANES]
        t = yg * yg
        lo = jnp.sum(jnp.where(lo_mask, t, 0.0), axis=-1, keepdims=True)
        hi = jnp.sum(jnp.where(lo_mask, 0.0, t), axis=-1, keepdims=True)
        ms = jnp.where(lo_mask, lo, hi) * (1.0 / QK_DIM)
        outs.append(yg * lax.rsqrt(ms + EPS))
    return jnp.concatenate(outs, axis=-1) * w


def _store_cache_format(ref, y):
    for h in range(DIFF_HEADS):
        ref[:, h, :] = y[:, h * HEAD_DIM:(h + 1) * HEAD_DIM]


def _proj_kernel(x_ref, ln1_ref, w_ref, qn_ref, kn_ref,
                 rq_ref, rk_ref, rv_ref, rg_ref, dq_ref, dk_ref, dv_ref, dkb_ref, dvb_ref):
    xn = _rms(x_ref[...], ln1_ref[...]).astype(BF16)

    def col(i):
        return _dot(xn, w_ref[:, i * HALF:(i + 1) * HALF])

    dq = _seg_rms(col(4), qn_ref[...])
    dq_ref[...] = (dq * (LOG2E * QK_DIM ** -0.5)).astype(BF16)
    dk = _seg_rms(col(5), kn_ref[...])
    dkb_ref[...] = dk.astype(BF16)
    _store_cache_format(dk_ref, dk)
    dv = col(6)
    dvb_ref[...] = dv.astype(BF16)
    _store_cache_format(dv_ref, dv)
    rk_ref[...] = (col(1) * (HEAD_DIM ** -0.5)).astype(BF16)
    rq_ref[...] = col(0).astype(BF16)
    rv_ref[...] = col(2).astype(BF16)
    rg_ref[...] = col(3)


def _project(x, ln1_w, w_in, qn_w, kn_w, tm):
    m = x.shape[0]
    tok = lambda width: pl.BlockSpec((tm, width), lambda i: (i, 0))
    half = lambda dt: jax.ShapeDtypeStruct((m, HALF), dt)
    cache_fmt = pl.BlockSpec((tm, DIFF_HEADS, HEAD_DIM), lambda i: (i, 0, 0))
    cache_shape = jax.ShapeDtypeStruct((m, DIFF_HEADS, HEAD_DIM), F32)
    return pl.pallas_call(
        _proj_kernel,
        grid=(m // tm,),
        in_specs=[tok(D_MODEL), _const_spec((1, D_MODEL)), _const_spec((D_MODEL, PROJ_WIDTH)),
                  _const_spec((1, HALF)), _const_spec((1, HALF))],
        out_specs=[tok(HALF)] * 5 + [cache_fmt] * 2 + [tok(HALF)] * 2,
        out_shape=[half(BF16), half(BF16), half(BF16), half(F32), half(BF16),
                   cache_shape, cache_shape, half(BF16), half(BF16)],
        compiler_params=pltpu.CompilerParams(dimension_semantics=("parallel",),
                                             vmem_limit_bytes=VMEM_LIMIT),
        name="proj",
    )(x, ln1_w, w_in, qn_w, kn_w)


def _group_norm_gate(o, g, gn_w, gn_b):
    mu = jnp.mean(o, axis=-1, keepdims=True)
    d = o - mu
    var = jnp.mean(d * d, axis=-1, keepdims=True)
    r = d * lax.rsqrt(var + EPS) * gn_w + gn_b
    return r * (g * _sigmoid(g))


def _ret_kernel(rq_ref, rk_ref, rv_ref, rg_ref, gnw_ref, gnb_ref, r_ref, fin_ref, state):
    c = pl.program_id(1)

    @pl.when(c == 0)
    def _():
        state[...] = jnp.zeros_like(state)

    n = RET_CHUNK
    row = lax.broadcasted_iota(jnp.int32, (n, n), 0).astype(F32)
    col = lax.broadcasted_iota(jnp.int32, (n, n), 1).astype(F32)
    rel = row - col
    for h in range(RET_HEADS):
        hs = slice(h * HEAD_DIM, (h + 1) * HEAD_DIM)
        lg = LOG_G[h]
        decay = jnp.where(rel >= 0, jnp.exp(lg * jnp.maximum(rel, 0.0)), 0.0)
        cross_w = jnp.exp(lg * (row + 1.0))
        k_w = jnp.exp(lg * (n - 1.0 - row))
        for b in range(rq_ref.shape[0]):
            q, k, v = rq_ref[b, :, hs], rk_ref[b, :, hs], rv_ref[b, :, hs]
            st = state[b, h]
            s = _dot_nt(q, k) * decay
            o = _dot(s.astype(BF16), v) + _dot(q, st.astype(BF16)) * cross_w
            kw_t = (k.astype(F32) * k_w).T.astype(BF16)
            state[b, h] = math.exp(lg * n) * st + _dot(kw_t, v)
            r = _group_norm_gate(o, rg_ref[b, :, hs], gnw_ref[:, hs], gnb_ref[:, hs])
            r_ref[b, :, hs] = r.astype(BF16)

    @pl.when(c == pl.num_programs(1) - 1)
    def _():
        fin_ref[...] = state[...]


def _prompt_retention(rq, rk, rv, rg, gn_w, gn_b, batch, seq):
    nb = RET_SEQS_PER_STEP
    tok = pl.BlockSpec((nb, RET_CHUNK, HALF), lambda b, c: (b, c, 0))
    seqs = lambda a: a.reshape(batch, seq, HALF)
    state = (nb, RET_HEADS, HEAD_DIM, HEAD_DIM)
    r, fin = pl.pallas_call(
        _ret_kernel,
        grid=(batch // nb, seq // RET_CHUNK),
        in_specs=[tok, tok, tok, tok, _const_spec((1, HALF)), _const_spec((1, HALF))],
        out_specs=[tok, pl.BlockSpec(state, lambda b, c: (b, 0, 0, 0))],
        out_shape=[jax.ShapeDtypeStruct((batch, seq, HALF), BF16),
                   jax.ShapeDtypeStruct((batch, RET_HEADS, HEAD_DIM, HEAD_DIM), F32)],
        scratch_shapes=[pltpu.VMEM(state, F32)],
        compiler_params=pltpu.CompilerParams(dimension_semantics=("parallel", "arbitrary"),
                                             vmem_limit_bytes=VMEM_LIMIT),
        name="prompt_retention",
    )(seqs(rq), seqs(rk), seqs(rv), seqs(rg), gn_w, gn_b)
    return r.reshape(batch * seq, HALF), fin


def _ret_step_kernel(rq_ref, rk_ref, rv_ref, rg_ref, st_ref, gnw_ref, gnb_ref, r_ref, new_ref):
    n = rq_ref.shape[1]
    row = lax.broadcasted_iota(jnp.int32, (n, LANES), 0).astype(F32)
    for h in range(RET_HEADS):
        hs = slice(h * HEAD_DIM, (h + 1) * HEAD_DIM)
        lg = LOG_G[h]
        q = rq_ref[0, :, hs].astype(F32)
        k = rk_ref[0, :, hs].astype(F32)
        v = rv_ref[0, :, hs].astype(F32)
        st = st_ref[0, h]
        q_pad = jnp.concatenate([q, jnp.zeros((16 - n, HEAD_DIM), F32)], axis=0).astype(BF16)
        o = _dot(q_pad, st.astype(BF16))[:n] * jnp.exp(lg * (row + 1.0))
        new = math.exp(lg * n) * st
        k_t = jnp.concatenate([k, jnp.zeros((8 - n, HEAD_DIM), F32)], axis=0).T
        for m in range(n):
            s_m = jnp.sum(q * k[m:m + 1, :], axis=-1, keepdims=True)
            decay = jnp.where(row >= m, jnp.exp(lg * jnp.maximum(row - m, 0.0)), 0.0)
            o = o + (s_m * decay) * v[m:m + 1, :]
            new = new + math.exp(lg * (n - 1.0 - m)) * (k_t[:, m:m + 1] * v[m:m + 1, :])
        new_ref[0, h] = new
        r = _group_norm_gate(o, rg_ref[0, :, hs], gnw_ref[:, hs], gnb_ref[:, hs])
        r_ref[0, :, hs] = r.astype(BF16)


def _sample_retention(rq, rk, rv, rg, state, gn_w, gn_b):
    db, ds = rq.shape[:2]
    tok = pl.BlockSpec((1, ds, HALF), lambda b: (b, 0, 0))
    st = pl.BlockSpec((1, RET_HEADS, HEAD_DIM, HEAD_DIM), lambda b: (b, 0, 0, 0))
    return pl.pallas_call(
        _ret_step_kernel,
        grid=(db,),
        in_specs=[tok, tok, tok, tok, st, _const_spec((1, HALF)), _const_spec((1, HALF))],
        out_specs=[tok, st],
        out_shape=[jax.ShapeDtypeStruct((db, ds, HALF), BF16),
                   jax.ShapeDtypeStruct(state.shape, F32)],
        compiler_params=pltpu.CompilerParams(dimension_semantics=("parallel",),
                                             vmem_limit_bytes=VMEM_LIMIT),
        name="sample_retention",
    )(rq, rk, rv, rg, state, gn_w, gn_b)


def _split_components(q):
    lo_mask = lax.broadcasted_iota(jnp.int32, q.shape, 1) < QK_DIM
    zero = jnp.zeros_like(q)
    return jnp.concatenate([jnp.where(lo_mask, q, zero), jnp.where(lo_mask, zero, q)], axis=0)


def _sub_norm(acc, l, lam, w):
    t = acc.shape[0] // 2
    o = acc / l
    d = o[:t] - lam * o[t:]
    return d * lax.rsqrt(jnp.mean(d * d, axis=-1, keepdims=True) + EPS) * w * (1.0 - LAM_INIT)


def _diff_kernel(q_ref, k_ref, v_ref, lq1_ref, lk1_ref, lq2_ref, lk2_ref, sub_ref, o_ref):
    i = pl.program_id(1)
    t = DIFF_T
    lam = _lambda(lq1_ref[...], lk1_ref[...], lq2_ref[...], lk2_ref[...])
    k_idx = lax.broadcasted_iota(jnp.int32, (1, t), 1).astype(F32)
    row = lax.broadcasted_iota(jnp.int32, (2 * t, t), 0)
    col = lax.broadcasted_iota(jnp.int32, (2 * t, t), 1)
    causal = jnp.where(row >= t, row - t, row) >= col
    for h in range(DIFF_HEADS):
        hs = slice(h * HEAD_DIM, (h + 1) * HEAD_DIM)
        qq = _split_components(q_ref[:, hs])
        slope = SLOPES[h] * LOG2E

        def block(j, carry, masked, hs=hs, qq=qq, slope=slope):
            m, l, acc = carry
            start = pl.multiple_of(j * t, t)
            k = k_ref[pl.ds(start, t), hs]
            v = v_ref[pl.ds(start, t), hs]
            s = _dot_nt(qq, k) + slope * (k_idx + (j * t).astype(F32))
            if masked:
                s = jnp.where(causal, s, -jnp.inf)
            m_new = jnp.maximum(m, jnp.max(s, axis=-1, keepdims=True))
            alpha = jnp.exp2(m - m_new)
            p = jnp.exp2(s - m_new)
            l = alpha * l + jnp.sum(p, axis=-1, keepdims=True)
            acc = alpha * acc + _dot(p.astype(BF16), v)
            return m_new, l, acc

        init = (jnp.full((2 * t, 1), NEG_BIG, F32), jnp.zeros((2 * t, 1), F32),
                jnp.zeros((2 * t, HEAD_DIM), F32))
        carry = lax.fori_loop(0, i, functools.partial(block, masked=False), init)
        _, l, acc = block(i, carry, masked=True)
        o_ref[:, hs] = _sub_norm(acc, l, lam, sub_ref[:, hs]).astype(BF16)


def _prompt_diff_attention(dq, dk, dv, lams, sub_w, batch, seq):
    nq = seq // DIFF_T
    qspec = pl.BlockSpec((DIFF_T, HALF), lambda b, i: (b * nq + i, 0))
    kvspec = pl.BlockSpec((seq, HALF), lambda b, i: (b, 0))
    lspec = _const_spec((1, QK_DIM))
    return pl.pallas_call(
        _diff_kernel,
        grid=(batch, nq),
        in_specs=[qspec, kvspec, kvspec, lspec, lspec, lspec, lspec, _const_spec((1, HALF))],
        out_specs=qspec,
        out_shape=jax.ShapeDtypeStruct((batch * seq, HALF), BF16),
        compiler_params=pltpu.CompilerParams(dimension_semantics=("parallel", "arbitrary"),
                                             vmem_limit_bytes=VMEM_LIMIT),
        name="prompt_diff_attention",
    )(dq, dk, dv, *lams, sub_w)


def _decode_kernel(pt_ref, q_ref, kn_ref, vn_ref, lq1_ref, lk1_ref, lq2_ref, lk2_ref, sub_ref, ck_hbm, cv_hbm,
                   o_ref, kbuf, vbuf, sems, bias_s, m_s, l_s, acc_s, *, n_seq, n_steps):
    npg = PAGES_PER_STEP
    j = pl.program_id(1)
    n = q_ref.shape[1]
    rows = DIFF_HEADS * 2 * n
    page_rows = PAGE * DIFF_HEADS
    past = n_steps * npg * PAGE
    step = pl.program_id(0) * n_steps + j
    total = n_seq * n_steps

    def page_copies(t):
        slot = t % DECODE_SLOTS
        seq, group = t // n_steps, t % n_steps
        copies = []
        for p in range(npg):
            page = pt_ref[seq, group * npg + p]
            copies.append(pltpu.make_async_copy(ck_hbm.at[page], kbuf.at[slot, p], sems.at[0, slot]))
            copies.append(pltpu.make_async_copy(cv_hbm.at[page], vbuf.at[slot, p], sems.at[1, slot]))
        return copies

    @pl.when(step == 0)
    def _():
        for t in range(DECODE_SLOTS - 1):
            for c in page_copies(t):
                c.start()

    @pl.when(step + (DECODE_SLOTS - 1) < total)
    def _():
        for c in page_copies(step + (DECODE_SLOTS - 1)):
            c.start()

    for c in page_copies(step):
        c.wait()
    slot = step % DECODE_SLOTS
    k_pages = [kbuf.at[slot, p] for p in range(npg)]
    v_pages = [vbuf.at[slot, p] for p in range(npg)]

    r1 = lax.broadcasted_iota(jnp.int32, (rows, 1), 0)
    head = r1 // (2 * n)
    qi = r1 % n
    slope = jnp.zeros((rows, 1), F32)
    for h in range(DIFF_HEADS):
        slope = jnp.where(head == h, SLOPES[h] * LOG2E, slope)

    @pl.when(j == 0)
    def _():
        m_s[...] = jnp.full_like(m_s, NEG_BIG)
        l_s[...] = jnp.zeros_like(l_s)
        acc_s[...] = jnp.zeros_like(acc_s)
        col = lax.broadcasted_iota(jnp.int32, (rows, npg * page_rows), 1)
        dist = (past + qi - col // DIFF_HEADS).astype(F32)
        bias_s[...] = jnp.where(col % DIFF_HEADS == head, -slope * dist, -jnp.inf)

    qq = jnp.concatenate([_split_components(q_ref[0, :, h * HEAD_DIM:(h + 1) * HEAD_DIM].astype(F32))
                          for h in range(DIFF_HEADS)], axis=0).astype(BF16)
    group_bias = slope * (j * (npg * PAGE)).astype(F32)
    per_chain = npg // DECODE_CHAINS
    chains = []
    for c in range(DECODE_CHAINS):
        pages = range(c * per_chain, (c + 1) * per_chain)
        s = jnp.concatenate([_dot_nt(qq, k_pages[p][...].astype(BF16)) for p in pages], axis=-1)
        s = s + bias_s[:, c * per_chain * page_rows:(c + 1) * per_chain * page_rows] + group_bias
        m_c = jnp.max(s, axis=-1, keepdims=True)
        pb = jnp.exp2(s - m_c)
        l_c = jnp.sum(pb, axis=-1, keepdims=True)
        pb = pb.astype(BF16)
        pv_c = _dot(pb[:, :page_rows], v_pages[pages[0]][...].astype(BF16))
        for idx, pg in enumerate(pages[1:], start=1):
            pv_c = pv_c + _dot(pb[:, idx * page_rows:(idx + 1) * page_rows], v_pages[pg][...].astype(BF16))
        chains.append((m_c, l_c, pv_c))
    m = m_s[:, :1]
    m_new = m
    for m_c, _, _ in chains:
        m_new = jnp.maximum(m_new, m_c)
    alpha = jnp.exp2(m - m_new)
    l_new = alpha * l_s[:, :1]
    acc = alpha * acc_s[...]
    for m_c, l_c, pv_c in chains:
        w = jnp.exp2(m_c - m_new)
        l_new = l_new + w * l_c
        acc = acc + w * pv_c
    m_s[...] = jnp.broadcast_to(m_new, (rows, LANES))
    l_s[...] = jnp.broadcast_to(l_new, (rows, LANES))
    acc_s[...] = acc

    @pl.when(j == n_steps - 1)
    def _():
        lam = _lambda(lq1_ref[...], lk1_ref[...], lq2_ref[...], lk2_ref[...])
        r8 = lax.broadcasted_iota(jnp.int32, (2 * n, 1), 0)
        qi = jnp.where(r8 >= n, r8 - n, r8)
        for h in range(DIFF_HEADS):
            hs = slice(h * HEAD_DIM, (h + 1) * HEAD_DIM)
            hr = slice(h * 2 * n, (h + 1) * 2 * n)
            qq = _split_components(q_ref[0, :, hs].astype(F32))
            kn = kn_ref[:, h, :]
            vn = vn_ref[:, h, :]
            m, l, acc = m_s[hr, :1], l_s[hr, :1], acc_s[hr, :]
            s_cols = []
            for t in range(n):
                s_t = jnp.sum(qq * kn[t:t + 1, :], axis=-1, keepdims=True)
                s_t = s_t - SLOPES[h] * LOG2E * (qi - t).astype(F32)
                s_cols.append(jnp.where(qi >= t, s_t, -jnp.inf))
            m_new = m
            for s_t in s_cols:
                m_new = jnp.maximum(m_new, s_t)
            alpha = jnp.exp2(m - m_new)
            l = alpha * l
            acc = alpha * acc
            for t in range(n):
                p_t = jnp.exp2(s_cols[t] - m_new)
                l = l + p_t
                acc = acc + p_t * vn[t:t + 1, :]
            o_ref[0, :, hs] = _sub_norm(acc, l, lam, sub_ref[:, hs]).astype(BF16)


def _sample_diff_attention(page_table, dq, dk_new, dv_new, cache_k, cache_v, lams, sub_w):
    db, ds = dq.shape[:2]
    n_pages = page_table.shape[1]
    npg = PAGES_PER_STEP
    tok = pl.BlockSpec((1, ds, HALF), lambda b, j, pt: (b, 0, 0))
    new = pl.BlockSpec((None, ds, DIFF_HEADS, HEAD_DIM), lambda b, j, pt: (b, 0, 0, 0))
    const = lambda shape: pl.BlockSpec(shape, lambda b, j, pt: (0,) * len(shape))
    page_rows = PAGE * DIFF_HEADS
    cache_k, cache_v = (c.reshape(-1, page_rows, HEAD_DIM) for c in (cache_k, cache_v))
    hbm = pl.BlockSpec(memory_space=pl.ANY)
    ring = pltpu.VMEM((DECODE_SLOTS, npg, page_rows, HEAD_DIM), F32)
    state = pltpu.VMEM((DIFF_HEADS * 2 * ds, LANES), F32)
    bias = pltpu.VMEM((DIFF_HEADS * 2 * ds, npg * page_rows), F32)
    n_steps = n_pages // npg
    return pl.pallas_call(
        functools.partial(_decode_kernel, n_seq=db, n_steps=n_steps),
        grid_spec=pltpu.PrefetchScalarGridSpec(
            num_scalar_prefetch=1,
            grid=(db, n_steps),
            in_specs=[tok, new, new] + [const((1, QK_DIM))] * 4 + [const((1, HALF))] + [hbm, hbm],
            out_specs=tok,
            scratch_shapes=[ring, ring, pltpu.SemaphoreType.DMA((2, DECODE_SLOTS)), bias, state, state, state],
        ),
        out_shape=jax.ShapeDtypeStruct((db, ds, HALF), BF16),
        compiler_params=pltpu.CompilerParams(dimension_semantics=("arbitrary", "arbitrary"),
                                             vmem_limit_bytes=VMEM_LIMIT),
        name="sample_diff_attention",
    )(page_table, dq, dk_new, dv_new, *lams, sub_w, cache_k, cache_v)


def _channel_kernel(x_ref, r_ref, dn_ref, p_ref, wo_ref, ln2_ref, wfi_ref, wfo_ref,
                    lnp_ref, wpg_ref, wpp_ref, y_ref):
    mixed = jnp.concatenate([r_ref[...], dn_ref[...]], axis=-1)
    h = x_ref[...] + _dot(mixed, wo_ref[...])
    hn = _rms(h, ln2_ref[...]).astype(BF16)
    ff = jnp.zeros_like(h)
    for c in range(D_FF // FF_CHUNK):
        g = _dot(hn, wfi_ref[:, c * FF_CHUNK:(c + 1) * FF_CHUNK])
        u = _dot(hn, wfi_ref[:, D_FF + c * FF_CHUNK:D_FF + (c + 1) * FF_CHUNK])
        act = (g * _sigmoid(g) * u).astype(BF16)
        ff = ff + _dot(act, wfo_ref[c * FF_CHUNK:(c + 1) * FF_CHUNK, :])
    h = h + ff
    gate = _sigmoid(_dot(_rms(h, lnp_ref[...]).astype(BF16), wpg_ref[...]))
    y_ref[...] = h + gate * _dot(p_ref[...].astype(BF16), wpp_ref[...])


def _channel(x, r, dn, p, w_o, ln2_w, w_ffn_in, w_ffn_out, ln_ple_w, w_ple_gate, w_ple_proj, tm):
    m = x.shape[0]
    tok = lambda width: pl.BlockSpec((tm, width), lambda i: (i, 0))
    return pl.pallas_call(
        _channel_kernel,
        grid=(m // tm,),
        in_specs=[tok(D_MODEL), tok(HALF), tok(HALF), tok(PLE_DIM),
                  _const_spec((D_MODEL, D_MODEL)), _const_spec((1, D_MODEL)),
                  _const_spec((D_MODEL, 2 * D_FF)), _const_spec((D_FF, D_MODEL)),
                  _const_spec((1, D_MODEL)), _const_spec((D_MODEL, D_MODEL)),
                  _const_spec((PLE_DIM, D_MODEL))],
        out_specs=tok(D_MODEL),
        out_shape=jax.ShapeDtypeStruct((m, D_MODEL), F32),
        compiler_params=pltpu.CompilerParams(dimension_semantics=("parallel",),
                                             vmem_limit_bytes=VMEM_LIMIT),
        name="channel",
    )(x, r, dn, p, w_o, ln2_w, w_ffn_in, w_ffn_out, ln_ple_w, w_ple_gate, w_ple_proj)


def kernel(x_prompt, x_sample, cache_k, cache_v, state_ret, page_table, p_prompt, p_sample, ln1_w, w_in, q_norm_w, k_norm_w, lambda_q1, lambda_k1, lambda_q2, lambda_k2, ret_gn_w, ret_gn_b, diff_subln_w, w_o, ln2_w, w_ffn_in, w_ffn_out, ln_ple_w, w_ple_gate, w_ple_proj):
    depth = w_in.shape[0]
    assert depth == 1, "single-layer trunk"
    b, s, _ = x_prompt.shape
    db, ds, _ = x_sample.shape

    row = lambda a: a[0].reshape(1, -1)
    wb = lambda a: a[0].astype(BF16)
    qn_w = jnp.tile(row(q_norm_w), (1, HALF // QK_DIM))
    kn_w = jnp.tile(row(k_norm_w), (1, HALF // QK_DIM))
    lams = (row(lambda_q1), row(lambda_k1), row(lambda_q2), row(lambda_k2))
    proj_w = (row(ln1_w), wb(w_in), qn_w, kn_w)
    chan_w = (wb(w_o), row(ln2_w), wb(w_ffn_in), wb(w_ffn_out), row(ln_ple_w), wb(w_ple_gate), wb(w_ple_proj))
    gn_w, gn_b, sub_w = row(ret_gn_w), row(ret_gn_b), row(diff_subln_w)

    xp = x_prompt.reshape(b * s, D_MODEL)
    rq, rk, rv, rg, dq, dk, dv, dkb, dvb = _project(xp, *proj_w, tm=PROJ_TM)
    r, ret_fin = _prompt_retention(rq, rk, rv, rg, gn_w, gn_b, b, s)
    dn = _prompt_diff_attention(dq, dkb, dvb, lams, sub_w, b, s)
    y_prompt = _channel(xp, r, dn, p_prompt[0].reshape(b * s, PLE_DIM), *chan_w, tm=CHAN_TM)

    xs = x_sample.reshape(db * ds, D_MODEL)
    rq_s, rk_s, rv_s, rg_s, dq_s, dk_s, dv_s, _, _ = _project(xs, *proj_w, tm=db * ds)
    tok3 = lambda a: a.reshape(db, ds, HALF)
    r_s, ret_new = _sample_retention(tok3(rq_s), tok3(rk_s), tok3(rv_s), tok3(rg_s), state_ret[0], gn_w, gn_b)
    new4 = lambda a: a.reshape(db, ds, DIFF_HEADS, HEAD_DIM)
    dn_s = _sample_diff_attention(page_table, tok3(dq_s), new4(dk_s), new4(dv_s), cache_k, cache_v, lams, sub_w)
    y_sample = _channel(xs, r_s.reshape(db * ds, HALF), dn_s.reshape(db * ds, HALF),
                        p_sample[0].reshape(db * ds, PLE_DIM), *chan_w, tm=db * ds)

    heads = lambda a, n: a.reshape(1, n, -1, DIFF_HEADS, HEAD_DIM)
    return (y_prompt.reshape(b, s, D_MODEL), y_sample.reshape(db, ds, D_MODEL),
            heads(dk, b), heads(dv, b), ret_fin[None],
            heads(dk_s, db), heads(dv_s, db), ret_new[None])
```

```python
import functools
import math

import jax
import jax.numpy as jnp
from jax import lax
from jax.experimental import pallas as pl
from jax.experimental.pallas import tpu as pltpu

F32 = jnp.float32
BF16 = jnp.bfloat16

D_MODEL = 1024
RET_HEADS = 4
HEAD_DIM = 128
DIFF_HEADS = 4
QK_DIM = 64
HALF = RET_HEADS * HEAD_DIM
PROJ_WIDTH = 7 * HALF
D_FF = 2816
PLE_DIM = 256
RET_CHUNK = 128
PAGE = 128
EPS = 1e-6
LAM_INIT = 0.8 - 0.6 * math.exp(-0.3 * 0)
LOG_G = tuple(math.log1p(-(2.0 ** (-5.0 - h))) for h in range(RET_HEADS))
SLOPES = tuple(2.0 ** (-8.0 / DIFF_HEADS * (h + 1)) for h in range(DIFF_HEADS))
LOG2E = math.log2(math.e)
NEG_BIG = -1e30

LANES = 128
VMEM_LIMIT = 56 * 1024 * 1024

CHAN_TM = 512
FF_CHUNK = 256
RET_SEQS_PER_STEP = 4
DIFF_T = 512
PAGES_PER_STEP = 16
DECODE_CHAINS = 2
DECODE_SLOTS = 3


def _sigmoid(x):
    return 1.0 / (1.0 + jnp.exp(-x))


def _rms(x, w):
    return x * lax.rsqrt(jnp.mean(x * x, axis=-1, keepdims=True) + EPS) * w


def _dot(a, b):
    return jnp.dot(a, b, preferred_element_type=F32)


def _dot_nt(a, b):
    return lax.dot_general(a, b, (((1,), (1,)), ((), ())), preferred_element_type=F32)


def _lambda(lq1, lk1, lq2, lk2):
    a = jnp.exp(jnp.sum(lq1 * lk1, axis=-1, keepdims=True))
    b = jnp.exp(jnp.sum(lq2 * lk2, axis=-1, keepdims=True))
    return a - b + LAM_INIT


def _const_spec(shape):
    nd = len(shape)
    return pl.BlockSpec(shape, lambda *_: (0,) * nd, pipeline_mode=pl.Buffered(1))


def _seg_rms(y, w):
    lo_mask = lax.broadcasted_iota(jnp.int32, (1, LANES), 1) < QK_DIM
    outs = []
    for g in range(HALF // LANES):
        yg = y[:, g * LANES:(g + 1) * LANES]
        t = yg * yg
        lo = jnp.sum(jnp.where(lo_mask, t, 0.0), axis=-1, keepdims=True)
        hi = jnp.sum(jnp.where(lo_mask, 0.0, t), axis=-1, keepdims=True)
        ms = jnp.where(lo_mask, lo, hi) * (1.0 / QK_DIM)
        outs.append(yg * lax.rsqrt(ms + EPS))
    return jnp.concatenate(outs, axis=-1) * w


def _store_cache_format(ref, y):
    for h in range(DIFF_HEADS):
        ref[:, h, :] = y[:, h * HEAD_DIM:(h + 1) * HEAD_DIM]


def _proj_kernel(x_ref, ln1_ref, w_ref, qn_ref, kn_ref,
                 rq_ref, rk_ref, rv_ref, rg_ref, dq_ref, dk_ref, dv_ref, dkb_ref, dvt_ref):
    xn = _rms(x_ref[...], ln1_ref[...]).astype(BF16)

    def col(i):
        return _dot(xn, w_ref[:, i * HALF:(i + 1) * HALF])

    dq = _seg_rms(col(4), qn_ref[...])
    dq_ref[...] = (dq * (LOG2E * QK_DIM ** -0.5)).astype(BF16)
    dk = _seg_rms(col(5), kn_ref[...])
    dkb_ref[...] = dk.astype(BF16)
    _store_cache_format(dk_ref, dk)
    dv = col(6)
    dvt_ref[0] = dv.T.astype(BF16)
    _store_cache_format(dv_ref, dv)
    rk_ref[...] = (col(1) * (HEAD_DIM ** -0.5)).astype(BF16)
    rq_ref[...] = col(0).astype(BF16)
    rv_ref[...] = col(2).astype(BF16)
    rg_ref[...] = col(3)


def _project(x, ln1_w, w_in, qn_w, kn_w, tm):
    m = x.shape[0]
    tok = lambda width: pl.BlockSpec((tm, width), lambda i: (i, 0))
    half = lambda dt: jax.ShapeDtypeStruct((m, HALF), dt)
    cache_fmt = pl.BlockSpec((tm, DIFF_HEADS, HEAD_DIM), lambda i: (i, 0, 0))
    cache_shape = jax.ShapeDtypeStruct((m, DIFF_HEADS, HEAD_DIM), F32)
    return pl.pallas_call(
        _proj_kernel,
        grid=(m // tm,),
        in_specs=[tok(D_MODEL), _const_spec((1, D_MODEL)), _const_spec((D_MODEL, PROJ_WIDTH)),
                  _const_spec((1, HALF)), _const_spec((1, HALF))],
        out_specs=[tok(HALF)] * 5 + [cache_fmt] * 2 + [tok(HALF), pl.BlockSpec((1, HALF, tm), lambda i: (i, 0, 0))],
        out_shape=[half(BF16), half(BF16), half(BF16), half(F32), half(BF16),
                   cache_shape, cache_shape, half(BF16), jax.ShapeDtypeStruct((m // tm, HALF, tm), BF16)],
        compiler_params=pltpu.CompilerParams(dimension_semantics=("parallel",),
                                             vmem_limit_bytes=VMEM_LIMIT),
        name="proj",
    )(x, ln1_w, w_in, qn_w, kn_w)


def _group_norm_gate(o, g, gn_w, gn_b):
    mu = jnp.mean(o, axis=-1, keepdims=True)
    d = o - mu
    var = jnp.mean(d * d, axis=-1, keepdims=True)
    r = d * lax.rsqrt(var + EPS) * gn_w + gn_b
    return r * (g * _sigmoid(g))


def _ret_kernel(rq_ref, rk_ref, rv_ref, rg_ref, gnw_ref, gnb_ref, r_ref, fin_ref, state):
    c = pl.program_id(1)

    @pl.when(c == 0)
    def _():
        state[...] = jnp.zeros_like(state)

    n = RET_CHUNK
    row = lax.broadcasted_iota(jnp.int32, (n, n), 0).astype(F32)
    col = lax.broadcasted_iota(jnp.int32, (n, n), 1).astype(F32)
    rel = row - col
    for h in range(RET_HEADS):
        hs = slice(h * HEAD_DIM, (h + 1) * HEAD_DIM)
        lg = LOG_G[h]
        decay = jnp.where(rel >= 0, jnp.exp(lg * jnp.maximum(rel, 0.0)), 0.0)
        cross_w = jnp.exp(lg * (row + 1.0))
        k_w = jnp.exp(lg * (n - 1.0 - row))
        for b in range(rq_ref.shape[0]):
            q, k, v = rq_ref[b, :, hs], rk_ref[b, :, hs], rv_ref[b, :, hs]
            st = state[b, h]
            s = _dot_nt(q, k) * decay
            o = _dot(s.astype(BF16), v) + _dot(q, st.astype(BF16)) * cross_w
            kw_t = (k.astype(F32) * k_w).T.astype(BF16)
            state[b, h] = math.exp(lg * n) * st + _dot(kw_t, v)
            r = _group_norm_gate(o, rg_ref[b, :, hs], gnw_ref[:, hs], gnb_ref[:, hs])
            r_ref[b, :, hs] = r.astype(BF16)

    @pl.when(c == pl.num_programs(1) - 1)
    def _():
        fin_ref[...] = state[...]


def _prompt_retention(rq, rk, rv, rg, gn_w, gn_b, batch, seq):
    nb = RET_SEQS_PER_STEP
    tok = pl.BlockSpec((nb, RET_CHUNK, HALF), lambda b, c: (b, c, 0))
    seqs = lambda a: a.reshape(batch, seq, HALF)
    state = (nb, RET_HEADS, HEAD_DIM, HEAD_DIM)
    r, fin = pl.pallas_call(
        _ret_kernel,
        grid=(batch // nb, seq // RET_CHUNK),
        in_specs=[tok, tok, tok, tok, _const_spec((1, HALF)), _const_spec((1, HALF))],
        out_specs=[tok, pl.BlockSpec(state, lambda b, c: (b, 0, 0, 0))],
        out_shape=[jax.ShapeDtypeStruct((batch, seq, HALF), BF16),
                   jax.ShapeDtypeStruct((batch, RET_HEADS, HEAD_DIM, HEAD_DIM), F32)],
        scratch_shapes=[pltpu.VMEM(state, F32)],
        compiler_params=pltpu.CompilerParams(dimension_semantics=("parallel", "arbitrary"),
                                             vmem_limit_bytes=VMEM_LIMIT),
        name="prompt_retention",
    )(seqs(rq), seqs(rk), seqs(rv), seqs(rg), gn_w, gn_b)
    return r.reshape(batch * seq, HALF), fin


def _ret_step_kernel(rq_ref, rk_ref, rv_ref, rg_ref, st_ref, gnw_ref, gnb_ref, r_ref, new_ref):
    n = rq_ref.shape[1]
    row = lax.broadcasted_iota(jnp.int32, (n, LANES), 0).astype(F32)
    for h in range(RET_HEADS):
        hs = slice(h * HEAD_DIM, (h + 1) * HEAD_DIM)
        lg = LOG_G[h]
        q = rq_ref[0, :, hs].astype(F32)
        k = rk_ref[0, :, hs].astype(F32)
        v = rv_ref[0, :, hs].astype(F32)
        st = st_ref[0, h]
        q_pad = jnp.concatenate([q, jnp.zeros((16 - n, HEAD_DIM), F32)], axis=0).astype(BF16)
        o = _dot(q_pad, st.astype(BF16))[:n] * jnp.exp(lg * (row + 1.0))
        new = math.exp(lg * n) * st
        k_t = jnp.concatenate([k, jnp.zeros((8 - n, HEAD_DIM), F32)], axis=0).T
        for m in range(n):
            s_m = jnp.sum(q * k[m:m + 1, :], axis=-1, keepdims=True)
            decay = jnp.where(row >= m, jnp.exp(lg * jnp.maximum(row - m, 0.0)), 0.0)
            o = o + (s_m * decay) * v[m:m + 1, :]
            new = new + math.exp(lg * (n - 1.0 - m)) * (k_t[:, m:m + 1] * v[m:m + 1, :])
        new_ref[0, h] = new
        r = _group_norm_gate(o, rg_ref[0, :, hs], gnw_ref[:, hs], gnb_ref[:, hs])
        r_ref[0, :, hs] = r.astype(BF16)


def _sample_retention(rq, rk, rv, rg, state, gn_w, gn_b):
    db, ds = rq.shape[:2]
    tok = pl.BlockSpec((1, ds, HALF), lambda b: (b, 0, 0))
    st = pl.BlockSpec((1, RET_HEADS, HEAD_DIM, HEAD_DIM), lambda b: (b, 0, 0, 0))
    return pl.pallas_call(
        _ret_step_kernel,
        grid=(db,),
        in_specs=[tok, tok, tok, tok, st, _const_spec((1, HALF)), _const_spec((1, HALF))],
        out_specs=[tok, st],
        out_shape=[jax.ShapeDtypeStruct((db, ds, HALF), BF16),
                   jax.ShapeDtypeStruct(state.shape, F32)],
        compiler_params=pltpu.CompilerParams(dimension_semantics=("parallel",),
                                             vmem_limit_bytes=VMEM_LIMIT),
        name="sample_retention",
    )(rq, rk, rv, rg, state, gn_w, gn_b)


def _split_components(q):
    lo_mask = lax.broadcasted_iota(jnp.int32, q.shape, 1) < QK_DIM
    zero = jnp.zeros_like(q)
    return jnp.concatenate([jnp.where(lo_mask, q, zero), jnp.where(lo_mask, zero, q)], axis=0)


def _sub_norm(acc, l, lam, w):
    t = acc.shape[0] // 2
    o = acc / l
    d = o[:t] - lam * o[t:]
    return d * lax.rsqrt(jnp.mean(d * d, axis=-1, keepdims=True) + EPS) * w * (1.0 - LAM_INIT)


def _diff_kernel(q_ref, k_ref, vt_ref, lq1_ref, lk1_ref, lq2_ref, lk2_ref, sub_ref, o_ref):
    i = pl.program_id(1)
    t = DIFF_T
    lam = _lambda(lq1_ref[...], lk1_ref[...], lq2_ref[...], lk2_ref[...])
    k_idx = lax.broadcasted_iota(jnp.int32, (t, LANES), 0).astype(F32)
    key = lax.broadcasted_iota(jnp.int32, (t, 2 * t), 0)
    col = lax.broadcasted_iota(jnp.int32, (t, 2 * t), 1)
    causal = key <= jnp.where(col >= t, col - t, col)
    heads = [slice(h * HEAD_DIM, (h + 1) * HEAD_DIM) for h in range(DIFF_HEADS)]
    qqs = [_split_components(q_ref[:, hs]) for hs in heads]

    def block(j, carry, masked):
        start = pl.multiple_of(j * t, t)
        k_pos = k_idx + (j * t).astype(F32)
        out = []
        for h, hs in enumerate(heads):
            m, l, acc = carry[h]
            k = k_ref[pl.ds(start, t), hs]
            vt = vt_ref[j, hs, :]
            bias = (SLOPES[h] * LOG2E) * k_pos
            s = _dot_nt(k, qqs[h]) + jnp.concatenate([bias] * (2 * t // LANES), axis=1)
            if masked:
                s = jnp.where(causal, s, -jnp.inf)
            m_new = jnp.maximum(m, jnp.max(s, axis=0, keepdims=True))
            alpha = jnp.exp2(m - m_new)
            p = jnp.exp2(s - m_new)
            l = alpha * l + jnp.sum(p, axis=0, keepdims=True)
            acc = alpha * acc + _dot(vt, p.astype(BF16))
            out.append((m_new, l, acc))
        return tuple(out)

    init = tuple((jnp.full((1, 2 * t), NEG_BIG, F32), jnp.zeros((1, 2 * t), F32),
                  jnp.zeros((HEAD_DIM, 2 * t), F32)) for _ in heads)
    carry = lax.fori_loop(0, i, functools.partial(block, masked=False), init)
    carry = block(i, carry, masked=True)
    for (_, l, acc), hs in zip(carry, heads):
        o = acc / l
        d = o[:, :t] - lam * o[:, t:]
        dn = d * lax.rsqrt(jnp.mean(d * d, axis=0, keepdims=True) + EPS)
        o_ref[:, hs] = (dn.T * sub_ref[:, hs] * (1.0 - LAM_INIT)).astype(BF16)


def _prompt_diff_attention(dq, dk, dvt, lams, sub_w, batch, seq):
    nq = seq // DIFF_T
    qspec = pl.BlockSpec((DIFF_T, HALF), lambda b, i: (b * nq + i, 0))
    kspec = pl.BlockSpec((seq, HALF), lambda b, i: (b, 0))
    vspec = pl.BlockSpec((nq, HALF, DIFF_T), lambda b, i: (b, 0, 0))
    lspec = _const_spec((1, QK_DIM))
    return pl.pallas_call(
        _diff_kernel,
        grid=(batch, nq),
        in_specs=[qspec, kspec, vspec, lspec, lspec, lspec, lspec, _const_spec((1, HALF))],
        out_specs=qspec,
        out_shape=jax.ShapeDtypeStruct((batch * seq, HALF), BF16),
        compiler_params=pltpu.CompilerParams(dimension_semantics=("parallel", "arbitrary"),
                                             vmem_limit_bytes=VMEM_LIMIT),
        name="prompt_diff_attention",
    )(dq, dk, dvt, *lams, sub_w)


def _decode_kernel(pt_ref, q_ref, kn_ref, vn_ref, lq1_ref, lk1_ref, lq2_ref, lk2_ref, sub_ref, ck_hbm, cv_hbm,
                   o_ref, kbuf, vbuf, sems, bias_s, m_s, l_s, acc_s, *, n_seq, n_steps):
    npg = PAGES_PER_STEP
    j = pl.program_id(1)
    n = q_ref.shape[1]
    rows = DIFF_HEADS * 2 * n
    page_rows = PAGE * DIFF_HEADS
    past = n_steps * npg * PAGE
    step = pl.program_id(0) * n_steps + j
    total = n_seq * n_steps

    def page_copies(t):
        slot = t % DECODE_SLOTS
        seq, group = t // n_steps, t % n_steps
        copies = []
        for p in range(npg):
            page = pt_ref[seq, group * npg + p]
            copies.append(pltpu.make_async_copy(ck_hbm.at[page], kbuf.at[slot, p], sems.at[0, slot]))
            copies.append(pltpu.make_async_copy(cv_hbm.at[page], vbuf.at[slot, p], sems.at[1, slot]))
        return copies

    @pl.when(step == 0)
    def _():
        for t in range(DECODE_SLOTS - 1):
            for c in page_copies(t):
                c.start()

    @pl.when(step + (DECODE_SLOTS - 1) < total)
    def _():
        for c in page_copies(step + (DECODE_SLOTS - 1)):
            c.start()

    for c in page_copies(step):
        c.wait()
    slot = step % DECODE_SLOTS
    k_pages = [kbuf.at[slot, p] for p in range(npg)]
    v_pages = [vbuf.at[slot, p] for p in range(npg)]

    r1 = lax.broadcasted_iota(jnp.int32, (rows, 1), 0)
    head = r1 // (2 * n)
    qi = r1 % n
    slope = jnp.zeros((rows, 1), F32)
    for h in range(DIFF_HEADS):
        slope = jnp.where(head == h, SLOPES[h] * LOG2E, slope)

    @pl.when(j == 0)
    def _():
        m_s[...] = jnp.full_like(m_s, NEG_BIG)
        l_s[...] = jnp.zeros_like(l_s)
        acc_s[...] = jnp.zeros_like(acc_s)
        col = lax.broadcasted_iota(jnp.int32, (rows, npg * page_rows), 1)
        dist = (past + qi - col // DIFF_HEADS).astype(F32)
        bias_s[...] = jnp.where(col % DIFF_HEADS == head, -slope * dist, -jnp.inf)

    qq = jnp.concatenate([_split_components(q_ref[0, :, h * HEAD_DIM:(h + 1) * HEAD_DIM].astype(F32))
                          for h in range(DIFF_HEADS)], axis=0).astype(BF16)
    group_bias = slope * (j * (npg * PAGE)).astype(F32)
    per_chain = npg // DECODE_CHAINS
    chains = []
    for c in range(DECODE_CHAINS):
        pages = range(c * per_chain, (c + 1) * per_chain)
        s = jnp.concatenate([_dot_nt(qq, k_pages[p][...].astype(BF16)) for p in pages], axis=-1)
        s = s + bias_s[:, c * per_chain * page_rows:(c + 1) * per_chain * page_rows] + group_bias
        m_c = jnp.max(s, axis=-1, keepdims=True)
        pb = jnp.exp2(s - m_c)
        l_c = jnp.sum(pb, axis=-1, keepdims=True)
        pb = pb.astype(BF16)
        pv_c = _dot(pb[:, :page_rows], v_pages[pages[0]][...].astype(BF16))
        for idx, pg in enumerate(pages[1:], start=1):
            pv_c = pv_c + _dot(pb[:, idx * page_rows:(idx + 1) * page_rows], v_pages[pg][...].astype(BF16))
        chains.append((m_c, l_c, pv_c))
    m = m_s[:, :1]
    m_new = m
    for m_c, _, _ in chains:
        m_new = jnp.maximum(m_new, m_c)
    alpha = jnp.exp2(m - m_new)
    l_new = alpha * l_s[:, :1]
    acc = alpha * acc_s[...]
    for m_c, l_c, pv_c in chains:
        w = jnp.exp2(m_c - m_new)
        l_new = l_new + w * l_c
        acc = acc + w * pv_c
    m_s[...] = jnp.broadcast_to(m_new, (rows, LANES))
    l_s[...] = jnp.broadcast_to(l_new, (rows, LANES))
    acc_s[...] = acc

    @pl.when(j == n_steps - 1)
    def _():
        lam = _lambda(lq1_ref[...], lk1_ref[...], lq2_ref[...], lk2_ref[...])
        r8 = lax.broadcasted_iota(jnp.int32, (2 * n, 1), 0)
        qi = jnp.where(r8 >= n, r8 - n, r8)
        for h in range(DIFF_HEADS):
            hs = slice(h * HEAD_DIM, (h + 1) * HEAD_DIM)
            hr = slice(h * 2 * n, (h + 1) * 2 * n)
            qq = _split_components(q_ref[0, :, hs].astype(F32))
            kn = kn_ref[:, h, :]
            vn = vn_ref[:, h, :]
            m, l, acc = m_s[hr, :1], l_s[hr, :1], acc_s[hr, :]
            s_cols = []
            for t in range(n):
                s_t = jnp.sum(qq * kn[t:t + 1, :], axis=-1, keepdims=True)
                s_t = s_t - SLOPES[h] * LOG2E * (qi - t).astype(F32)
                s_cols.append(jnp.where(qi >= t, s_t, -jnp.inf))
            m_new = m
            for s_t in s_cols:
                m_new = jnp.maximum(m_new, s_t)
            alpha = jnp.exp2(m - m_new)
            l = alpha * l
            acc = alpha * acc
            for t in range(n):
                p_t = jnp.exp2(s_cols[t] - m_new)
                l = l + p_t
                acc = acc + p_t * vn[t:t + 1, :]
            o_ref[0, :, hs] = _sub_norm(acc, l, lam, sub_ref[:, hs]).astype(BF16)


def _sample_diff_attention(page_table, dq, dk_new, dv_new, cache_k, cache_v, lams, sub_w):
    db, ds = dq.shape[:2]
    n_pages = page_table.shape[1]
    npg = PAGES_PER_STEP
    tok = pl.BlockSpec((1, ds, HALF), lambda b, j, pt: (b, 0, 0))
    new = pl.BlockSpec((None, ds, DIFF_HEADS, HEAD_DIM), lambda b, j, pt: (b, 0, 0, 0))
    const = lambda shape: pl.BlockSpec(shape, lambda b, j, pt: (0,) * len(shape))
    page_rows = PAGE * DIFF_HEADS
    cache_k, cache_v = (c.reshape(-1, page_rows, HEAD_DIM) for c in (cache_k, cache_v))
    hbm = pl.BlockSpec(memory_space=pl.ANY)
    ring = pltpu.VMEM((DECODE_SLOTS, npg, page_rows, HEAD_DIM), F32)
    state = pltpu.VMEM((DIFF_HEADS * 2 * ds, LANES), F32)
    bias = pltpu.VMEM((DIFF_HEADS * 2 * ds, npg * page_rows), F32)
    n_steps = n_pages // npg
    return pl.pallas_call(
        functools.partial(_decode_kernel, n_seq=db, n_steps=n_steps),
        grid_spec=pltpu.PrefetchScalarGridSpec(
            num_scalar_prefetch=1,
            grid=(db, n_steps),
            in_specs=[tok, new, new] + [const((1, QK_DIM))] * 4 + [const((1, HALF))] + [hbm, hbm],
            out_specs=tok,
            scratch_shapes=[ring, ring, pltpu.SemaphoreType.DMA((2, DECODE_SLOTS)), bias, state, state, state],
        ),
        out_shape=jax.ShapeDtypeStruct((db, ds, HALF), BF16),
        compiler_params=pltpu.CompilerParams(dimension_semantics=("arbitrary", "arbitrary"),
                                             vmem_limit_bytes=VMEM_LIMIT),
        name="sample_diff_attention",
    )(page_table, dq, dk_new, dv_new, *lams, sub_w, cache_k, cache_v)


def _channel_kernel(x_ref, r_ref, dn_ref, p_ref, wo_ref, ln2_ref, wfi_ref, wfo_ref,
                    lnp_ref, wpg_ref, wpp_ref, y_ref):
    mixed = jnp.concatenate([r_ref[...], dn_ref[...]], axis=-1)
    h = x_ref[...] + _dot(mixed, wo_ref[...])
    hn = _rms(h, ln2_ref[...]).astype(BF16)
    ff = jnp.zeros_like(h)
    for c in range(D_FF // FF_CHUNK):
        g = _dot(hn, wfi_ref[:, c * FF_CHUNK:(c + 1) * FF_CHUNK])
        u = _dot(hn, wfi_ref[:, D_FF + c * FF_CHUNK:D_FF + (c + 1) * FF_CHUNK])
        act = (g * _sigmoid(g) * u).astype(BF16)
        ff = ff + _dot(act, wfo_ref[c * FF_CHUNK:(c + 1) * FF_CHUNK, :])
    h = h + ff
    gate = _sigmoid(_dot(_rms(h, lnp_ref[...]).astype(BF16), wpg_ref[...]))
    y_ref[...] = h + gate * _dot(p_ref[...].astype(BF16), wpp_ref[...])


def _channel(x, r, dn, p, w_o, ln2_w, w_ffn_in, w_ffn_out, ln_ple_w, w_ple_gate, w_ple_proj, tm):
    m = x.shape[0]
    tok = lambda width: pl.BlockSpec((tm, width), lambda i: (i, 0))
    return pl.pallas_call(
        _channel_kernel,
        grid=(m // tm,),
        in_specs=[tok(D_MODEL), tok(HALF), tok(HALF), tok(PLE_DIM),
                  _const_spec((D_MODEL, D_MODEL)), _const_spec((1, D_MODEL)),
                  _const_spec((D_MODEL, 2 * D_FF)), _const_spec((D_FF, D_MODEL)),
                  _const_spec((1, D_MODEL)), _const_spec((D_MODEL, D_MODEL)),
                  _const_spec((PLE_DIM, D_MODEL))],
        out_specs=tok(D_MODEL),
        out_shape=jax.ShapeDtypeStruct((m, D_MODEL), F32),
        compiler_params=pltpu.CompilerParams(dimension_semantics=("parallel",),
                                             vmem_limit_bytes=VMEM_LIMIT),
        name="channel",
    )(x, r, dn, p, w_o, ln2_w, w_ffn_in, w_ffn_out, ln_ple_w, w_ple_gate, w_ple_proj)


def kernel(x_prompt, x_sample, cache_k, cache_v, state_ret, page_table, p_prompt, p_sample, ln1_w, w_in, q_norm_w, k_norm_w, lambda_q1, lambda_k1, lambda_q2, lambda_k2, ret_gn_w, ret_gn_b, diff_subln_w, w_o, ln2_w, w_ffn_in, w_ffn_out, ln_ple_w, w_ple_gate, w_ple_proj):
    depth = w_in.shape[0]
    assert depth == 1, "single-layer trunk"
    b, s, _ = x_prompt.shape
    db, ds, _ = x_sample.shape

    row = lambda a: a[0].reshape(1, -1)
    wb = lambda a: a[0].astype(BF16)
    qn_w = jnp.tile(row(q_norm_w), (1, HALF // QK_DIM))
    kn_w = jnp.tile(row(k_norm_w), (1, HALF // QK_DIM))
    lams = (row(lambda_q1), row(lambda_k1), row(lambda_q2), row(lambda_k2))
    proj_w = (row(ln1_w), wb(w_in), qn_w, kn_w)
    chan_w = (wb(w_o), row(ln2_w), wb(w_ffn_in), wb(w_ffn_out), row(ln_ple_w), wb(w_ple_gate), wb(w_ple_proj))
    gn_w, gn_b, sub_w = row(ret_gn_w), row(ret_gn_b), row(diff_subln_w)

    xp = x_prompt.reshape(b * s, D_MODEL)
    rq, rk, rv, rg, dq, dk, dv, dkb, dvt = _project(xp, *proj_w, tm=DIFF_T)
    r, ret_fin = _prompt_retention(rq, rk, rv, rg, gn_w, gn_b, b, s)
    dn = _prompt_diff_attention(dq, dkb, dvt, lams, sub_w, b, s)
    y_prompt = _channel(xp, r, dn, p_prompt[0].reshape(b * s, PLE_DIM), *chan_w, tm=CHAN_TM)

    xs = x_sample.reshape(db * ds, D_MODEL)
    rq_s, rk_s, rv_s, rg_s, dq_s, dk_s, dv_s, _, _ = _project(xs, *proj_w, tm=db * ds)
    tok3 = lambda a: a.reshape(db, ds, HALF)
    r_s, ret_new = _sample_retention(tok3(rq_s), tok3(rk_s), tok3(rv_s), tok3(rg_s), state_ret[0], gn_w, gn_b)
    new4 = lambda a: a.reshape(db, ds, DIFF_HEADS, HEAD_DIM)
    dn_s = _sample_diff_attention(page_table, tok3(dq_s), new4(dk_s), new4(dv_s), cache_k, cache_v, lams, sub_w)
    y_sample = _channel(xs, r_s.reshape(db * ds, HALF), dn_s.reshape(db * ds, HALF),
                        p_sample[0].reshape(db * ds, PLE_DIM), *chan_w, tm=db * ds)

    heads = lambda a, n: a.reshape(1, n, -1, DIFF_HEADS, HEAD_DIM)
    return (y_prompt.reshape(b, s, D_MODEL), y_sample.reshape(db, ds, D_MODEL),
            heads(dk, b), heads(dv, b), ret_fin[None],
            heads(dk_s, db), heads(dv_s, db), ret_new[None])
```

```python
import functools
import math

import jax
import jax.numpy as jnp
from jax import lax
from jax.experimental import pallas as pl
from jax.experimental.pallas import tpu as pltpu

F32 = jnp.float32
BF16 = jnp.bfloat16

D_MODEL = 1024
RET_HEADS = 4
HEAD_DIM = 128
DIFF_HEADS = 4
QK_DIM = 64
HALF = RET_HEADS * HEAD_DIM
PROJ_WIDTH = 7 * HALF
D_FF = 2816
PLE_DIM = 256
RET_CHUNK = 128
PAGE = 128
EPS = 1e-6
LAM_INIT = 0.8 - 0.6 * math.exp(-0.3 * 0)
LOG_G = tuple(math.log1p(-(2.0 ** (-5.0 - h))) for h in range(RET_HEADS))
SLOPES = tuple(2.0 ** (-8.0 / DIFF_HEADS * (h + 1)) for h in range(DIFF_HEADS))
LOG2E = math.log2(math.e)
NEG_BIG = -1e30

LANES = 128
VMEM_LIMIT = 56 * 1024 * 1024
FUSED_VMEM_LIMIT = 61 * 1024 * 1024

FF_CHUNK = 256
RET_SEQS_PER_STEP = 4
DIFF_T = 512
DECODE_PAGES = 16
DECODE_CHAINS = 2
DECODE_SLOTS = 2


def _sigmoid(x):
    return 1.0 / (1.0 + jnp.exp(-x))


def _rms(x, w):
    return x * lax.rsqrt(jnp.mean(x * x, axis=-1, keepdims=True) + EPS) * w


def _dot(a, b):
    return jnp.dot(a, b, preferred_element_type=F32)


def _dot_nt(a, b):
    return lax.dot_general(a, b, (((1,), (1,)), ((), ())), preferred_element_type=F32)


def _lambda(lq1, lk1, lq2, lk2):
    a = jnp.exp(jnp.sum(lq1 * lk1, axis=-1, keepdims=True))
    b = jnp.exp(jnp.sum(lq2 * lk2, axis=-1, keepdims=True))
    return a - b + LAM_INIT


def _const_spec(shape):
    nd = len(shape)
    return pl.BlockSpec(shape, lambda *_: (0,) * nd, pipeline_mode=pl.Buffered(1))


def _seg_rms(y, w):
    lo_mask = lax.broadcasted_iota(jnp.int32, (1, LANES), 1) < QK_DIM
    outs = []
    for g in range(HALF // LANES):
        yg = y[:, g * LANES:(g + 1) * LANES]
        t = yg * yg
        lo = jnp.sum(jnp.where(lo_mask, t, 0.0), axis=-1, keepdims=True)
        hi = jnp.sum(jnp.where(lo_mask, 0.0, t), axis=-1, keepdims=True)
        ms = jnp.where(lo_mask, lo, hi) * (1.0 / QK_DIM)
        outs.append(yg * lax.rsqrt(ms + EPS))
    return jnp.concatenate(outs, axis=-1) * w


def _store_cache_format(ref, y):
    for h in range(DIFF_HEADS):
        ref[:, h, :] = y[:, h * HEAD_DIM:(h + 1) * HEAD_DIM]


def _proj_kernel(x_ref, ln1_ref, w_ref, qn_ref, kn_ref,
                 rq_ref, rk_ref, rv_ref, rg_ref, dq_ref, dk_ref, dv_ref, dkb_ref, dvt_ref):
    xn = _rms(x_ref[...], ln1_ref[...]).astype(BF16)

    def col(i):
        return _dot(xn, w_ref[:, i * HALF:(i + 1) * HALF])

    dq = _seg_rms(col(4), qn_ref[...])
    dq_ref[...] = (dq * (LOG2E * QK_DIM ** -0.5)).astype(BF16)
    dk = _seg_rms(col(5), kn_ref[...])
    dkb_ref[...] = dk.astype(BF16)
    _store_cache_format(dk_ref, dk)
    dv = col(6)
    dvt_ref[0] = dv.T.astype(BF16)
    _store_cache_format(dv_ref, dv)
    rk_ref[...] = (col(1) * (HEAD_DIM ** -0.5)).astype(BF16)
    rq_ref[...] = col(0).astype(BF16)
    rv_ref[...] = col(2).astype(BF16)
    rg_ref[...] = col(3)


def _project(x, ln1_w, w_in, qn_w, kn_w, tm):
    m = x.shape[0]
    tok = lambda width: pl.BlockSpec((tm, width), lambda i: (i, 0))
    half = lambda dt: jax.ShapeDtypeStruct((m, HALF), dt)
    cache_fmt = pl.BlockSpec((tm, DIFF_HEADS, HEAD_DIM), lambda i: (i, 0, 0))
    cache_shape = jax.ShapeDtypeStruct((m, DIFF_HEADS, HEAD_DIM), F32)
    return pl.pallas_call(
        _proj_kernel,
        grid=(m // tm,),
        in_specs=[tok(D_MODEL), _const_spec((1, D_MODEL)), _const_spec((D_MODEL, PROJ_WIDTH)),
                  _const_spec((1, HALF)), _const_spec((1, HALF))],
        out_specs=[tok(HALF)] * 5 + [cache_fmt] * 2 + [tok(HALF), pl.BlockSpec((1, HALF, tm), lambda i: (i, 0, 0))],
        out_shape=[half(BF16), half(BF16), half(BF16), half(F32), half(BF16),
                   cache_shape, cache_shape, half(BF16), jax.ShapeDtypeStruct((m // tm, HALF, tm), BF16)],
        compiler_params=pltpu.CompilerParams(dimension_semantics=("parallel",),
                                             vmem_limit_bytes=VMEM_LIMIT),
        name="proj",
    )(x, ln1_w, w_in, qn_w, kn_w)


def _group_norm_gate(o, g, gn_w, gn_b):
    mu = jnp.mean(o, axis=-1, keepdims=True)
    d = o - mu
    var = jnp.mean(d * d, axis=-1, keepdims=True)
    r = d * lax.rsqrt(var + EPS) * gn_w + gn_b
    return r * (g * _sigmoid(g))


def _ret_kernel(rq_ref, rk_ref, rv_ref, rg_ref, gnw_ref, gnb_ref, r_ref, fin_ref, state):
    c = pl.program_id(1)

    @pl.when(c == 0)
    def _():
        state[...] = jnp.zeros_like(state)

    n = RET_CHUNK
    row = lax.broadcasted_iota(jnp.int32, (n, n), 0).astype(F32)
    col = lax.broadcasted_iota(jnp.int32, (n, n), 1).astype(F32)
    rel = row - col
    for h in range(RET_HEADS):
        hs = slice(h * HEAD_DIM, (h + 1) * HEAD_DIM)
        lg = LOG_G[h]
        decay = jnp.where(rel >= 0, jnp.exp(lg * jnp.maximum(rel, 0.0)), 0.0)
        cross_w = jnp.exp(lg * (row + 1.0))
        k_w = jnp.exp(lg * (n - 1.0 - row))
        for b in range(rq_ref.shape[0]):
            q, k, v = rq_ref[b, :, hs], rk_ref[b, :, hs], rv_ref[b, :, hs]
            st = state[b, h]
            s = _dot_nt(q, k) * decay
            o = _dot(s.astype(BF16), v) + _dot(q, st.astype(BF16)) * cross_w
            kw_t = (k.astype(F32) * k_w).T.astype(BF16)
            state[b, h] = math.exp(lg * n) * st + _dot(kw_t, v)
            r = _group_norm_gate(o, rg_ref[b, :, hs], gnw_ref[:, hs], gnb_ref[:, hs])
            r_ref[b, :, hs] = r.astype(BF16)

    @pl.when(c == pl.num_programs(1) - 1)
    def _():
        fin_ref[...] = state[...]


def _prompt_retention(rq, rk, rv, rg, gn_w, gn_b, batch, seq):
    nb = RET_SEQS_PER_STEP
    tok = pl.BlockSpec((nb, RET_CHUNK, HALF), lambda b, c: (b, c, 0))
    seqs = lambda a: a.reshape(batch, seq, HALF)
    state = (nb, RET_HEADS, HEAD_DIM, HEAD_DIM)
    r, fin = pl.pallas_call(
        _ret_kernel,
        grid=(batch // nb, seq // RET_CHUNK),
        in_specs=[tok, tok, tok, tok, _const_spec((1, HALF)), _const_spec((1, HALF))],
        out_specs=[tok, pl.BlockSpec(state, lambda b, c: (b, 0, 0, 0))],
        out_shape=[jax.ShapeDtypeStruct((batch, seq, HALF), BF16),
                   jax.ShapeDtypeStruct((batch, RET_HEADS, HEAD_DIM, HEAD_DIM), F32)],
        scratch_shapes=[pltpu.VMEM(state, F32)],
        compiler_params=pltpu.CompilerParams(dimension_semantics=("parallel", "arbitrary"),
                                             vmem_limit_bytes=VMEM_LIMIT),
        name="prompt_retention",
    )(seqs(rq), seqs(rk), seqs(rv), seqs(rg), gn_w, gn_b)
    return r.reshape(batch * seq, HALF), fin


def _ret_step_kernel(rq_ref, rk_ref, rv_ref, rg_ref, st_ref, gnw_ref, gnb_ref, r_ref, new_ref):
    n = rq_ref.shape[1]
    row = lax.broadcasted_iota(jnp.int32, (n, LANES), 0).astype(F32)
    for h in range(RET_HEADS):
        hs = slice(h * HEAD_DIM, (h + 1) * HEAD_DIM)
        lg = LOG_G[h]
        q = rq_ref[0, :, hs].astype(F32)
        k = rk_ref[0, :, hs].astype(F32)
        v = rv_ref[0, :, hs].astype(F32)
        st = st_ref[0, h]
        q_pad = jnp.concatenate([q, jnp.zeros((16 - n, HEAD_DIM), F32)], axis=0).astype(BF16)
        o = _dot(q_pad, st.astype(BF16))[:n] * jnp.exp(lg * (row + 1.0))
        new = math.exp(lg * n) * st
        k_t = jnp.concatenate([k, jnp.zeros((8 - n, HEAD_DIM), F32)], axis=0).T
        for m in range(n):
            s_m = jnp.sum(q * k[m:m + 1, :], axis=-1, keepdims=True)
            decay = jnp.where(row >= m, jnp.exp(lg * jnp.maximum(row - m, 0.0)), 0.0)
            o = o + (s_m * decay) * v[m:m + 1, :]
            new = new + math.exp(lg * (n - 1.0 - m)) * (k_t[:, m:m + 1] * v[m:m + 1, :])
        new_ref[0, h] = new
        r = _group_norm_gate(o, rg_ref[0, :, hs], gnw_ref[:, hs], gnb_ref[:, hs])
        r_ref[0, :, hs] = r.astype(BF16)


def _sample_retention(rq, rk, rv, rg, state, gn_w, gn_b):
    db, ds = rq.shape[:2]
    tok = pl.BlockSpec((1, ds, HALF), lambda b: (b, 0, 0))
    st = pl.BlockSpec((1, RET_HEADS, HEAD_DIM, HEAD_DIM), lambda b: (b, 0, 0, 0))
    return pl.pallas_call(
        _ret_step_kernel,
        grid=(db,),
        in_specs=[tok, tok, tok, tok, st, _const_spec((1, HALF)), _const_spec((1, HALF))],
        out_specs=[tok, st],
        out_shape=[jax.ShapeDtypeStruct((db, ds, HALF), BF16),
                   jax.ShapeDtypeStruct(state.shape, F32)],
        compiler_params=pltpu.CompilerParams(dimension_semantics=("parallel",),
                                             vmem_limit_bytes=VMEM_LIMIT),
        name="sample_retention",
    )(rq, rk, rv, rg, state, gn_w, gn_b)


def _split_components(q):
    lo_mask = lax.broadcasted_iota(jnp.int32, q.shape, 1) < QK_DIM
    zero = jnp.zeros_like(q)
    return jnp.concatenate([jnp.where(lo_mask, q, zero), jnp.where(lo_mask, zero, q)], axis=0)


def _sub_norm(acc, l, lam, w):
    t = acc.shape[0] // 2
    o = acc / l
    d = o[:t] - lam * o[t:]
    return d * lax.rsqrt(jnp.mean(d * d, axis=-1, keepdims=True) + EPS) * w * (1.0 - LAM_INIT)


def _diff_kernel(q_ref, k_ref, vt_ref, lq1_ref, lk1_ref, lq2_ref, lk2_ref, sub_ref, o_ref):
    i = pl.program_id(1)
    t = DIFF_T
    lam = _lambda(lq1_ref[...], lk1_ref[...], lq2_ref[...], lk2_ref[...])
    k_idx = lax.broadcasted_iota(jnp.int32, (t, LANES), 0).astype(F32)
    key = lax.broadcasted_iota(jnp.int32, (t, 2 * t), 0)
    col = lax.broadcasted_iota(jnp.int32, (t, 2 * t), 1)
    causal = key <= jnp.where(col >= t, col - t, col)
    heads = [slice(h * HEAD_DIM, (h + 1) * HEAD_DIM) for h in range(DIFF_HEADS)]
    qqs = [_split_components(q_ref[:, hs]) for hs in heads]

    def block(j, carry, masked):
        start = pl.multiple_of(j * t, t)
        k_pos = k_idx + (j * t).astype(F32)
        out = []
        for h, hs in enumerate(heads):
            m, l, acc = carry[h]
            k = k_ref[pl.ds(start, t), hs]
            vt = vt_ref[j, hs, :]
            bias = (SLOPES[h] * LOG2E) * k_pos
            s = _dot_nt(k, qqs[h]) + jnp.concatenate([bias] * (2 * t // LANES), axis=1)
            if masked:
                s = jnp.where(causal, s, -jnp.inf)
            m_new = jnp.maximum(m, jnp.max(s, axis=0, keepdims=True))
            alpha = jnp.exp2(m - m_new)
            p = jnp.exp2(s - m_new)
            l = alpha * l + jnp.sum(p, axis=0, keepdims=True)
            acc = alpha * acc + _dot(vt, p.astype(BF16))
            out.append((m_new, l, acc))
        return tuple(out)

    init = tuple((jnp.full((1, 2 * t), NEG_BIG, F32), jnp.zeros((1, 2 * t), F32),
                  jnp.zeros((HEAD_DIM, 2 * t), F32)) for _ in heads)
    carry = lax.fori_loop(0, i, functools.partial(block, masked=False), init)
    carry = block(i, carry, masked=True)
    for (_, l, acc), hs in zip(carry, heads):
        o = acc / l
        d = o[:, :t] - lam * o[:, t:]
        dn = d * lax.rsqrt(jnp.mean(d * d, axis=0, keepdims=True) + EPS)
        o_ref[:, hs] = (dn.T * sub_ref[:, hs] * (1.0 - LAM_INIT)).astype(BF16)


def _prompt_diff_attention(dq, dk, dvt, lams, sub_w, batch, seq):
    nq = seq // DIFF_T
    qspec = pl.BlockSpec((DIFF_T, HALF), lambda b, i: (b * nq + i, 0))
    kspec = pl.BlockSpec((seq, HALF), lambda b, i: (b, 0))
    vspec = pl.BlockSpec((nq, HALF, DIFF_T), lambda b, i: (b, 0, 0))
    lspec = _const_spec((1, QK_DIM))
    return pl.pallas_call(
        _diff_kernel,
        grid=(batch, nq),
        in_specs=[qspec, kspec, vspec, lspec, lspec, lspec, lspec, _const_spec((1, HALF))],
        out_specs=qspec,
        out_shape=jax.ShapeDtypeStruct((batch * seq, HALF), BF16),
        compiler_params=pltpu.CompilerParams(dimension_semantics=("parallel", "arbitrary"),
                                             vmem_limit_bytes=VMEM_LIMIT),
        name="prompt_diff_attention",
    )(dq, dk, dvt, *lams, sub_w)


class _PagedDecoder:
    def __init__(self, pt_ref, q_ref, kn_ref, vn_ref, lam_refs, sub_ref, ck_hbm, cv_hbm, o_ref,
                 kbuf, vbuf, sems, bias_s, m_s, l_s, acc_s, n_groups):
        self.pt_ref, self.q_ref, self.kn_ref, self.vn_ref = pt_ref, q_ref, kn_ref, vn_ref
        self.lam_refs, self.sub_ref, self.ck_hbm, self.cv_hbm, self.o_ref = lam_refs, sub_ref, ck_hbm, cv_hbm, o_ref
        self.kbuf, self.vbuf, self.sems = kbuf, vbuf, sems
        self.bias_s, self.m_s, self.l_s, self.acc_s = bias_s, m_s, l_s, acc_s
        self.n = q_ref.shape[1]
        self.rows = DIFF_HEADS * 2 * self.n
        self.page_rows = PAGE * DIFF_HEADS
        self.past = n_groups * DECODE_PAGES * PAGE
        r1 = lax.broadcasted_iota(jnp.int32, (self.rows, 1), 0)
        self.head = r1 // (2 * self.n)
        self.qi = r1 % self.n
        self.slope = jnp.zeros((self.rows, 1), F32)
        for h in range(DIFF_HEADS):
            self.slope = jnp.where(self.head == h, SLOPES[h] * LOG2E, self.slope)

    def _copies(self, seq, group):
        slot = group % DECODE_SLOTS
        copies = []
        for p in range(DECODE_PAGES):
            page = self.pt_ref[seq, group * DECODE_PAGES + p]
            copies.append(pltpu.make_async_copy(self.ck_hbm.at[page], self.kbuf.at[slot, p], self.sems.at[0, slot]))
            copies.append(pltpu.make_async_copy(self.cv_hbm.at[page], self.vbuf.at[slot, p], self.sems.at[1, slot]))
        return copies

    def start(self, seq, group):
        for c in self._copies(seq, group):
            c.start()

    def wait(self, seq, group):
        for c in self._copies(seq, group):
            c.wait()

    def init_bias(self):
        col = lax.broadcasted_iota(jnp.int32, self.bias_s.shape, 1)
        dist = (self.past + self.qi - col // DIFF_HEADS).astype(F32)
        self.bias_s[...] = jnp.where(col % DIFF_HEADS == self.head, -self.slope * dist, -jnp.inf)

    def begin_sequence(self):
        self.m_s[...] = jnp.full_like(self.m_s, NEG_BIG)
        self.l_s[...] = jnp.zeros_like(self.l_s)
        self.acc_s[...] = jnp.zeros_like(self.acc_s)
        q = self.q_ref[0]
        self.qq = jnp.concatenate([_split_components(q[:, h * HEAD_DIM:(h + 1) * HEAD_DIM].astype(F32))
                                   for h in range(DIFF_HEADS)], axis=0).astype(BF16)

    def scores(self, g):
        slot, per_chain = g % DECODE_SLOTS, DECODE_PAGES // DECODE_CHAINS
        return [jnp.concatenate([_dot_nt(self.qq, self.kbuf[slot, p].astype(BF16))
                                 for p in range(c * per_chain, (c + 1) * per_chain)], axis=-1)
                for c in range(DECODE_CHAINS)]

    def fold(self, g, scores):
        slot, page_rows = g % DECODE_SLOTS, self.page_rows
        per_chain = DECODE_PAGES // DECODE_CHAINS
        chains = []
        for c, s in enumerate(scores):
            pages = range(c * per_chain, (c + 1) * per_chain)
            s = s + jnp.concatenate([self.bias_s[...] + self.slope * float((g * DECODE_PAGES + p) * PAGE)
                                     for p in pages], axis=-1)
            m_c = jnp.max(s, axis=-1, keepdims=True)
            pb = jnp.exp2(s - m_c)
            l_c = jnp.sum(pb, axis=-1, keepdims=True)
            pb = pb.astype(BF16)
            pv_c = _dot(pb[:, :page_rows], self.vbuf[slot, pages[0]].astype(BF16))
            for idx, pg in enumerate(pages[1:], start=1):
                pv_c = pv_c + _dot(pb[:, idx * page_rows:(idx + 1) * page_rows], self.vbuf[slot, pg].astype(BF16))
            chains.append((m_c, l_c, pv_c))
        m = self.m_s[:, :1]
        m_new = m
        for m_c, _, _ in chains:
            m_new = jnp.maximum(m_new, m_c)
        alpha = jnp.exp2(m - m_new)
        l_new = alpha * self.l_s[:, :1]
        acc = alpha * self.acc_s[...]
        for m_c, l_c, pv_c in chains:
            w = jnp.exp2(m_c - m_new)
            l_new = l_new + w * l_c
            acc = acc + w * pv_c
        self.m_s[...] = jnp.broadcast_to(m_new, (self.rows, LANES))
        self.l_s[...] = jnp.broadcast_to(l_new, (self.rows, LANES))
        self.acc_s[...] = acc

    def finish(self):
        n = self.n
        lam = _lambda(*(ref[...] for ref in self.lam_refs))
        r8 = lax.broadcasted_iota(jnp.int32, (2 * n, 1), 0)
        qi = jnp.where(r8 >= n, r8 - n, r8)
        for h in range(DIFF_HEADS):
            hs = slice(h * HEAD_DIM, (h + 1) * HEAD_DIM)
            hr = slice(h * 2 * n, (h + 1) * 2 * n)
            qq = _split_components(self.q_ref[0, :, hs].astype(F32))
            kn = self.kn_ref[:, h, :]
            vn = self.vn_ref[:, h, :]
            m, l, acc = self.m_s[hr, :1], self.l_s[hr, :1], self.acc_s[hr, :]
            s_cols = []
            for t in range(n):
                s_t = jnp.sum(qq * kn[t:t + 1, :], axis=-1, keepdims=True)
                s_t = s_t - SLOPES[h] * LOG2E * (qi - t).astype(F32)
                s_cols.append(jnp.where(qi >= t, s_t, -jnp.inf))
            m_new = m
            for s_t in s_cols:
                m_new = jnp.maximum(m_new, s_t)
            alpha = jnp.exp2(m - m_new)
            l = alpha * l
            acc = alpha * acc
            for t in range(n):
                p_t = jnp.exp2(s_cols[t] - m_new)
                l = l + p_t
                acc = acc + p_t * vn[t:t + 1, :]
            self.o_ref[0, :, hs] = _sub_norm(acc, l, lam, self.sub_ref[:, hs]).astype(BF16)


def _channel_phases(x_ref, r_ref, dn_ref, p_ref, wo_ref, ln2_ref, wfi_ref, wfo_ref, lnp_ref, wpg_ref, wpp_ref,
                    y_ref):
    st = {}

    def attention_out():
        mixed = jnp.concatenate([r_ref[...], dn_ref[...]], axis=-1)
        st["h"] = x_ref[...] + _dot(mixed, wo_ref[...])
        st["hn"] = _rms(st["h"], ln2_ref[...]).astype(BF16)

    def ffn_chunk(c):
        g = _dot(st["hn"], wfi_ref[:, c * FF_CHUNK:(c + 1) * FF_CHUNK])
        u = _dot(st["hn"], wfi_ref[:, D_FF + c * FF_CHUNK:D_FF + (c + 1) * FF_CHUNK])
        act = (g * _sigmoid(g) * u).astype(BF16)
        part = _dot(act, wfo_ref[c * FF_CHUNK:(c + 1) * FF_CHUNK, :])
        st["ff"] = part if c == 0 else st["ff"] + part

    def embedding_gate():
        h = st["h"] + st["ff"]
        gate = _sigmoid(_dot(_rms(h, lnp_ref[...]).astype(BF16), wpg_ref[...]))
        y_ref[...] = h + gate * _dot(p_ref[...].astype(BF16), wpp_ref[...])

    return ([attention_out] + [functools.partial(ffn_chunk, c) for c in range(D_FF // FF_CHUNK)]
            + [embedding_gate])


def _channel_kernel(*refs):
    for phase in _channel_phases(*refs):
        phase()


def _channel_specs(tm, index):
    tok = lambda width: pl.BlockSpec((tm, width), index)
    return [tok(D_MODEL), tok(HALF), tok(HALF), tok(PLE_DIM),
            _const_spec((D_MODEL, D_MODEL)), _const_spec((1, D_MODEL)),
            _const_spec((D_MODEL, 2 * D_FF)), _const_spec((D_FF, D_MODEL)),
            _const_spec((1, D_MODEL)), _const_spec((D_MODEL, D_MODEL)),
            _const_spec((PLE_DIM, D_MODEL))], tok(D_MODEL)


def _channel(x, r, dn, p, chan_w, tm):
    m = x.shape[0]
    in_specs, out_spec = _channel_specs(tm, lambda i: (i, 0))
    return pl.pallas_call(
        _channel_kernel,
        grid=(m // tm,),
        in_specs=in_specs,
        out_specs=out_spec,
        out_shape=jax.ShapeDtypeStruct((m, D_MODEL), F32),
        compiler_params=pltpu.CompilerParams(dimension_semantics=("parallel",),
                                             vmem_limit_bytes=VMEM_LIMIT),
        name="channel",
    )(x, r, dn, p, *chan_w)


def _channel_decode_kernel(pt_ref, x_ref, r_ref, dn_ref, p_ref, wo_ref, ln2_ref, wfi_ref, wfo_ref, lnp_ref, wpg_ref,
                           wpp_ref, q_ref, kn_ref, vn_ref, lq1_ref, lk1_ref, lq2_ref, lk2_ref, sub_ref, ck_hbm,
                           cv_hbm, y_ref, o_ref, kbuf, vbuf, sems, bias_s, m_s, l_s, acc_s, *, n_seq, n_groups):
    i = pl.program_id(0)
    ahead = DECODE_SLOTS - 1
    chan_refs = (x_ref, r_ref, dn_ref, p_ref, wo_ref, ln2_ref, wfi_ref, wfo_ref, lnp_ref, wpg_ref, wpp_ref)
    dec = _PagedDecoder(pt_ref, q_ref, kn_ref, vn_ref, (lq1_ref, lk1_ref, lq2_ref, lk2_ref), sub_ref, ck_hbm,
                        cv_hbm, o_ref, kbuf, vbuf, sems, bias_s, m_s, l_s, acc_s, n_groups)

    @pl.when(i == 0)
    def _():
        dec.init_bias()
        for g in range(ahead):
            dec.start(0, g)

    dec.begin_sequence()
    phases = _channel_phases(*chan_refs, y_ref)
    per_group = -(-len(phases) // n_groups)
    for g in range(n_groups):
        nxt = g + ahead
        if nxt < n_groups:
            dec.start(i, nxt)
        else:
            pl.when(i + 1 < n_seq)(functools.partial(dec.start, i + 1, nxt - n_groups))
        dec.wait(i, g)
        scores = dec.scores(g)
        for phase in phases[g * per_group:(g + 1) * per_group]:
            phase()
        dec.fold(g, scores)
    dec.finish()


def _channel_and_decode(x, r, dn, p, chan_w, page_table, dq, dk_new, dv_new, cache_k, cache_v, lams, sub_w):
    m = x.shape[0]
    db, ds = dq.shape[:2]
    tm = m // db
    assert tm * db == m and tm % 256 == 0, (m, db)
    n_groups = page_table.shape[1] // DECODE_PAGES
    assert n_groups * DECODE_PAGES == page_table.shape[1] and n_groups % DECODE_SLOTS == 0
    chan_in, chan_out = _channel_specs(tm, lambda i, pt: (i, 0))
    tok = pl.BlockSpec((1, ds, HALF), lambda i, pt: (i, 0, 0))
    new = pl.BlockSpec((None, ds, DIFF_HEADS, HEAD_DIM), lambda i, pt: (i, 0, 0, 0))
    page_rows = PAGE * DIFF_HEADS
    cache_k, cache_v = (c.reshape(-1, page_rows, HEAD_DIM) for c in (cache_k, cache_v))
    hbm = pl.BlockSpec(memory_space=pl.ANY)
    ring = pltpu.VMEM((DECODE_SLOTS, DECODE_PAGES, page_rows, HEAD_DIM), F32)
    state = pltpu.VMEM((DIFF_HEADS * 2 * ds, LANES), F32)
    bias = pltpu.VMEM((DIFF_HEADS * 2 * ds, page_rows), F32)
    return pl.pallas_call(
        functools.partial(_channel_decode_kernel, n_seq=db, n_groups=n_groups),
        grid_spec=pltpu.PrefetchScalarGridSpec(
            num_scalar_prefetch=1,
            grid=(db,),
            in_specs=chan_in + [tok, new, new] + [_const_spec((1, QK_DIM))] * 4 + [_const_spec((1, HALF)), hbm, hbm],
            out_specs=[chan_out, tok],
            scratch_shapes=[ring, ring, pltpu.SemaphoreType.DMA((2, DECODE_SLOTS)), bias, state, state, state],
        ),
        out_shape=[jax.ShapeDtypeStruct((m, D_MODEL), F32), jax.ShapeDtypeStruct((db, ds, HALF), BF16)],
        compiler_params=pltpu.CompilerParams(dimension_semantics=("arbitrary",),
                                             vmem_limit_bytes=FUSED_VMEM_LIMIT),
        name="channel_and_decode",
    )(page_table, x, r, dn, p, *chan_w, dq, dk_new, dv_new, *lams, sub_w, cache_k, cache_v)


def kernel(x_prompt, x_sample, cache_k, cache_v, state_ret, page_table, p_prompt, p_sample, ln1_w, w_in, q_norm_w, k_norm_w, lambda_q1, lambda_k1, lambda_q2, lambda_k2, ret_gn_w, ret_gn_b, diff_subln_w, w_o, ln2_w, w_ffn_in, w_ffn_out, ln_ple_w, w_ple_gate, w_ple_proj):
    depth = w_in.shape[0]
    assert depth == 1, "single-layer trunk"
    b, s, _ = x_prompt.shape
    db, ds, _ = x_sample.shape

    row = lambda a: a[0].reshape(1, -1)
    wb = lambda a: a[0].astype(BF16)
    qn_w = jnp.tile(row(q_norm_w), (1, HALF // QK_DIM))
    kn_w = jnp.tile(row(k_norm_w), (1, HALF // QK_DIM))
    lams = (row(lambda_q1), row(lambda_k1), row(lambda_q2), row(lambda_k2))
    proj_w = (row(ln1_w), wb(w_in), qn_w, kn_w)
    chan_w = (wb(w_o), row(ln2_w), wb(w_ffn_in), wb(w_ffn_out), row(ln_ple_w), wb(w_ple_gate), wb(w_ple_proj))
    gn_w, gn_b, sub_w = row(ret_gn_w), row(ret_gn_b), row(diff_subln_w)

    xp = x_prompt.reshape(b * s, D_MODEL)
    rq, rk, rv, rg, dq, dk, dv, dkb, dvt = _project(xp, *proj_w, tm=DIFF_T)
    r, ret_fin = _prompt_retention(rq, rk, rv, rg, gn_w, gn_b, b, s)
    dn = _prompt_diff_attention(dq, dkb, dvt, lams, sub_w, b, s)

    xs = x_sample.reshape(db * ds, D_MODEL)
    rq_s, rk_s, rv_s, rg_s, dq_s, dk_s, dv_s, _, _ = _project(xs, *proj_w, tm=db * ds)
    tok3 = lambda a: a.reshape(db, ds, HALF)
    r_s, ret_new = _sample_retention(tok3(rq_s), tok3(rk_s), tok3(rv_s), tok3(rg_s), state_ret[0], gn_w, gn_b)
    new4 = lambda a: a.reshape(db, ds, DIFF_HEADS, HEAD_DIM)
    y_prompt, dn_s = _channel_and_decode(xp, r, dn, p_prompt[0].reshape(b * s, PLE_DIM), chan_w, page_table,
                                         tok3(dq_s), new4(dk_s), new4(dv_s), cache_k, cache_v, lams, sub_w)
    y_sample = _channel(xs, r_s.reshape(db * ds, HALF), dn_s.reshape(db * ds, HALF),
                        p_sample[0].reshape(db * ds, PLE_DIM), chan_w, tm=db * ds)

    heads = lambda a, n: a.reshape(1, n, -1, DIFF_HEADS, HEAD_DIM)
    return (y_prompt.reshape(b, s, D_MODEL), y_sample.reshape(db, ds, D_MODEL),
            heads(dk, b), heads(dv, b), ret_fin[None],
            heads(dk_s, db), heads(dv_s, db), ret_new[None])
```

```python
import functools
import math

import jax
import jax.numpy as jnp
from jax import lax
from jax.experimental import pallas as pl
from jax.experimental.pallas import tpu as pltpu

F32 = jnp.float32
BF16 = jnp.bfloat16

D_MODEL = 1024
RET_HEADS = 4
HEAD_DIM = 128
DIFF_HEADS = 4
QK_DIM = 64
HALF = RET_HEADS * HEAD_DIM
PROJ_WIDTH = 7 * HALF
D_FF = 2816
PLE_DIM = 256
RET_CHUNK = 128
PAGE = 128
EPS = 1e-6
LAM_INIT = 0.8 - 0.6 * math.exp(-0.3 * 0)
LOG_G = tuple(math.log1p(-(2.0 ** (-5.0 - h))) for h in range(RET_HEADS))
SLOPES = tuple(2.0 ** (-8.0 / DIFF_HEADS * (h + 1)) for h in range(DIFF_HEADS))
LOG2E = math.log2(math.e)
NEG_BIG = -1e30

LANES = 128
MXU_TILE = 256
VMEM_LIMIT = 56 * 1024 * 1024
FUSED_VMEM_LIMIT = 61 * 1024 * 1024

FF_CHUNK = 256
RET_SEQS_PER_STEP = 4
DIFF_T = 512
DECODE_PAGES = 16
DECODE_CHAINS = 2
DECODE_SLOTS = 2


def _sigmoid(x):
    return 1.0 / (1.0 + jnp.exp(-x))


def _rms(x, w):
    return x * lax.rsqrt(jnp.mean(x * x, axis=-1, keepdims=True) + EPS) * w


def _dot(a, b):
    return jnp.dot(a, b, preferred_element_type=F32)


def _dot_nt(a, b):
    return lax.dot_general(a, b, (((1,), (1,)), ((), ())), preferred_element_type=F32)


def _lambda(lq1, lk1, lq2, lk2):
    a = jnp.exp(jnp.sum(lq1 * lk1, axis=-1, keepdims=True))
    b = jnp.exp(jnp.sum(lq2 * lk2, axis=-1, keepdims=True))
    return a - b + LAM_INIT


def _const_spec(shape):
    nd = len(shape)
    return pl.BlockSpec(shape, lambda *_: (0,) * nd, pipeline_mode=pl.Buffered(1))


def _seg_rms(y, w):
    lo_mask = lax.broadcasted_iota(jnp.int32, (1, LANES), 1) < QK_DIM
    outs = []
    for g in range(HALF // LANES):
        yg = y[:, g * LANES:(g + 1) * LANES]
        t = yg * yg
        lo = jnp.sum(jnp.where(lo_mask, t, 0.0), axis=-1, keepdims=True)
        hi = jnp.sum(jnp.where(lo_mask, 0.0, t), axis=-1, keepdims=True)
        ms = jnp.where(lo_mask, lo, hi) * (1.0 / QK_DIM)
        outs.append(yg * lax.rsqrt(ms + EPS))
    return jnp.concatenate(outs, axis=-1) * w


def _store_cache_format(ref, y):
    for h in range(DIFF_HEADS):
        ref[:, h, :] = y[:, h * HEAD_DIM:(h + 1) * HEAD_DIM]


def _proj_kernel(x_ref, ln1_ref, w_ref, qn_ref, kn_ref,
                 rq_ref, rk_ref, rv_ref, rg_ref, dq_ref, dk_ref, dv_ref, dkb_ref, dvt_ref):
    xn = _rms(x_ref[...], ln1_ref[...]).astype(BF16)

    def col(i):
        return _dot(xn, w_ref[:, i * HALF:(i + 1) * HALF])

    dq = _seg_rms(col(4), qn_ref[...])
    dq_ref[...] = (dq * (LOG2E * QK_DIM ** -0.5)).astype(BF16)
    dk = _seg_rms(col(5), kn_ref[...])
    dkb_ref[...] = dk.astype(BF16)
    _store_cache_format(dk_ref, dk)
    dv = col(6)
    dvt_ref[0] = dv.T.astype(BF16)
    _store_cache_format(dv_ref, dv)
    rk_ref[...] = (col(1) * (HEAD_DIM ** -0.5)).astype(BF16)
    rq_ref[...] = col(0).astype(BF16)
    rv_ref[...] = col(2).astype(BF16)
    rg_ref[...] = col(3)


def _project(x, ln1_w, w_in, qn_w, kn_w, tm):
    m = x.shape[0]
    tok = lambda width: pl.BlockSpec((tm, width), lambda i: (i, 0))
    half = lambda dt: jax.ShapeDtypeStruct((m, HALF), dt)
    cache_fmt = pl.BlockSpec((tm, DIFF_HEADS, HEAD_DIM), lambda i: (i, 0, 0))
    cache_shape = jax.ShapeDtypeStruct((m, DIFF_HEADS, HEAD_DIM), F32)
    return pl.pallas_call(
        _proj_kernel,
        grid=(m // tm,),
        in_specs=[tok(D_MODEL), _const_spec((1, D_MODEL)), _const_spec((D_MODEL, PROJ_WIDTH)),
                  _const_spec((1, HALF)), _const_spec((1, HALF))],
        out_specs=[tok(HALF)] * 5 + [cache_fmt] * 2 + [tok(HALF), pl.BlockSpec((1, HALF, tm), lambda i: (i, 0, 0))],
        out_shape=[half(BF16), half(BF16), half(BF16), half(F32), half(BF16),
                   cache_shape, cache_shape, half(BF16), jax.ShapeDtypeStruct((m // tm, HALF, tm), BF16)],
        compiler_params=pltpu.CompilerParams(dimension_semantics=("parallel",),
                                             vmem_limit_bytes=VMEM_LIMIT),
        name="proj",
    )(x, ln1_w, w_in, qn_w, kn_w)


def _group_norm_gate(o, g, gn_w, gn_b):
    mu = jnp.mean(o, axis=-1, keepdims=True)
    d = o - mu
    var = jnp.mean(d * d, axis=-1, keepdims=True)
    r = d * lax.rsqrt(var + EPS) * gn_w + gn_b
    return r * (g * _sigmoid(g))


def _ret_kernel(rq_ref, rk_ref, rv_ref, rg_ref, gnw_ref, gnb_ref, r_ref, fin_ref, state):
    c = pl.program_id(1)

    @pl.when(c == 0)
    def _():
        state[...] = jnp.zeros_like(state)

    n = RET_CHUNK
    row = lax.broadcasted_iota(jnp.int32, (n, n), 0).astype(F32)
    col = lax.broadcasted_iota(jnp.int32, (n, n), 1).astype(F32)
    rel = row - col
    for h in range(RET_HEADS):
        hs = slice(h * HEAD_DIM, (h + 1) * HEAD_DIM)
        lg = LOG_G[h]
        decay = jnp.where(rel >= 0, jnp.exp(lg * jnp.maximum(rel, 0.0)), 0.0)
        cross_w = jnp.exp(lg * (row + 1.0))
        k_w = jnp.exp(lg * (n - 1.0 - row))
        for b in range(rq_ref.shape[0]):
            q, k, v = rq_ref[b, :, hs], rk_ref[b, :, hs], rv_ref[b, :, hs]
            st = state[b, h]
            s = _dot_nt(q, k) * decay
            o = _dot(s.astype(BF16), v) + _dot(q, st.astype(BF16)) * cross_w
            kw_t = (k.astype(F32) * k_w).T.astype(BF16)
            state[b, h] = math.exp(lg * n) * st + _dot(kw_t, v)
            r = _group_norm_gate(o, rg_ref[b, :, hs], gnw_ref[:, hs], gnb_ref[:, hs])
            r_ref[b, :, hs] = r.astype(BF16)

    @pl.when(c == pl.num_programs(1) - 1)
    def _():
        fin_ref[...] = state[...]


def _prompt_retention(rq, rk, rv, rg, gn_w, gn_b, batch, seq):
    nb = RET_SEQS_PER_STEP
    tok = pl.BlockSpec((nb, RET_CHUNK, HALF), lambda b, c: (b, c, 0))
    seqs = lambda a: a.reshape(batch, seq, HALF)
    state = (nb, RET_HEADS, HEAD_DIM, HEAD_DIM)
    r, fin = pl.pallas_call(
        _ret_kernel,
        grid=(batch // nb, seq // RET_CHUNK),
        in_specs=[tok, tok, tok, tok, _const_spec((1, HALF)), _const_spec((1, HALF))],
        out_specs=[tok, pl.BlockSpec(state, lambda b, c: (b, 0, 0, 0))],
        out_shape=[jax.ShapeDtypeStruct((batch, seq, HALF), BF16),
                   jax.ShapeDtypeStruct((batch, RET_HEADS, HEAD_DIM, HEAD_DIM), F32)],
        scratch_shapes=[pltpu.VMEM(state, F32)],
        compiler_params=pltpu.CompilerParams(dimension_semantics=("parallel", "arbitrary"),
                                             vmem_limit_bytes=VMEM_LIMIT),
        name="prompt_retention",
    )(seqs(rq), seqs(rk), seqs(rv), seqs(rg), gn_w, gn_b)
    return r.reshape(batch * seq, HALF), fin


def _ret_step_kernel(rq_ref, rk_ref, rv_ref, rg_ref, st_ref, gnw_ref, gnb_ref, r_ref, new_ref):
    n = rq_ref.shape[1]
    row = lax.broadcasted_iota(jnp.int32, (n, LANES), 0).astype(F32)
    for h in range(RET_HEADS):
        hs = slice(h * HEAD_DIM, (h + 1) * HEAD_DIM)
        lg = LOG_G[h]
        q = rq_ref[0, :, hs].astype(F32)
        k = rk_ref[0, :, hs].astype(F32)
        v = rv_ref[0, :, hs].astype(F32)
        st = st_ref[0, h]
        q_pad = jnp.concatenate([q, jnp.zeros((16 - n, HEAD_DIM), F32)], axis=0).astype(BF16)
        o = _dot(q_pad, st.astype(BF16))[:n] * jnp.exp(lg * (row + 1.0))
        new = math.exp(lg * n) * st
        k_t = jnp.concatenate([k, jnp.zeros((8 - n, HEAD_DIM), F32)], axis=0).T
        for m in range(n):
            s_m = jnp.sum(q * k[m:m + 1, :], axis=-1, keepdims=True)
            decay = jnp.where(row >= m, jnp.exp(lg * jnp.maximum(row - m, 0.0)), 0.0)
            o = o + (s_m * decay) * v[m:m + 1, :]
            new = new + math.exp(lg * (n - 1.0 - m)) * (k_t[:, m:m + 1] * v[m:m + 1, :])
        new_ref[0, h] = new
        r = _group_norm_gate(o, rg_ref[0, :, hs], gnw_ref[:, hs], gnb_ref[:, hs])
        r_ref[0, :, hs] = r.astype(BF16)


def _sample_retention(rq, rk, rv, rg, state, gn_w, gn_b):
    db, ds = rq.shape[:2]
    tok = pl.BlockSpec((1, ds, HALF), lambda b: (b, 0, 0))
    st = pl.BlockSpec((1, RET_HEADS, HEAD_DIM, HEAD_DIM), lambda b: (b, 0, 0, 0))
    return pl.pallas_call(
        _ret_step_kernel,
        grid=(db,),
        in_specs=[tok, tok, tok, tok, st, _const_spec((1, HALF)), _const_spec((1, HALF))],
        out_specs=[tok, st],
        out_shape=[jax.ShapeDtypeStruct((db, ds, HALF), BF16),
                   jax.ShapeDtypeStruct(state.shape, F32)],
        compiler_params=pltpu.CompilerParams(dimension_semantics=("parallel",),
                                             vmem_limit_bytes=VMEM_LIMIT),
        name="sample_retention",
    )(rq, rk, rv, rg, state, gn_w, gn_b)


def _split_components(q):
    lo_mask = lax.broadcasted_iota(jnp.int32, q.shape, 1) < QK_DIM
    zero = jnp.zeros_like(q)
    return jnp.concatenate([jnp.where(lo_mask, q, zero), jnp.where(lo_mask, zero, q)], axis=0)


def _sub_norm(acc, l, lam, w):
    t = acc.shape[0] // 2
    o = acc / l
    d = o[:t] - lam * o[t:]
    return d * lax.rsqrt(jnp.mean(d * d, axis=-1, keepdims=True) + EPS) * w * (1.0 - LAM_INIT)


def _diff_kernel(q_ref, k_ref, vt_ref, lq1_ref, lk1_ref, lq2_ref, lk2_ref, sub_ref, o_ref):
    i = pl.program_id(1)
    t = DIFF_T
    lam = _lambda(lq1_ref[...], lk1_ref[...], lq2_ref[...], lk2_ref[...])
    k_idx = lax.broadcasted_iota(jnp.int32, (t, LANES), 0).astype(F32)
    key = lax.broadcasted_iota(jnp.int32, (t, 2 * t), 0)
    col = lax.broadcasted_iota(jnp.int32, (t, 2 * t), 1)
    causal = key <= jnp.where(col >= t, col - t, col)
    heads = [slice(h * HEAD_DIM, (h + 1) * HEAD_DIM) for h in range(DIFF_HEADS)]
    qqs = [_split_components(q_ref[:, hs]) for hs in heads]

    def block(j, carry, masked):
        start = pl.multiple_of(j * t, t)
        k_pos = k_idx + (j * t).astype(F32)
        out = []
        for h, hs in enumerate(heads):
            m, l, acc = carry[h]
            k = k_ref[pl.ds(start, t), hs]
            vt = vt_ref[j, hs, :]
            bias = (SLOPES[h] * LOG2E) * k_pos
            s = _dot_nt(k, qqs[h]) + jnp.concatenate([bias] * (2 * t // LANES), axis=1)
            if masked:
                s = jnp.where(causal, s, -jnp.inf)
            m_new = jnp.maximum(m, jnp.max(s, axis=0, keepdims=True))
            alpha = jnp.exp2(m - m_new)
            p = jnp.exp2(s - m_new)
            l = alpha * l + jnp.sum(p, axis=0, keepdims=True)
            acc = alpha * acc + _dot(vt, p.astype(BF16))
            out.append((m_new, l, acc))
        return tuple(out)

    init = tuple((jnp.full((1, 2 * t), NEG_BIG, F32), jnp.zeros((1, 2 * t), F32),
                  jnp.zeros((HEAD_DIM, 2 * t), F32)) for _ in heads)
    carry = lax.fori_loop(0, i, functools.partial(block, masked=False), init)
    carry = block(i, carry, masked=True)
    for (_, l, acc), hs in zip(carry, heads):
        o = acc / l
        d = o[:, :t] - lam * o[:, t:]
        dn = d * lax.rsqrt(jnp.mean(d * d, axis=0, keepdims=True) + EPS)
        o_ref[:, hs] = (dn.T * sub_ref[:, hs] * (1.0 - LAM_INIT)).astype(BF16)


def _prompt_diff_attention(dq, dk, dvt, lams, sub_w, batch, seq):
    nq = seq // DIFF_T
    qspec = pl.BlockSpec((DIFF_T, HALF), lambda b, i: (b * nq + i, 0))
    kspec = pl.BlockSpec((seq, HALF), lambda b, i: (b, 0))
    vspec = pl.BlockSpec((nq, HALF, DIFF_T), lambda b, i: (b, 0, 0))
    lspec = _const_spec((1, QK_DIM))
    return pl.pallas_call(
        _diff_kernel,
        grid=(batch, nq),
        in_specs=[qspec, kspec, vspec, lspec, lspec, lspec, lspec, _const_spec((1, HALF))],
        out_specs=qspec,
        out_shape=jax.ShapeDtypeStruct((batch * seq, HALF), BF16),
        compiler_params=pltpu.CompilerParams(dimension_semantics=("parallel", "arbitrary"),
                                             vmem_limit_bytes=VMEM_LIMIT),
        name="prompt_diff_attention",
    )(dq, dk, dvt, *lams, sub_w)


class _PagedDecoder:
    def __init__(self, pt_ref, q_ref, kn_ref, vn_ref, lam_refs, sub_ref, ck_hbm, cv_hbm, o_ref,
                 kbuf, vbuf, sems, bias_s, m_s, l_s, acc_s, n_groups):
        self.pt_ref, self.q_ref, self.kn_ref, self.vn_ref = pt_ref, q_ref, kn_ref, vn_ref
        self.lam_refs, self.sub_ref, self.ck_hbm, self.cv_hbm, self.o_ref = lam_refs, sub_ref, ck_hbm, cv_hbm, o_ref
        self.kbuf, self.vbuf, self.sems = kbuf, vbuf, sems
        self.bias_s, self.m_s, self.l_s, self.acc_s = bias_s, m_s, l_s, acc_s
        self.n = q_ref.shape[1]
        self.rows = DIFF_HEADS * 2 * self.n
        self.page_rows = PAGE * DIFF_HEADS
        self.past = n_groups * DECODE_PAGES * PAGE
        r1 = lax.broadcasted_iota(jnp.int32, (self.rows, 1), 0)
        self.head = r1 // (2 * self.n)
        self.qi = r1 % self.n
        self.slope = jnp.zeros((self.rows, 1), F32)
        for h in range(DIFF_HEADS):
            self.slope = jnp.where(self.head == h, SLOPES[h] * LOG2E, self.slope)

    def _copies(self, seq, group):
        slot = group % DECODE_SLOTS
        copies = []
        for p in range(DECODE_PAGES):
            page = self.pt_ref[seq, group * DECODE_PAGES + p]
            copies.append(pltpu.make_async_copy(self.ck_hbm.at[page], self.kbuf.at[slot, p], self.sems.at[0, slot]))
            copies.append(pltpu.make_async_copy(self.cv_hbm.at[page], self.vbuf.at[slot, p], self.sems.at[1, slot]))
        return copies

    def start(self, seq, group):
        for c in self._copies(seq, group):
            c.start()

    def wait(self, seq, group):
        for c in self._copies(seq, group):
            c.wait()

    def init_bias(self):
        col = lax.broadcasted_iota(jnp.int32, self.bias_s.shape, 1)
        dist = (self.past + self.qi - col // DIFF_HEADS).astype(F32)
        self.bias_s[...] = jnp.where(col % DIFF_HEADS == self.head, -self.slope * dist, -jnp.inf)

    def begin_sequence(self):
        self.m_s[...] = jnp.full_like(self.m_s, NEG_BIG)
        self.l_s[...] = jnp.zeros_like(self.l_s)
        self.acc_s[...] = jnp.zeros_like(self.acc_s)
        q = self.q_ref[0]
        self.qq = jnp.concatenate([_split_components(q[:, h * HEAD_DIM:(h + 1) * HEAD_DIM].astype(F32))
                                   for h in range(DIFF_HEADS)], axis=0).astype(BF16)

    def group(self, g, after_matmul):
        slot, page_rows = g % DECODE_SLOTS, self.page_rows
        per_chain = DECODE_PAGES // DECODE_CHAINS
        scores = []
        for p in range(DECODE_PAGES):
            scores.append(_dot_nt(self.qq, self.kbuf[slot, p].astype(BF16)))
            after_matmul()
        chains = []
        for c in range(DECODE_CHAINS):
            pages = range(c * per_chain, (c + 1) * per_chain)
            s = jnp.concatenate([scores[p] + (self.bias_s[...] + self.slope * float((g * DECODE_PAGES + p) * PAGE))
                                 for p in pages], axis=-1)
            m_c = jnp.max(s, axis=-1, keepdims=True)
            pb = jnp.exp2(s - m_c)
            l_c = jnp.sum(pb, axis=-1, keepdims=True)
            chains.append((m_c, l_c, pb.astype(BF16)))
        for c, (m_c, l_c, pb) in enumerate(chains):
            pv_c = None
            for idx in range(per_chain):
                part = _dot(pb[:, idx * page_rows:(idx + 1) * page_rows],
                            self.vbuf[slot, c * per_chain + idx].astype(BF16))
                pv_c = part if pv_c is None else pv_c + part
                after_matmul()
            chains[c] = (m_c, l_c, pv_c)
        m = self.m_s[:, :1]
        m_new = m
        for m_c, _, _ in chains:
            m_new = jnp.maximum(m_new, m_c)
        alpha = jnp.exp2(m - m_new)
        l_new = alpha * self.l_s[:, :1]
        acc = alpha * self.acc_s[...]
        for m_c, l_c, pv_c in chains:
            w = jnp.exp2(m_c - m_new)
            l_new = l_new + w * l_c
            acc = acc + w * pv_c
        self.m_s[...] = jnp.broadcast_to(m_new, (self.rows, LANES))
        self.l_s[...] = jnp.broadcast_to(l_new, (self.rows, LANES))
        self.acc_s[...] = acc

    def finish(self):
        n = self.n
        lam = _lambda(*(ref[...] for ref in self.lam_refs))
        r8 = lax.broadcasted_iota(jnp.int32, (2 * n, 1), 0)
        qi = jnp.where(r8 >= n, r8 - n, r8)
        for h in range(DIFF_HEADS):
            hs = slice(h * HEAD_DIM, (h + 1) * HEAD_DIM)
            hr = slice(h * 2 * n, (h + 1) * 2 * n)
            qq = _split_components(self.q_ref[0, :, hs].astype(F32))
            kn = self.kn_ref[:, h, :]
            vn = self.vn_ref[:, h, :]
            m, l, acc = self.m_s[hr, :1], self.l_s[hr, :1], self.acc_s[hr, :]
            s_cols = []
            for t in range(n):
                s_t = jnp.sum(qq * kn[t:t + 1, :], axis=-1, keepdims=True)
                s_t = s_t - SLOPES[h] * LOG2E * (qi - t).astype(F32)
                s_cols.append(jnp.where(qi >= t, s_t, -jnp.inf))
            m_new = m
            for s_t in s_cols:
                m_new = jnp.maximum(m_new, s_t)
            alpha = jnp.exp2(m - m_new)
            l = alpha * l
            acc = alpha * acc
            for t in range(n):
                p_t = jnp.exp2(s_cols[t] - m_new)
                l = l + p_t
                acc = acc + p_t * vn[t:t + 1, :]
            self.o_ref[0, :, hs] = _sub_norm(acc, l, lam, self.sub_ref[:, hs]).astype(BF16)


def _dot_by_tiles(a, w_ref, row0, col0, k, n):
    t = MXU_TILE
    cols = []
    for cn in range(0, n, t):
        acc = None
        for ck in range(0, k, t):
            part = _dot(a[:, ck:ck + t], w_ref[row0 + ck:row0 + ck + t, col0 + cn:col0 + cn + t])
            acc = part if acc is None else acc + part
            yield
        cols.append(acc)
    return jnp.concatenate(cols, axis=-1)


def _channel_tile_count():
    weights = D_MODEL * D_MODEL + 2 * D_MODEL * D_FF + D_FF * D_MODEL + D_MODEL * D_MODEL + PLE_DIM * D_MODEL
    return weights // (MXU_TILE * MXU_TILE)


def _channel_tasks(x_ref, r_ref, dn_ref, p_ref, wo_ref, ln2_ref, wfi_ref, wfo_ref, lnp_ref, wpg_ref, wpp_ref,
                   y_ref):
    mixed = jnp.concatenate([r_ref[...], dn_ref[...]], axis=-1)
    h = x_ref[...] + (yield from _dot_by_tiles(mixed, wo_ref, 0, 0, D_MODEL, D_MODEL))
    hn = _rms(h, ln2_ref[...]).astype(BF16)
    ff, act = None, None
    n_chunks = D_FF // FF_CHUNK
    for c in range(n_chunks + 1):
        if c < n_chunks:
            g = yield from _dot_by_tiles(hn, wfi_ref, 0, c * FF_CHUNK, D_MODEL, FF_CHUNK)
            u = yield from _dot_by_tiles(hn, wfi_ref, 0, D_FF + c * FF_CHUNK, D_MODEL, FF_CHUNK)
        if act is not None:
            part = yield from _dot_by_tiles(act, wfo_ref, (c - 1) * FF_CHUNK, 0, FF_CHUNK, D_MODEL)
            ff = part if ff is None else ff + part
        if c < n_chunks:
            act = (g * _sigmoid(g) * u).astype(BF16)
    h = h + ff
    gate_in = _rms(h, lnp_ref[...]).astype(BF16)
    embed = yield from _dot_by_tiles(p_ref[...].astype(BF16), wpp_ref, 0, 0, PLE_DIM, D_MODEL)
    gate = _sigmoid((yield from _dot_by_tiles(gate_in, wpg_ref, 0, 0, D_MODEL, D_MODEL)))
    y_ref[...] = h + gate * embed


def _channel_kernel(*refs):
    for _ in _channel_tasks(*refs):
        pass


def _channel_specs(tm, index):
    tok = lambda width: pl.BlockSpec((tm, width), index)
    return [tok(D_MODEL), tok(HALF), tok(HALF), tok(PLE_DIM),
            _const_spec((D_MODEL, D_MODEL)), _const_spec((1, D_MODEL)),
            _const_spec((D_MODEL, 2 * D_FF)), _const_spec((D_FF, D_MODEL)),
            _const_spec((1, D_MODEL)), _const_spec((D_MODEL, D_MODEL)),
            _const_spec((PLE_DIM, D_MODEL))], tok(D_MODEL)


def _channel(x, r, dn, p, chan_w, tm):
    m = x.shape[0]
    in_specs, out_spec = _channel_specs(tm, lambda i: (i, 0))
    return pl.pallas_call(
        _channel_kernel,
        grid=(m // tm,),
        in_specs=in_specs,
        out_specs=out_spec,
        out_shape=jax.ShapeDtypeStruct((m, D_MODEL), F32),
        compiler_params=pltpu.CompilerParams(dimension_semantics=("parallel",),
                                             vmem_limit_bytes=VMEM_LIMIT),
        name="channel",
    )(x, r, dn, p, *chan_w)


def _channel_decode_kernel(pt_ref, x_ref, r_ref, dn_ref, p_ref, wo_ref, ln2_ref, wfi_ref, wfo_ref, lnp_ref, wpg_ref,
                           wpp_ref, q_ref, kn_ref, vn_ref, lq1_ref, lk1_ref, lq2_ref, lk2_ref, sub_ref, ck_hbm,
                           cv_hbm, y_ref, o_ref, kbuf, vbuf, sems, bias_s, m_s, l_s, acc_s, *, n_seq, n_groups):
    i = pl.program_id(0)
    ahead = DECODE_SLOTS - 1
    chan_refs = (x_ref, r_ref, dn_ref, p_ref, wo_ref, ln2_ref, wfi_ref, wfo_ref, lnp_ref, wpg_ref, wpp_ref)
    dec = _PagedDecoder(pt_ref, q_ref, kn_ref, vn_ref, (lq1_ref, lk1_ref, lq2_ref, lk2_ref), sub_ref, ck_hbm,
                        cv_hbm, o_ref, kbuf, vbuf, sems, bias_s, m_s, l_s, acc_s, n_groups)

    @pl.when(i == 0)
    def _():
        dec.init_bias()
        for g in range(ahead):
            dec.start(0, g)

    dec.begin_sequence()
    tasks = _channel_tasks(*chan_refs, y_ref)
    slots = 2 * DECODE_PAGES * n_groups
    n_tasks = _channel_tile_count()
    assert n_tasks <= slots
    issued = [0, 0]

    def after_matmul():
        issued[0] += 1
        while issued[1] * slots < issued[0] * n_tasks:
            next(tasks, None)
            issued[1] += 1

    for g in range(n_groups):
        nxt = g + ahead
        if nxt < n_groups:
            dec.start(i, nxt)
        else:
            pl.when(i + 1 < n_seq)(functools.partial(dec.start, i + 1, nxt - n_groups))
        dec.wait(i, g)
        dec.group(g, after_matmul)
    for _ in tasks:
        pass
    dec.finish()


def _channel_and_decode(x, r, dn, p, chan_w, page_table, dq, dk_new, dv_new, cache_k, cache_v, lams, sub_w):
    m = x.shape[0]
    db, ds = dq.shape[:2]
    tm = m // db
    assert tm * db == m and tm % 256 == 0, (m, db)
    n_groups = page_table.shape[1] // DECODE_PAGES
    assert n_groups * DECODE_PAGES == page_table.shape[1] and n_groups % DECODE_SLOTS == 0
    chan_in, chan_out = _channel_specs(tm, lambda i, pt: (i, 0))
    tok = pl.BlockSpec((1, ds, HALF), lambda i, pt: (i, 0, 0))
    new = pl.BlockSpec((None, ds, DIFF_HEADS, HEAD_DIM), lambda i, pt: (i, 0, 0, 0))
    page_rows = PAGE * DIFF_HEADS
    cache_k, cache_v = (c.reshape(-1, page_rows, HEAD_DIM) for c in (cache_k, cache_v))
    hbm = pl.BlockSpec(memory_space=pl.ANY)
    ring = pltpu.VMEM((DECODE_SLOTS, DECODE_PAGES, page_rows, HEAD_DIM), F32)
    state = pltpu.VMEM((DIFF_HEADS * 2 * ds, LANES), F32)
    bias = pltpu.VMEM((DIFF_HEADS * 2 * ds, page_rows), F32)
    return pl.pallas_call(
        functools.partial(_channel_decode_kernel, n_seq=db, n_groups=n_groups),
        grid_spec=pltpu.PrefetchScalarGridSpec(
            num_scalar_prefetch=1,
            grid=(db,),
            in_specs=chan_in + [tok, new, new] + [_const_spec((1, QK_DIM))] * 4 + [_const_spec((1, HALF)), hbm, hbm],
            out_specs=[chan_out, tok],
            scratch_shapes=[ring, ring, pltpu.SemaphoreType.DMA((2, DECODE_SLOTS)), bias, state, state, state],
        ),
        out_shape=[jax.ShapeDtypeStruct((m, D_MODEL), F32), jax.ShapeDtypeStruct((db, ds, HALF), BF16)],
        compiler_params=pltpu.CompilerParams(dimension_semantics=("arbitrary",),
                                             vmem_limit_bytes=FUSED_VMEM_LIMIT),
        name="channel_and_decode",
    )(page_table, x, r, dn, p, *chan_w, dq, dk_new, dv_new, *lams, sub_w, cache_k, cache_v)


def kernel(x_prompt, x_sample, cache_k, cache_v, state_ret, page_table, p_prompt, p_sample, ln1_w, w_in, q_norm_w, k_norm_w, lambda_q1, lambda_k1, lambda_q2, lambda_k2, ret_gn_w, ret_gn_b, diff_subln_w, w_o, ln2_w, w_ffn_in, w_ffn_out, ln_ple_w, w_ple_gate, w_ple_proj):
    depth = w_in.shape[0]
    assert depth == 1, "single-layer trunk"
    b, s, _ = x_prompt.shape
    db, ds, _ = x_sample.shape

    row = lambda a: a[0].reshape(1, -1)
    wb = lambda a: a[0].astype(BF16)
    qn_w = jnp.tile(row(q_norm_w), (1, HALF // QK_DIM))
    kn_w = jnp.tile(row(k_norm_w), (1, HALF // QK_DIM))
    lams = (row(lambda_q1), row(lambda_k1), row(lambda_q2), row(lambda_k2))
    proj_w = (row(ln1_w), wb(w_in), qn_w, kn_w)
    chan_w = (wb(w_o), row(ln2_w), wb(w_ffn_in), wb(w_ffn_out), row(ln_ple_w), wb(w_ple_gate), wb(w_ple_proj))
    gn_w, gn_b, sub_w = row(ret_gn_w), row(ret_gn_b), row(diff_subln_w)

    xp = x_prompt.reshape(b * s, D_MODEL)
    rq, rk, rv, rg, dq, dk, dv, dkb, dvt = _project(xp, *proj_w, tm=DIFF_T)
    r, ret_fin = _prompt_retention(rq, rk, rv, rg, gn_w, gn_b, b, s)
    dn = _prompt_diff_attention(dq, dkb, dvt, lams, sub_w, b, s)

    xs = x_sample.reshape(db * ds, D_MODEL)
    rq_s, rk_s, rv_s, rg_s, dq_s, dk_s, dv_s, _, _ = _project(xs, *proj_w, tm=db * ds)
    tok3 = lambda a: a.reshape(db, ds, HALF)
    r_s, ret_new = _sample_retention(tok3(rq_s), tok3(rk_s), tok3(rv_s), tok3(rg_s), state_ret[0], gn_w, gn_b)
    new4 = lambda a: a.reshape(db, ds, DIFF_HEADS, HEAD_DIM)
    y_prompt, dn_s = _channel_and_decode(xp, r, dn, p_prompt[0].reshape(b * s, PLE_DIM), chan_w, page_table,
                                         tok3(dq_s), new4(dk_s), new4(dv_s), cache_k, cache_v, lams, sub_w)
    y_sample = _channel(xs, r_s.reshape(db * ds, HALF), dn_s.reshape(db * ds, HALF),
                        p_sample[0].reshape(db * ds, PLE_DIM), chan_w, tm=db * ds)

    heads = lambda a, n: a.reshape(1, n, -1, DIFF_HEADS, HEAD_DIM)
    return (y_prompt.reshape(b, s, D_MODEL), y_sample.reshape(db, ds, D_MODEL),
            heads(dk, b), heads(dv, b), ret_fin[None],
            heads(dk_s, db), heads(dv_s, db), ret_new[None])
```

```python
import functools
import math

import jax
import jax.numpy as jnp
from jax import lax
from jax.experimental import pallas as pl
from jax.experimental.pallas import tpu as pltpu

F32 = jnp.float32
BF16 = jnp.bfloat16

D_MODEL = 1024
RET_HEADS = 4
HEAD_DIM = 128
DIFF_HEADS = 4
QK_DIM = 64
HALF = RET_HEADS * HEAD_DIM
PROJ_WIDTH = 7 * HALF
D_FF = 2816
PLE_DIM = 256
RET_CHUNK = 128
PAGE = 128
EPS = 1e-6
LAM_INIT = 0.8 - 0.6 * math.exp(-0.3 * 0)
LOG_G = tuple(math.log1p(-(2.0 ** (-5.0 - h))) for h in range(RET_HEADS))
SLOPES = tuple(2.0 ** (-8.0 / DIFF_HEADS * (h + 1)) for h in range(DIFF_HEADS))
LOG2E = math.log2(math.e)
NEG_BIG = -1e30

LANES = 128
VMEM_LIMIT = 56 * 1024 * 1024
FUSED_VMEM_LIMIT = 61 * 1024 * 1024

FF_CHUNK = 256
RET_SEQS_PER_STEP = 4
DIFF_T = 512
DECODE_PAGES = 16
DECODE_CHAINS = 2
DECODE_SLOTS = 2


def _sigmoid(x):
    return 1.0 / (1.0 + jnp.exp(-x))


def _rms(x, w):
    return x * lax.rsqrt(jnp.mean(x * x, axis=-1, keepdims=True) + EPS) * w


def _dot(a, b):
    return jnp.dot(a, b, preferred_element_type=F32)


def _dot_nt(a, b):
    return lax.dot_general(a, b, (((1,), (1,)), ((), ())), preferred_element_type=F32)


def _lambda(lq1, lk1, lq2, lk2):
    a = jnp.exp(jnp.sum(lq1 * lk1, axis=-1, keepdims=True))
    b = jnp.exp(jnp.sum(lq2 * lk2, axis=-1, keepdims=True))
    return a - b + LAM_INIT


def _const_spec(shape):
    nd = len(shape)
    return pl.BlockSpec(shape, lambda *_: (0,) * nd, pipeline_mode=pl.Buffered(1))


def _seg_rms(y, w):
    lo_mask = lax.broadcasted_iota(jnp.int32, (1, LANES), 1) < QK_DIM
    outs = []
    for g in range(HALF // LANES):
        yg = y[:, g * LANES:(g + 1) * LANES]
        t = yg * yg
        lo = jnp.sum(jnp.where(lo_mask, t, 0.0), axis=-1, keepdims=True)
        hi = jnp.sum(jnp.where(lo_mask, 0.0, t), axis=-1, keepdims=True)
        ms = jnp.where(lo_mask, lo, hi) * (1.0 / QK_DIM)
        outs.append(yg * lax.rsqrt(ms + EPS))
    return jnp.concatenate(outs, axis=-1) * w


def _store_cache_format(ref, y):
    for h in range(DIFF_HEADS):
        ref[pl.ds(h, y.shape[0], stride=DIFF_HEADS), :] = y[:, h * HEAD_DIM:(h + 1) * HEAD_DIM]


def _projection(x_ref, ln1_ref, w_ref):
    xn = _rms(x_ref[...], ln1_ref[...]).astype(BF16)
    return lambda i: _dot(xn, w_ref[:, i * HALF:(i + 1) * HALF])


def _retention_operands(col):
    return col(0).astype(BF16), (col(1) * (HEAD_DIM ** -0.5)).astype(BF16), col(2).astype(BF16), col(3)


def _diff_projection_steps(col, qn_ref, kn_ref, dq_ref, dk_ref, dv_ref, dkb_ref, dvt_ref):
    def q_step():
        dq = _seg_rms(col(4), qn_ref[...])
        dq_ref[...] = (dq * (LOG2E * QK_DIM ** -0.5)).astype(BF16)

    def k_step():
        dk = _seg_rms(col(5), kn_ref[...])
        dkb_ref[...] = dk.astype(BF16)
        _store_cache_format(dk_ref, dk)

    def v_step():
        dv = col(6)
        dvt_ref[0] = dv.T.astype(BF16)
        _store_cache_format(dv_ref, dv)

    return [q_step, k_step, v_step]


def _proj_kernel(x_ref, ln1_ref, w_ref, qn_ref, kn_ref,
                 rq_ref, rk_ref, rv_ref, rg_ref, dq_ref, dk_ref, dv_ref, dkb_ref, dvt_ref):
    col = _projection(x_ref, ln1_ref, w_ref)
    for step in _diff_projection_steps(col, qn_ref, kn_ref, dq_ref, dk_ref, dv_ref, dkb_ref, dvt_ref):
        step()
    rq_ref[...], rk_ref[...], rv_ref[...], rg_ref[...] = _retention_operands(col)


def _proj_retention_kernel(x_ref, ln1_ref, w_ref, qn_ref, kn_ref, gnw_ref, gnb_ref,
                           dq_ref, dk_ref, dv_ref, dkb_ref, dvt_ref, r_ref, fin_ref, state, *, tiles_per_seq):
    i = pl.program_id(0)

    @pl.when(i % tiles_per_seq == 0)
    def _():
        state[...] = jnp.zeros_like(state)

    col = _projection(x_ref, ln1_ref, w_ref)
    chunks = _retention_chunks(*_retention_operands(col), gnw_ref, gnb_ref, state, r_ref)
    for step in _diff_projection_steps(col, qn_ref, kn_ref, dq_ref, dk_ref, dv_ref, dkb_ref, dvt_ref):
        next(chunks, None)
        step()
    for _ in chunks:
        pass

    @pl.when((i + 1) % tiles_per_seq == 0)
    def _():
        fin_ref[0] = state[...]


def _proj_specs(m, tm):
    tok = lambda width: pl.BlockSpec((tm, width), lambda i: (i, 0))
    half = lambda dt: jax.ShapeDtypeStruct((m, HALF), dt)
    cache_fmt = pl.BlockSpec((tm * DIFF_HEADS, HEAD_DIM), lambda i: (i, 0))
    cache_shape = jax.ShapeDtypeStruct((m * DIFF_HEADS, HEAD_DIM), F32)
    in_specs = [tok(D_MODEL), _const_spec((1, D_MODEL)), _const_spec((D_MODEL, PROJ_WIDTH)),
                _const_spec((1, HALF)), _const_spec((1, HALF))]
    diff_specs = [tok(HALF), cache_fmt, cache_fmt, tok(HALF), pl.BlockSpec((1, HALF, tm), lambda i: (i, 0, 0))]
    diff_shapes = [half(BF16), cache_shape, cache_shape, half(BF16),
                   jax.ShapeDtypeStruct((m // tm, HALF, tm), BF16)]
    return tok, half, in_specs, diff_specs, diff_shapes


def _project(x, ln1_w, w_in, qn_w, kn_w, tm):
    m = x.shape[0]
    tok, half, in_specs, diff_specs, diff_shapes = _proj_specs(m, tm)
    return pl.pallas_call(
        _proj_kernel,
        grid=(m // tm,),
        in_specs=in_specs,
        out_specs=[tok(HALF)] * 4 + diff_specs,
        out_shape=[half(BF16), half(BF16), half(BF16), half(F32)] + diff_shapes,
        compiler_params=pltpu.CompilerParams(dimension_semantics=("parallel",),
                                             vmem_limit_bytes=VMEM_LIMIT),
        name="proj",
    )(x, ln1_w, w_in, qn_w, kn_w)


def _project_and_retain(x, ln1_w, w_in, qn_w, kn_w, gn_w, gn_b, batch, seq):
    m, tm = x.shape[0], DIFF_T
    tok, half, in_specs, diff_specs, diff_shapes = _proj_specs(m, tm)
    tiles_per_seq = seq // tm
    state = (RET_HEADS, HEAD_DIM, HEAD_DIM)
    return pl.pallas_call(
        functools.partial(_proj_retention_kernel, tiles_per_seq=tiles_per_seq),
        grid=(m // tm,),
        in_specs=in_specs + [_const_spec((1, HALF)), _const_spec((1, HALF))],
        out_specs=diff_specs + [tok(HALF), pl.BlockSpec((1,) + state, lambda i: (i // tiles_per_seq, 0, 0, 0))],
        out_shape=diff_shapes + [half(BF16), jax.ShapeDtypeStruct((batch,) + state, F32)],
        scratch_shapes=[pltpu.VMEM(state, F32)],
        compiler_params=pltpu.CompilerParams(dimension_semantics=("arbitrary",),
                                             vmem_limit_bytes=VMEM_LIMIT),
        name="proj_retention",
    )(x, ln1_w, w_in, qn_w, kn_w, gn_w, gn_b)


def _group_norm_gate(o, g, gn_w, gn_b):
    mu = jnp.mean(o, axis=-1, keepdims=True)
    d = o - mu
    var = jnp.mean(d * d, axis=-1, keepdims=True)
    r = d * lax.rsqrt(var + EPS) * gn_w + gn_b
    return r * (g * _sigmoid(g))


def _retention_chunks(rq, rk, rv, rg, gnw_ref, gnb_ref, state, r_ref):
    n = RET_CHUNK
    row = lax.broadcasted_iota(jnp.int32, (n, n), 0).astype(F32)
    col = lax.broadcasted_iota(jnp.int32, (n, n), 1).astype(F32)
    rel = row - col
    heads = [slice(h * HEAD_DIM, (h + 1) * HEAD_DIM) for h in range(RET_HEADS)]
    decay = [jnp.where(rel >= 0, jnp.exp(lg * jnp.maximum(rel, 0.0)), 0.0) for lg in LOG_G]
    cross_w = [jnp.exp(lg * (row + 1.0)) for lg in LOG_G]
    k_w = [jnp.exp(lg * (n - 1.0 - row)) for lg in LOG_G]
    for c in range(rq.shape[0] // n):
        rows = slice(c * n, (c + 1) * n)
        first = []
        for h, hs in enumerate(heads):
            q, k, v = rq[rows, hs], rk[rows, hs], rv[rows, hs]
            st = state[h]
            first.append((_dot_nt(q, k), _dot(q, st.astype(BF16)), v))
            kw_t = (k.astype(F32) * k_w[h]).T.astype(BF16)
            state[h] = math.exp(LOG_G[h] * n) * st + _dot(kw_t, v)
        yield
        for h, hs in enumerate(heads):
            s, cross, v = first[h]
            o = _dot((s * decay[h]).astype(BF16), v) + cross * cross_w[h]
            r = _group_norm_gate(o, rg[rows, hs], gnw_ref[:, hs], gnb_ref[:, hs])
            r_ref[rows, hs] = r.astype(BF16)


def _ret_step_kernel(rq_ref, rk_ref, rv_ref, rg_ref, st_ref, gnw_ref, gnb_ref, r_ref, new_ref):
    n = rq_ref.shape[1]
    row = lax.broadcasted_iota(jnp.int32, (n, LANES), 0).astype(F32)
    for b, h in [(b, h) for b in range(rq_ref.shape[0]) for h in range(RET_HEADS)]:
        hs = slice(h * HEAD_DIM, (h + 1) * HEAD_DIM)
        lg = LOG_G[h]
        q = rq_ref[b, :, hs].astype(F32)
        k = rk_ref[b, :, hs].astype(F32)
        v = rv_ref[b, :, hs].astype(F32)
        st = st_ref[b, h]
        q_pad = jnp.concatenate([q, jnp.zeros((16 - n, HEAD_DIM), F32)], axis=0).astype(BF16)
        o = _dot(q_pad, st.astype(BF16))[:n] * jnp.exp(lg * (row + 1.0))
        new = math.exp(lg * n) * st
        k_t = jnp.concatenate([k, jnp.zeros((8 - n, HEAD_DIM), F32)], axis=0).T
        for m in range(n):
            s_m = jnp.sum(q * k[m:m + 1, :], axis=-1, keepdims=True)
            decay = jnp.where(row >= m, jnp.exp(lg * jnp.maximum(row - m, 0.0)), 0.0)
            o = o + (s_m * decay) * v[m:m + 1, :]
            new = new + math.exp(lg * (n - 1.0 - m)) * (k_t[:, m:m + 1] * v[m:m + 1, :])
        new_ref[b, h] = new
        r = _group_norm_gate(o, rg_ref[b, :, hs], gnw_ref[:, hs], gnb_ref[:, hs])
        r_ref[b, :, hs] = r.astype(BF16)


def _sample_retention(rq, rk, rv, rg, state, gn_w, gn_b):
    db, ds = rq.shape[:2]
    nb = RET_SEQS_PER_STEP
    tok = pl.BlockSpec((nb, ds, HALF), lambda b: (b, 0, 0))
    st = pl.BlockSpec((nb, RET_HEADS, HEAD_DIM, HEAD_DIM), lambda b: (b, 0, 0, 0))
    return pl.pallas_call(
        _ret_step_kernel,
        grid=(db // nb,),
        in_specs=[tok, tok, tok, tok, st, _const_spec((1, HALF)), _const_spec((1, HALF))],
        out_specs=[tok, st],
        out_shape=[jax.ShapeDtypeStruct((db, ds, HALF), BF16),
                   jax.ShapeDtypeStruct(state.shape, F32)],
        compiler_params=pltpu.CompilerParams(dimension_semantics=("parallel",),
                                             vmem_limit_bytes=VMEM_LIMIT),
        name="sample_retention",
    )(rq, rk, rv, rg, state, gn_w, gn_b)


def _split_components(q):
    lo_mask = lax.broadcasted_iota(jnp.int32, q.shape, 1) < QK_DIM
    zero = jnp.zeros_like(q)
    return jnp.concatenate([jnp.where(lo_mask, q, zero), jnp.where(lo_mask, zero, q)], axis=0)


def _sub_norm(acc, l, lam, w):
    t = acc.shape[0] // 2
    o = acc / l
    d = o[:t] - lam * o[t:]
    return d * lax.rsqrt(jnp.mean(d * d, axis=-1, keepdims=True) + EPS) * w * (1.0 - LAM_INIT)


def _diff_kernel(q_ref, k_ref, vt_ref, lq1_ref, lk1_ref, lq2_ref, lk2_ref, sub_ref, o_ref):
    i = pl.program_id(1)
    t = DIFF_T
    lam = _lambda(lq1_ref[...], lk1_ref[...], lq2_ref[...], lk2_ref[...])
    k_idx = lax.broadcasted_iota(jnp.int32, (t, LANES), 0).astype(F32)
    key = lax.broadcasted_iota(jnp.int32, (t, 2 * t), 0)
    col = lax.broadcasted_iota(jnp.int32, (t, 2 * t), 1)
    causal = key <= jnp.where(col >= t, col - t, col)
    heads = [slice(h * HEAD_DIM, (h + 1) * HEAD_DIM) for h in range(DIFF_HEADS)]
    qqs = [_split_components(q_ref[:, hs]) for hs in heads]

    def block(j, carry, masked):
        start = pl.multiple_of(j * t, t)
        k_pos = k_idx + (j * t).astype(F32)
        out = []
        for h, hs in enumerate(heads):
            m, l, acc = carry[h]
            k = k_ref[pl.ds(start, t), hs]
            vt = vt_ref[j, hs, :]
            bias = (SLOPES[h] * LOG2E) * k_pos
            s = _dot_nt(k, qqs[h]) + jnp.concatenate([bias] * (2 * t // LANES), axis=1)
            if masked:
                s = jnp.where(causal, s, -jnp.inf)
            m_new = jnp.maximum(m, jnp.max(s, axis=0, keepdims=True))
            alpha = jnp.exp2(m - m_new)
            p = jnp.exp2(s - m_new)
            l = alpha * l + jnp.sum(p, axis=0, keepdims=True)
            acc = alpha * acc + _dot(vt, p.astype(BF16))
            out.append((m_new, l, acc))
        return tuple(out)

    init = tuple((jnp.full((1, 2 * t), NEG_BIG, F32), jnp.zeros((1, 2 * t), F32),
                  jnp.zeros((HEAD_DIM, 2 * t), F32)) for _ in heads)
    carry = lax.fori_loop(0, i, functools.partial(block, masked=False), init)
    carry = block(i, carry, masked=True)
    for (_, l, acc), hs in zip(carry, heads):
        o = acc / l
        d = o[:, :t] - lam * o[:, t:]
        dn = d * lax.rsqrt(jnp.mean(d * d, axis=0, keepdims=True) + EPS)
        o_ref[:, hs] = (dn.T * sub_ref[:, hs] * (1.0 - LAM_INIT)).astype(BF16)


def _prompt_diff_attention(dq, dk, dvt, lams, sub_w, batch, seq):
    nq = seq // DIFF_T
    qspec = pl.BlockSpec((DIFF_T, HALF), lambda b, i: (b * nq + i, 0))
    kspec = pl.BlockSpec((seq, HALF), lambda b, i: (b, 0))
    vspec = pl.BlockSpec((nq, HALF, DIFF_T), lambda b, i: (b, 0, 0))
    lspec = _const_spec((1, QK_DIM))
    return pl.pallas_call(
        _diff_kernel,
        grid=(batch, nq),
        in_specs=[qspec, kspec, vspec, lspec, lspec, lspec, lspec, _const_spec((1, HALF))],
        out_specs=qspec,
        out_shape=jax.ShapeDtypeStruct((batch * seq, HALF), BF16),
        compiler_params=pltpu.CompilerParams(dimension_semantics=("parallel", "arbitrary"),
                                             vmem_limit_bytes=VMEM_LIMIT),
        name="prompt_diff_attention",
    )(dq, dk, dvt, *lams, sub_w)


class _PagedDecoder:
    def __init__(self, pt_ref, q_ref, kn_ref, vn_ref, lam_refs, sub_ref, ck_hbm, cv_hbm, o_ref,
                 kbuf, vbuf, sems, bias_s, m_s, l_s, acc_s, n_groups):
        self.pt_ref, self.q_ref, self.kn_ref, self.vn_ref = pt_ref, q_ref, kn_ref, vn_ref
        self.lam_refs, self.sub_ref, self.ck_hbm, self.cv_hbm, self.o_ref = lam_refs, sub_ref, ck_hbm, cv_hbm, o_ref
        self.kbuf, self.vbuf, self.sems = kbuf, vbuf, sems
        self.bias_s, self.m_s, self.l_s, self.acc_s = bias_s, m_s, l_s, acc_s
        self.n = q_ref.shape[1]
        self.rows = DIFF_HEADS * 2 * self.n
        self.page_rows = PAGE * DIFF_HEADS
        self.past = n_groups * DECODE_PAGES * PAGE
        r1 = lax.broadcasted_iota(jnp.int32, (self.rows, 1), 0)
        self.head = r1 // (2 * self.n)
        self.qi = r1 % self.n
        self.slope = jnp.zeros((self.rows, 1), F32)
        for h in range(DIFF_HEADS):
            self.slope = jnp.where(self.head == h, SLOPES[h] * LOG2E, self.slope)

    def _copies(self, seq, group):
        slot = group % DECODE_SLOTS
        copies = []
        for p in range(DECODE_PAGES):
            page = self.pt_ref[seq, group * DECODE_PAGES + p]
            copies.append(pltpu.make_async_copy(self.ck_hbm.at[page], self.kbuf.at[slot, p], self.sems.at[0, slot]))
            copies.append(pltpu.make_async_copy(self.cv_hbm.at[page], self.vbuf.at[slot, p], self.sems.at[1, slot]))
        return copies

    def start(self, seq, group):
        for c in self._copies(seq, group):
            c.start()

    def wait(self, seq, group):
        for c in self._copies(seq, group):
            c.wait()

    def init_bias(self):
        col = lax.broadcasted_iota(jnp.int32, self.bias_s.shape, 1)
        dist = (self.past + self.qi - col // DIFF_HEADS).astype(F32)
        self.bias_s[...] = jnp.where(col % DIFF_HEADS == self.head, -self.slope * dist, -jnp.inf)

    def begin_sequence(self):
        self.m_s[...] = jnp.full_like(self.m_s, NEG_BIG)
        self.l_s[...] = jnp.zeros_like(self.l_s)
        self.acc_s[...] = jnp.zeros_like(self.acc_s)
        q = self.q_ref[0]
        self.qq = jnp.concatenate([_split_components(q[:, h * HEAD_DIM:(h + 1) * HEAD_DIM].astype(F32))
                                   for h in range(DIFF_HEADS)], axis=0).astype(BF16)

    def scores(self, g):
        slot, per_chain = g % DECODE_SLOTS, DECODE_PAGES // DECODE_CHAINS
        return [jnp.concatenate([_dot_nt(self.qq, self.kbuf[slot, p].astype(BF16))
                                 for p in range(c * per_chain, (c + 1) * per_chain)], axis=-1)
                for c in range(DECODE_CHAINS)]

    def fold(self, g, scores):
        slot, page_rows = g % DECODE_SLOTS, self.page_rows
        per_chain = DECODE_PAGES // DECODE_CHAINS
        chains = []
        for c, s in enumerate(scores):
            pages = range(c * per_chain, (c + 1) * per_chain)
            s = s + jnp.concatenate([self.bias_s[...] + self.slope * float((g * DECODE_PAGES + p) * PAGE)
                                     for p in pages], axis=-1)
            m_c = jnp.max(s, axis=-1, keepdims=True)
            pb = jnp.exp2(s - m_c)
            l_c = jnp.sum(pb, axis=-1, keepdims=True)
            pb = pb.astype(BF16)
            pv_c = _dot(pb[:, :page_rows], self.vbuf[slot, pages[0]].astype(BF16))
            for idx, pg in enumerate(pages[1:], start=1):
                pv_c = pv_c + _dot(pb[:, idx * page_rows:(idx + 1) * page_rows], self.vbuf[slot, pg].astype(BF16))
            chains.append((m_c, l_c, pv_c))
        m = self.m_s[:, :1]
        m_new = m
        for m_c, _, _ in chains:
            m_new = jnp.maximum(m_new, m_c)
        alpha = jnp.exp2(m - m_new)
        l_new = alpha * self.l_s[:, :1]
        acc = alpha * self.acc_s[...]
        for m_c, l_c, pv_c in chains:
            w = jnp.exp2(m_c - m_new)
            l_new = l_new + w * l_c
            acc = acc + w * pv_c
        self.m_s[...] = jnp.broadcast_to(m_new, (self.rows, LANES))
        self.l_s[...] = jnp.broadcast_to(l_new, (self.rows, LANES))
        self.acc_s[...] = acc

    def finish(self):
        n = self.n
        lam = _lambda(*(ref[...] for ref in self.lam_refs))
        r8 = lax.broadcasted_iota(jnp.int32, (2 * n, 1), 0)
        qi = jnp.where(r8 >= n, r8 - n, r8)
        for h in range(DIFF_HEADS):
            hs = slice(h * HEAD_DIM, (h + 1) * HEAD_DIM)
            hr = slice(h * 2 * n, (h + 1) * 2 * n)
            qq = _split_components(self.q_ref[0, :, hs].astype(F32))
            kn = self.kn_ref[:, h, :]
            vn = self.vn_ref[:, h, :]
            m, l, acc = self.m_s[hr, :1], self.l_s[hr, :1], self.acc_s[hr, :]
            s_cols = []
            for t in range(n):
                s_t = jnp.sum(qq * kn[t:t + 1, :], axis=-1, keepdims=True)
                s_t = s_t - SLOPES[h] * LOG2E * (qi - t).astype(F32)
                s_cols.append(jnp.where(qi >= t, s_t, -jnp.inf))
            m_new = m
            for s_t in s_cols:
                m_new = jnp.maximum(m_new, s_t)
            alpha = jnp.exp2(m - m_new)
            l = alpha * l
            acc = alpha * acc
            for t in range(n):
                p_t = jnp.exp2(s_cols[t] - m_new)
                l = l + p_t
                acc = acc + p_t * vn[t:t + 1, :]
            self.o_ref[0, :, hs] = _sub_norm(acc, l, lam, self.sub_ref[:, hs]).astype(BF16)


def _channel_phases(x_ref, r_ref, dn_ref, p_ref, wo_ref, ln2_ref, wfi_ref, wfo_ref, lnp_ref, wpg_ref, wpp_ref,
                    y_ref):
    st = {}

    def attention_out():
        mixed = jnp.concatenate([r_ref[...], dn_ref[...]], axis=-1)
        st["h"] = x_ref[...] + _dot(mixed, wo_ref[...])
        st["hn"] = _rms(st["h"], ln2_ref[...]).astype(BF16)

    def ffn_chunk(c):
        g = _dot(st["hn"], wfi_ref[:, c * FF_CHUNK:(c + 1) * FF_CHUNK])
        u = _dot(st["hn"], wfi_ref[:, D_FF + c * FF_CHUNK:D_FF + (c + 1) * FF_CHUNK])
        act = (g * _sigmoid(g) * u).astype(BF16)
        part = _dot(act, wfo_ref[c * FF_CHUNK:(c + 1) * FF_CHUNK, :])
        st["ff"] = part if c == 0 else st["ff"] + part

    def embedding_gate():
        h = st["h"] + st["ff"]
        gate = _sigmoid(_dot(_rms(h, lnp_ref[...]).astype(BF16), wpg_ref[...]))
        y_ref[...] = h + gate * _dot(p_ref[...].astype(BF16), wpp_ref[...])

    return ([attention_out] + [functools.partial(ffn_chunk, c) for c in range(D_FF // FF_CHUNK)]
            + [embedding_gate])


def _channel_kernel(*refs):
    for phase in _channel_phases(*refs):
        phase()


def _channel_specs(tm, index):
    tok = lambda width: pl.BlockSpec((tm, width), index)
    return [tok(D_MODEL), tok(HALF), tok(HALF), tok(PLE_DIM),
            _const_spec((D_MODEL, D_MODEL)), _const_spec((1, D_MODEL)),
            _const_spec((D_MODEL, 2 * D_FF)), _const_spec((D_FF, D_MODEL)),
            _const_spec((1, D_MODEL)), _const_spec((D_MODEL, D_MODEL)),
            _const_spec((PLE_DIM, D_MODEL))], tok(D_MODEL)


def _channel(x, r, dn, p, chan_w, tm):
    m = x.shape[0]
    in_specs, out_spec = _channel_specs(tm, lambda i: (i, 0))
    return pl.pallas_call(
        _channel_kernel,
        grid=(m // tm,),
        in_specs=in_specs,
        out_specs=out_spec,
        out_shape=jax.ShapeDtypeStruct((m, D_MODEL), F32),
        compiler_params=pltpu.CompilerParams(dimension_semantics=("parallel",),
                                             vmem_limit_bytes=VMEM_LIMIT),
        name="channel",
    )(x, r, dn, p, *chan_w)


def _channel_decode_kernel(pt_ref, x_ref, r_ref, dn_ref, p_ref, wo_ref, ln2_ref, wfi_ref, wfo_ref, lnp_ref, wpg_ref,
                           wpp_ref, q_ref, kn_ref, vn_ref, lq1_ref, lk1_ref, lq2_ref, lk2_ref, sub_ref, ck_hbm,
                           cv_hbm, y_ref, o_ref, kbuf, vbuf, sems, bias_s, m_s, l_s, acc_s, *, n_seq, n_groups):
    i = pl.program_id(0)
    ahead = DECODE_SLOTS - 1
    chan_refs = (x_ref, r_ref, dn_ref, p_ref, wo_ref, ln2_ref, wfi_ref, wfo_ref, lnp_ref, wpg_ref, wpp_ref)
    dec = _PagedDecoder(pt_ref, q_ref, kn_ref, vn_ref, (lq1_ref, lk1_ref, lq2_ref, lk2_ref), sub_ref, ck_hbm,
                        cv_hbm, o_ref, kbuf, vbuf, sems, bias_s, m_s, l_s, acc_s, n_groups)

    @pl.when(i == 0)
    def _():
        dec.init_bias()
        for g in range(ahead):
            dec.start(0, g)

    dec.begin_sequence()
    phases = _channel_phases(*chan_refs, y_ref)
    per_group = -(-len(phases) // n_groups)
    for g in range(n_groups):
        nxt = g + ahead
        if nxt < n_groups:
            dec.start(i, nxt)
        else:
            pl.when(i + 1 < n_seq)(functools.partial(dec.start, i + 1, nxt - n_groups))
        dec.wait(i, g)
        scores = dec.scores(g)
        for phase in phases[g * per_group:(g + 1) * per_group]:
            phase()
        dec.fold(g, scores)
    dec.finish()


def _channel_and_decode(x, r, dn, p, chan_w, page_table, dq, dk_new, dv_new, cache_k, cache_v, lams, sub_w):
    m = x.shape[0]
    db, ds = dq.shape[:2]
    tm = m // db
    assert tm * db == m and tm % 256 == 0, (m, db)
    n_groups = page_table.shape[1] // DECODE_PAGES
    assert n_groups * DECODE_PAGES == page_table.shape[1] and n_groups % DECODE_SLOTS == 0
    chan_in, chan_out = _channel_specs(tm, lambda i, pt: (i, 0))
    tok = pl.BlockSpec((1, ds, HALF), lambda i, pt: (i, 0, 0))
    new = pl.BlockSpec((None, ds, DIFF_HEADS, HEAD_DIM), lambda i, pt: (i, 0, 0, 0))
    page_rows = PAGE * DIFF_HEADS
    cache_k, cache_v = (c.reshape(-1, page_rows, HEAD_DIM) for c in (cache_k, cache_v))
    hbm = pl.BlockSpec(memory_space=pl.ANY)
    ring = pltpu.VMEM((DECODE_SLOTS, DECODE_PAGES, page_rows, HEAD_DIM), F32)
    state = pltpu.VMEM((DIFF_HEADS * 2 * ds, LANES), F32)
    bias = pltpu.VMEM((DIFF_HEADS * 2 * ds, page_rows), F32)
    return pl.pallas_call(
        functools.partial(_channel_decode_kernel, n_seq=db, n_groups=n_groups),
        grid_spec=pltpu.PrefetchScalarGridSpec(
            num_scalar_prefetch=1,
            grid=(db,),
            in_specs=chan_in + [tok, new, new] + [_const_spec((1, QK_DIM))] * 4 + [_const_spec((1, HALF)), hbm, hbm],
            out_specs=[chan_out, tok],
            scratch_shapes=[ring, ring, pltpu.SemaphoreType.DMA((2, DECODE_SLOTS)), bias, state, state, state],
        ),
        out_shape=[jax.ShapeDtypeStruct((m, D_MODEL), F32), jax.ShapeDtypeStruct((db, ds, HALF), BF16)],
        compiler_params=pltpu.CompilerParams(dimension_semantics=("arbitrary",),
                                             vmem_limit_bytes=FUSED_VMEM_LIMIT),
        name="channel_and_decode",
    )(page_table, x, r, dn, p, *chan_w, dq, dk_new, dv_new, *lams, sub_w, cache_k, cache_v)


def kernel(x_prompt, x_sample, cache_k, cache_v, state_ret, page_table, p_prompt, p_sample, ln1_w, w_in, q_norm_w, k_norm_w, lambda_q1, lambda_k1, lambda_q2, lambda_k2, ret_gn_w, ret_gn_b, diff_subln_w, w_o, ln2_w, w_ffn_in, w_ffn_out, ln_ple_w, w_ple_gate, w_ple_proj):
    depth = w_in.shape[0]
    assert depth == 1, "single-layer trunk"
    b, s, _ = x_prompt.shape
    db, ds, _ = x_sample.shape

    row = lambda a: a[0].reshape(1, -1)
    wb = lambda a: a[0].astype(BF16)
    qn_w = jnp.tile(row(q_norm_w), (1, HALF // QK_DIM))
    kn_w = jnp.tile(row(k_norm_w), (1, HALF // QK_DIM))
    lams = (row(lambda_q1), row(lambda_k1), row(lambda_q2), row(lambda_k2))
    proj_w = (row(ln1_w), wb(w_in), qn_w, kn_w)
    chan_w = (wb(w_o), row(ln2_w), wb(w_ffn_in), wb(w_ffn_out), row(ln_ple_w), wb(w_ple_gate), wb(w_ple_proj))
    gn_w, gn_b, sub_w = row(ret_gn_w), row(ret_gn_b), row(diff_subln_w)

    xp = x_prompt.reshape(b * s, D_MODEL)
    dq, dk, dv, dkb, dvt, r, ret_fin = _project_and_retain(xp, *proj_w, gn_w, gn_b, b, s)
    dn = _prompt_diff_attention(dq, dkb, dvt, lams, sub_w, b, s)

    xs = x_sample.reshape(db * ds, D_MODEL)
    rq_s, rk_s, rv_s, rg_s, dq_s, dk_s, dv_s, _, _ = _project(xs, *proj_w, tm=db * ds)
    tok3 = lambda a: a.reshape(db, ds, HALF)
    r_s, ret_new = _sample_retention(tok3(rq_s), tok3(rk_s), tok3(rv_s), tok3(rg_s), state_ret[0], gn_w, gn_b)
    new4 = lambda a: a.reshape(db, ds, DIFF_HEADS, HEAD_DIM)
    y_prompt, dn_s = _channel_and_decode(xp, r, dn, p_prompt[0].reshape(b * s, PLE_DIM), chan_w, page_table,
                                         tok3(dq_s), new4(dk_s), new4(dv_s), cache_k, cache_v, lams, sub_w)
    y_sample = _channel(xs, r_s.reshape(db * ds, HALF), dn_s.reshape(db * ds, HALF),
                        p_sample[0].reshape(db * ds, PLE_DIM), chan_w, tm=db * ds)

    heads = lambda a, n: a.reshape(1, n, -1, DIFF_HEADS, HEAD_DIM)
    return (y_prompt.reshape(b, s, D_MODEL), y_sample.reshape(db, ds, D_MODEL),
            heads(dk, b), heads(dv, b), ret_fin[None],
            heads(dk_s, db), heads(dv_s, db), ret_new[None])
```

```python
import functools
import math

import jax
import jax.numpy as jnp
from jax import lax
from jax.experimental import pallas as pl
from jax.experimental.pallas import tpu as pltpu

F32 = jnp.float32
BF16 = jnp.bfloat16

D_MODEL = 1024
RET_HEADS = 4
HEAD_DIM = 128
DIFF_HEADS = 4
QK_DIM = 64
HALF = RET_HEADS * HEAD_DIM
PROJ_WIDTH = 7 * HALF
D_FF = 2816
PLE_DIM = 256
RET_CHUNK = 128
PAGE = 128
EPS = 1e-6
LAM_INIT = 0.8 - 0.6 * math.exp(-0.3 * 0)
LOG_G = tuple(math.log1p(-(2.0 ** (-5.0 - h))) for h in range(RET_HEADS))
SLOPES = tuple(2.0 ** (-8.0 / DIFF_HEADS * (h + 1)) for h in range(DIFF_HEADS))
LOG2E = math.log2(math.e)
NEG_BIG = -1e30

LANES = 128
VMEM_LIMIT = 56 * 1024 * 1024
FUSED_VMEM_LIMIT = 61 * 1024 * 1024

FF_CHUNK = 256
RET_SEQS_PER_STEP = 4
DIFF_T = 512
DECODE_PAGES = 16
DECODE_CHAINS = 2
DECODE_SLOTS = 2


def _sigmoid(x):
    return 1.0 / (1.0 + jnp.exp(-x))


def _rms(x, w):
    return x * lax.rsqrt(jnp.mean(x * x, axis=-1, keepdims=True) + EPS) * w


def _dot(a, b):
    return jnp.dot(a, b, preferred_element_type=F32)


def _dot_nt(a, b):
    return lax.dot_general(a, b, (((1,), (1,)), ((), ())), preferred_element_type=F32)


def _lambda(lq1, lk1, lq2, lk2):
    a = jnp.exp(jnp.sum(lq1 * lk1, axis=-1, keepdims=True))
    b = jnp.exp(jnp.sum(lq2 * lk2, axis=-1, keepdims=True))
    return a - b + LAM_INIT


def _const_spec(shape):
    nd = len(shape)
    return pl.BlockSpec(shape, lambda *_: (0,) * nd, pipeline_mode=pl.Buffered(1))


def _seg_rms(y, w):
    lo_mask = lax.broadcasted_iota(jnp.int32, (1, LANES), 1) < QK_DIM
    outs = []
    for g in range(HALF // LANES):
        yg = y[:, g * LANES:(g + 1) * LANES]
        t = yg * yg
        lo = jnp.sum(jnp.where(lo_mask, t, 0.0), axis=-1, keepdims=True)
        hi = jnp.sum(jnp.where(lo_mask, 0.0, t), axis=-1, keepdims=True)
        ms = jnp.where(lo_mask, lo, hi) * (1.0 / QK_DIM)
        outs.append(yg * lax.rsqrt(ms + EPS))
    return jnp.concatenate(outs, axis=-1) * w


def _store_cache_format(ref, y):
    for h in range(DIFF_HEADS):
        ref[pl.ds(h, y.shape[0], stride=DIFF_HEADS), :] = y[:, h * HEAD_DIM:(h + 1) * HEAD_DIM]


def _projection(x_ref, ln1_ref, w_ref):
    xn = _rms(x_ref[...], ln1_ref[...]).astype(BF16)
    return lambda i: _dot(xn, w_ref[:, i * HALF:(i + 1) * HALF])


def _retention_operands(col):
    return col(0).astype(BF16), (col(1) * (HEAD_DIM ** -0.5)).astype(BF16), col(2).astype(BF16), col(3)


def _diff_projection_steps(col, qn_ref, kn_ref, dq_ref, dk_ref, dv_ref, dkb_ref, dvt_ref):
    def q_step():
        dq = _seg_rms(col(4), qn_ref[...])
        dq_ref[...] = (dq * (LOG2E * QK_DIM ** -0.5)).astype(BF16)

    def k_step():
        dk = _seg_rms(col(5), kn_ref[...])
        dkb_ref[...] = dk.astype(BF16)
        _store_cache_format(dk_ref, dk)

    def v_step():
        dv = col(6)
        dvt_ref[0] = dv.T.astype(BF16)
        _store_cache_format(dv_ref, dv)

    return [q_step, k_step, v_step]


def _proj_kernel(x_ref, ln1_ref, w_ref, qn_ref, kn_ref,
                 rq_ref, rk_ref, rv_ref, rg_ref, dq_ref, dk_ref, dv_ref, dkb_ref, dvt_ref):
    col = _projection(x_ref, ln1_ref, w_ref)
    for step in _diff_projection_steps(col, qn_ref, kn_ref, dq_ref, dk_ref, dv_ref, dkb_ref, dvt_ref):
        step()
    rq_ref[...], rk_ref[...], rv_ref[...], rg_ref[...] = _retention_operands(col)


def _proj_retention_kernel(x_ref, ln1_ref, w_ref, qn_ref, kn_ref, gnw_ref, gnb_ref,
                           dq_ref, dk_ref, dv_ref, dkb_ref, dvt_ref, r_ref, fin_ref, state, *, tiles_per_seq):
    i = pl.program_id(0)

    @pl.when(i % tiles_per_seq == 0)
    def _():
        state[...] = jnp.zeros_like(state)

    col = _projection(x_ref, ln1_ref, w_ref)
    chunks = _retention_chunks(*_retention_operands(col), gnw_ref, gnb_ref, state, r_ref)
    for step in _diff_projection_steps(col, qn_ref, kn_ref, dq_ref, dk_ref, dv_ref, dkb_ref, dvt_ref):
        next(chunks, None)
        step()
    for _ in chunks:
        pass

    @pl.when((i + 1) % tiles_per_seq == 0)
    def _():
        fin_ref[0] = state[...]


def _proj_specs(m, tm):
    tok = lambda width: pl.BlockSpec((tm, width), lambda i: (i, 0))
    half = lambda dt: jax.ShapeDtypeStruct((m, HALF), dt)
    cache_fmt = pl.BlockSpec((tm * DIFF_HEADS, HEAD_DIM), lambda i: (i, 0))
    cache_shape = jax.ShapeDtypeStruct((m * DIFF_HEADS, HEAD_DIM), F32)
    in_specs = [tok(D_MODEL), _const_spec((1, D_MODEL)), _const_spec((D_MODEL, PROJ_WIDTH)),
                _const_spec((1, HALF)), _const_spec((1, HALF))]
    diff_specs = [tok(HALF), cache_fmt, cache_fmt, tok(HALF), pl.BlockSpec((1, HALF, tm), lambda i: (i, 0, 0))]
    diff_shapes = [half(BF16), cache_shape, cache_shape, half(BF16),
                   jax.ShapeDtypeStruct((m // tm, HALF, tm), BF16)]
    return tok, half, in_specs, diff_specs, diff_shapes


def _project(x, ln1_w, w_in, qn_w, kn_w, tm):
    m = x.shape[0]
    tok, half, in_specs, diff_specs, diff_shapes = _proj_specs(m, tm)
    return pl.pallas_call(
        _proj_kernel,
        grid=(m // tm,),
        in_specs=in_specs,
        out_specs=[tok(HALF)] * 4 + diff_specs,
        out_shape=[half(BF16), half(BF16), half(BF16), half(F32)] + diff_shapes,
        compiler_params=pltpu.CompilerParams(dimension_semantics=("parallel",),
                                             vmem_limit_bytes=VMEM_LIMIT),
        name="proj",
    )(x, ln1_w, w_in, qn_w, kn_w)


def _project_and_retain(x, ln1_w, w_in, qn_w, kn_w, gn_w, gn_b, batch, seq):
    m, tm = x.shape[0], DIFF_T
    tok, half, in_specs, diff_specs, diff_shapes = _proj_specs(m, tm)
    tiles_per_seq = seq // tm
    state = (RET_HEADS, HEAD_DIM, HEAD_DIM)
    return pl.pallas_call(
        functools.partial(_proj_retention_kernel, tiles_per_seq=tiles_per_seq),
        grid=(m // tm,),
        in_specs=in_specs + [_const_spec((1, HALF)), _const_spec((1, HALF))],
        out_specs=diff_specs + [tok(HALF), pl.BlockSpec((1,) + state, lambda i: (i // tiles_per_seq, 0, 0, 0))],
        out_shape=diff_shapes + [half(BF16), jax.ShapeDtypeStruct((batch,) + state, F32)],
        scratch_shapes=[pltpu.VMEM(state, F32)],
        compiler_params=pltpu.CompilerParams(dimension_semantics=("arbitrary",),
                                             vmem_limit_bytes=VMEM_LIMIT),
        name="proj_retention",
    )(x, ln1_w, w_in, qn_w, kn_w, gn_w, gn_b)


def _group_norm_gate(o, g, gn_w, gn_b):
    mu = jnp.mean(o, axis=-1, keepdims=True)
    d = o - mu
    var = jnp.mean(d * d, axis=-1, keepdims=True)
    r = d * lax.rsqrt(var + EPS) * gn_w + gn_b
    return r * (g * _sigmoid(g))


def _retention_chunks(rq, rk, rv, rg, gnw_ref, gnb_ref, state, r_ref):
    n = RET_CHUNK
    row = lax.broadcasted_iota(jnp.int32, (n, n), 0).astype(F32)
    col = lax.broadcasted_iota(jnp.int32, (n, n), 1).astype(F32)
    rel = row - col
    heads = [slice(h * HEAD_DIM, (h + 1) * HEAD_DIM) for h in range(RET_HEADS)]
    decay = [jnp.where(rel >= 0, jnp.exp(lg * jnp.maximum(rel, 0.0)), 0.0) for lg in LOG_G]
    cross_w = [jnp.exp(lg * (row + 1.0)) for lg in LOG_G]
    k_w = [jnp.exp(lg * (n - 1.0 - row)) for lg in LOG_G]
    for c in range(rq.shape[0] // n):
        rows = slice(c * n, (c + 1) * n)
        first = []
        for h, hs in enumerate(heads):
            q, k, v = rq[rows, hs], rk[rows, hs], rv[rows, hs]
            st = state[h]
            first.append((_dot_nt(q, k), _dot(q, st.astype(BF16)), v))
            kw_t = (k.astype(F32) * k_w[h]).T.astype(BF16)
            state[h] = math.exp(LOG_G[h] * n) * st + _dot(kw_t, v)
        yield
        for h, hs in enumerate(heads):
            s, cross, v = first[h]
            o = _dot((s * decay[h]).astype(BF16), v) + cross * cross_w[h]
            r = _group_norm_gate(o, rg[rows, hs], gnw_ref[:, hs], gnb_ref[:, hs])
            r_ref[rows, hs] = r.astype(BF16)


def _ret_step_kernel(rq_ref, rk_ref, rv_ref, rg_ref, st_ref, gnw_ref, gnb_ref, r_ref, new_ref):
    n = rq_ref.shape[1]
    row = lax.broadcasted_iota(jnp.int32, (n, LANES), 0).astype(F32)
    for b, h in [(b, h) for b in range(rq_ref.shape[0]) for h in range(RET_HEADS)]:
        hs = slice(h * HEAD_DIM, (h + 1) * HEAD_DIM)
        lg = LOG_G[h]
        q = rq_ref[b, :, hs].astype(F32)
        k = rk_ref[b, :, hs].astype(F32)
        v = rv_ref[b, :, hs].astype(F32)
        st = st_ref[b, h]
        q_pad = jnp.concatenate([q, jnp.zeros((16 - n, HEAD_DIM), F32)], axis=0).astype(BF16)
        o = _dot(q_pad, st.astype(BF16))[:n] * jnp.exp(lg * (row + 1.0))
        new = math.exp(lg * n) * st
        k_t = jnp.concatenate([k, jnp.zeros((8 - n, HEAD_DIM), F32)], axis=0).T
        for m in range(n):
            s_m = jnp.sum(q * k[m:m + 1, :], axis=-1, keepdims=True)
            decay = jnp.where(row >= m, jnp.exp(lg * jnp.maximum(row - m, 0.0)), 0.0)
            o = o + (s_m * decay) * v[m:m + 1, :]
            new = new + math.exp(lg * (n - 1.0 - m)) * (k_t[:, m:m + 1] * v[m:m + 1, :])
        new_ref[b, h] = new
        r = _group_norm_gate(o, rg_ref[b, :, hs], gnw_ref[:, hs], gnb_ref[:, hs])
        r_ref[b, :, hs] = r.astype(BF16)


def _sample_retention(rq, rk, rv, rg, state, gn_w, gn_b):
    db, ds = rq.shape[:2]
    nb = RET_SEQS_PER_STEP
    tok = pl.BlockSpec((nb, ds, HALF), lambda b: (b, 0, 0))
    st = pl.BlockSpec((nb, RET_HEADS, HEAD_DIM, HEAD_DIM), lambda b: (b, 0, 0, 0))
    return pl.pallas_call(
        _ret_step_kernel,
        grid=(db // nb,),
        in_specs=[tok, tok, tok, tok, st, _const_spec((1, HALF)), _const_spec((1, HALF))],
        out_specs=[tok, st],
        out_shape=[jax.ShapeDtypeStruct((db, ds, HALF), BF16),
                   jax.ShapeDtypeStruct(state.shape, F32)],
        compiler_params=pltpu.CompilerParams(dimension_semantics=("parallel",),
                                             vmem_limit_bytes=VMEM_LIMIT),
        name="sample_retention",
    )(rq, rk, rv, rg, state, gn_w, gn_b)


def _split_components(q):
    lo_mask = lax.broadcasted_iota(jnp.int32, q.shape, 1) < QK_DIM
    zero = jnp.zeros_like(q)
    return jnp.concatenate([jnp.where(lo_mask, q, zero), jnp.where(lo_mask, zero, q)], axis=0)


def _sub_norm(acc, l, lam, w):
    t = acc.shape[0] // 2
    o = acc / l
    d = o[:t] - lam * o[t:]
    return d * lax.rsqrt(jnp.mean(d * d, axis=-1, keepdims=True) + EPS) * w * (1.0 - LAM_INIT)


def _diff_kernel(q_ref, k_ref, vt_ref, lq1_ref, lk1_ref, lq2_ref, lk2_ref, sub_ref, o_ref):
    i = pl.program_id(1)
    t = DIFF_T
    lam = _lambda(lq1_ref[...], lk1_ref[...], lq2_ref[...], lk2_ref[...])
    k_idx = lax.broadcasted_iota(jnp.int32, (t, LANES), 0).astype(F32)
    key = lax.broadcasted_iota(jnp.int32, (t, 2 * t), 0)
    col = lax.broadcasted_iota(jnp.int32, (t, 2 * t), 1)
    causal = key <= jnp.where(col >= t, col - t, col)
    heads = [slice(h * HEAD_DIM, (h + 1) * HEAD_DIM) for h in range(DIFF_HEADS)]
    qqs = [_split_components(q_ref[:, hs]) for hs in heads]

    def block(j, carry, masked):
        start = pl.multiple_of(j * t, t)
        k_pos = k_idx + (j * t).astype(F32)
        out = []
        for h, hs in enumerate(heads):
            m, l, acc = carry[h]
            k = k_ref[pl.ds(start, t), hs]
            vt = vt_ref[j, hs, :]
            bias = (SLOPES[h] * LOG2E) * k_pos
            s = _dot_nt(k, qqs[h]) + jnp.concatenate([bias] * (2 * t // LANES), axis=1)
            if masked:
                s = jnp.where(causal, s, -jnp.inf)
            m_new = jnp.maximum(m, jnp.max(s, axis=0, keepdims=True))
            alpha = jnp.exp2(m - m_new)
            p = jnp.exp2(s - m_new)
            l = alpha * l + jnp.sum(p, axis=0, keepdims=True)
            acc = alpha * acc + _dot(vt, p.astype(BF16))
            out.append((m_new, l, acc))
        return tuple(out)

    init = tuple((jnp.full((1, 2 * t), NEG_BIG, F32), jnp.zeros((1, 2 * t), F32),
                  jnp.zeros((HEAD_DIM, 2 * t), F32)) for _ in heads)
    carry = lax.fori_loop(0, i, functools.partial(block, masked=False), init)
    carry = block(i, carry, masked=True)
    for (_, l, acc), hs in zip(carry, heads):
        o = acc / l
        d = o[:, :t] - lam * o[:, t:]
        dn = d * lax.rsqrt(jnp.mean(d * d, axis=0, keepdims=True) + EPS)
        o_ref[:, hs] = (dn.T * sub_ref[:, hs] * (1.0 - LAM_INIT)).astype(BF16)


def _prompt_diff_attention(dq, dk, dvt, lams, sub_w, batch, seq):
    nq = seq // DIFF_T
    qspec = pl.BlockSpec((DIFF_T, HALF), lambda b, i: (b * nq + i, 0))
    kspec = pl.BlockSpec((seq, HALF), lambda b, i: (b, 0))
    vspec = pl.BlockSpec((nq, HALF, DIFF_T), lambda b, i: (b, 0, 0))
    lspec = _const_spec((1, QK_DIM))
    return pl.pallas_call(
        _diff_kernel,
        grid=(batch, nq),
        in_specs=[qspec, kspec, vspec, lspec, lspec, lspec, lspec, _const_spec((1, HALF))],
        out_specs=qspec,
        out_shape=jax.ShapeDtypeStruct((batch * seq, HALF), BF16),
        compiler_params=pltpu.CompilerParams(dimension_semantics=("parallel", "arbitrary"),
                                             vmem_limit_bytes=VMEM_LIMIT),
        name="prompt_diff_attention",
    )(dq, dk, dvt, *lams, sub_w)


class _PagedDecoder:
    def __init__(self, pt_ref, q_ref, kn_ref, vn_ref, lam_refs, sub_ref, ck_hbm, cv_hbm, o_ref,
                 kbuf, vbuf, sems, bias_s, m_s, l_s, acc_s, n_groups):
        self.pt_ref, self.q_ref, self.kn_ref, self.vn_ref = pt_ref, q_ref, kn_ref, vn_ref
        self.lam_refs, self.sub_ref, self.ck_hbm, self.cv_hbm, self.o_ref = lam_refs, sub_ref, ck_hbm, cv_hbm, o_ref
        self.kbuf, self.vbuf, self.sems = kbuf, vbuf, sems
        self.bias_s, self.m_s, self.l_s, self.acc_s = bias_s, m_s, l_s, acc_s
        self.n = q_ref.shape[1]
        self.rows = DIFF_HEADS * 2 * self.n
        self.page_rows = PAGE * DIFF_HEADS
        self.past = n_groups * DECODE_PAGES * PAGE
        r1 = lax.broadcasted_iota(jnp.int32, (self.rows, 1), 0)
        self.head = r1 // (2 * self.n)
        self.qi = r1 % self.n
        self.slope = jnp.zeros((self.rows, 1), F32)
        for h in range(DIFF_HEADS):
            self.slope = jnp.where(self.head == h, SLOPES[h] * LOG2E, self.slope)

    def _copies(self, seq, group):
        slot = group % DECODE_SLOTS
        copies = []
        for p in range(DECODE_PAGES):
            page = self.pt_ref[seq, group * DECODE_PAGES + p]
            copies.append(pltpu.make_async_copy(self.ck_hbm.at[page], self.kbuf.at[slot, p], self.sems.at[0, slot]))
            copies.append(pltpu.make_async_copy(self.cv_hbm.at[page], self.vbuf.at[slot, p], self.sems.at[1, slot]))
        return copies

    def start(self, seq, group):
        for c in self._copies(seq, group):
            c.start()

    def wait(self, seq, group):
        for c in self._copies(seq, group):
            c.wait()

    def init_bias(self):
        col = lax.broadcasted_iota(jnp.int32, self.bias_s.shape, 1)
        dist = (self.past + self.qi - col // DIFF_HEADS).astype(F32)
        self.bias_s[...] = jnp.where(col % DIFF_HEADS == self.head, -self.slope * dist, -jnp.inf)

    def begin_sequence(self):
        self.m_s[...] = jnp.full_like(self.m_s, NEG_BIG)
        self.l_s[...] = jnp.zeros_like(self.l_s)
        self.acc_s[...] = jnp.zeros_like(self.acc_s)
        q = self.q_ref[0]
        self.qq = jnp.concatenate([_split_components(q[:, h * HEAD_DIM:(h + 1) * HEAD_DIM].astype(F32))
                                   for h in range(DIFF_HEADS)], axis=0).astype(BF16)

    def scores(self, g):
        slot, per_chain = g % DECODE_SLOTS, DECODE_PAGES // DECODE_CHAINS
        return [jnp.concatenate([_dot(self.qq, self.kbuf[slot, p].T.astype(BF16))
                                 for p in range(c * per_chain, (c + 1) * per_chain)], axis=-1)
                for c in range(DECODE_CHAINS)]

    def fold(self, g, scores):
        slot, page_rows = g % DECODE_SLOTS, self.page_rows
        per_chain = DECODE_PAGES // DECODE_CHAINS
        chains = []
        for c, s in enumerate(scores):
            pages = range(c * per_chain, (c + 1) * per_chain)
            s = s + jnp.concatenate([self.bias_s[...] + self.slope * float((g * DECODE_PAGES + p) * PAGE)
                                     for p in pages], axis=-1)
            m_c = jnp.max(s, axis=-1, keepdims=True)
            pb = jnp.exp2(s - m_c)
            l_c = jnp.sum(pb, axis=-1, keepdims=True)
            pb = pb.astype(BF16)
            pv_c = _dot(pb[:, :page_rows], self.vbuf[slot, pages[0]].astype(BF16))
            for idx, pg in enumerate(pages[1:], start=1):
                pv_c = pv_c + _dot(pb[:, idx * page_rows:(idx + 1) * page_rows], self.vbuf[slot, pg].astype(BF16))
            chains.append((m_c, l_c, pv_c))
        m = self.m_s[:, :1]
        m_new = m
        for m_c, _, _ in chains:
            m_new = jnp.maximum(m_new, m_c)
        alpha = jnp.exp2(m - m_new)
        l_new = alpha * self.l_s[:, :1]
        acc = alpha * self.acc_s[...]
        for m_c, l_c, pv_c in chains:
            w = jnp.exp2(m_c - m_new)
            l_new = l_new + w * l_c
            acc = acc + w * pv_c
        self.m_s[...] = jnp.broadcast_to(m_new, (self.rows, LANES))
        self.l_s[...] = jnp.broadcast_to(l_new, (self.rows, LANES))
        self.acc_s[...] = acc

    def finish(self):
        n = self.n
        lam = _lambda(*(ref[...] for ref in self.lam_refs))
        r8 = lax.broadcasted_iota(jnp.int32, (2 * n, 1), 0)
        qi = jnp.where(r8 >= n, r8 - n, r8)
        for h in range(DIFF_HEADS):
            hs = slice(h * HEAD_DIM, (h + 1) * HEAD_DIM)
            hr = slice(h * 2 * n, (h + 1) * 2 * n)
            qq = _split_components(self.q_ref[0, :, hs].astype(F32))
            kn = self.kn_ref[:, h, :]
            vn = self.vn_ref[:, h, :]
            m, l, acc = self.m_s[hr, :1], self.l_s[hr, :1], self.acc_s[hr, :]
            s_cols = []
            for t in range(n):
                s_t = jnp.sum(qq * kn[t:t + 1, :], axis=-1, keepdims=True)
                s_t = s_t - SLOPES[h] * LOG2E * (qi - t).astype(F32)
                s_cols.append(jnp.where(qi >= t, s_t, -jnp.inf))
            m_new = m
            for s_t in s_cols:
                m_new = jnp.maximum(m_new, s_t)
            alpha = jnp.exp2(m - m_new)
            l = alpha * l
            acc = alpha * acc
            for t in range(n):
                p_t = jnp.exp2(s_cols[t] - m_new)
                l = l + p_t
                acc = acc + p_t * vn[t:t + 1, :]
            self.o_ref[0, :, hs] = _sub_norm(acc, l, lam, self.sub_ref[:, hs]).astype(BF16)


def _channel_phases(x_ref, r_ref, dn_ref, p_ref, wo_ref, ln2_ref, wfi_ref, wfo_ref, lnp_ref, wpg_ref, wpp_ref,
                    y_ref):
    st = {}

    def attention_out():
        mixed = jnp.concatenate([r_ref[...], dn_ref[...]], axis=-1)
        st["h"] = x_ref[...] + _dot(mixed, wo_ref[...])
        st["hn"] = _rms(st["h"], ln2_ref[...]).astype(BF16)

    def ffn_chunk(c):
        g = _dot(st["hn"], wfi_ref[:, c * FF_CHUNK:(c + 1) * FF_CHUNK])
        u = _dot(st["hn"], wfi_ref[:, D_FF + c * FF_CHUNK:D_FF + (c + 1) * FF_CHUNK])
        act = (g * _sigmoid(g) * u).astype(BF16)
        part = _dot(act, wfo_ref[c * FF_CHUNK:(c + 1) * FF_CHUNK, :])
        st["ff"] = part if c == 0 else st["ff"] + part

    def embedding_gate():
        h = st["h"] + st["ff"]
        gate = _sigmoid(_dot(_rms(h, lnp_ref[...]).astype(BF16), wpg_ref[...]))
        y_ref[...] = h + gate * _dot(p_ref[...].astype(BF16), wpp_ref[...])

    return ([attention_out] + [functools.partial(ffn_chunk, c) for c in range(D_FF // FF_CHUNK)]
            + [embedding_gate])


def _channel_kernel(*refs):
    for phase in _channel_phases(*refs):
        phase()


def _channel_specs(tm, index):
    tok = lambda width: pl.BlockSpec((tm, width), index)
    return [tok(D_MODEL), tok(HALF), tok(HALF), tok(PLE_DIM),
            _const_spec((D_MODEL, D_MODEL)), _const_spec((1, D_MODEL)),
            _const_spec((D_MODEL, 2 * D_FF)), _const_spec((D_FF, D_MODEL)),
            _const_spec((1, D_MODEL)), _const_spec((D_MODEL, D_MODEL)),
            _const_spec((PLE_DIM, D_MODEL))], tok(D_MODEL)


def _channel(x, r, dn, p, chan_w, tm):
    m = x.shape[0]
    in_specs, out_spec = _channel_specs(tm, lambda i: (i, 0))
    return pl.pallas_call(
        _channel_kernel,
        grid=(m // tm,),
        in_specs=in_specs,
        out_specs=out_spec,
        out_shape=jax.ShapeDtypeStruct((m, D_MODEL), F32),
        compiler_params=pltpu.CompilerParams(dimension_semantics=("parallel",),
                                             vmem_limit_bytes=VMEM_LIMIT),
        name="channel",
    )(x, r, dn, p, *chan_w)


def _channel_decode_kernel(pt_ref, x_ref, r_ref, dn_ref, p_ref, wo_ref, ln2_ref, wfi_ref, wfo_ref, lnp_ref, wpg_ref,
                           wpp_ref, q_ref, kn_ref, vn_ref, lq1_ref, lk1_ref, lq2_ref, lk2_ref, sub_ref, ck_hbm,
                           cv_hbm, y_ref, o_ref, kbuf, vbuf, sems, bias_s, m_s, l_s, acc_s, *, n_seq, n_groups):
    i = pl.program_id(0)
    ahead = DECODE_SLOTS - 1
    chan_refs = (x_ref, r_ref, dn_ref, p_ref, wo_ref, ln2_ref, wfi_ref, wfo_ref, lnp_ref, wpg_ref, wpp_ref)
    dec = _PagedDecoder(pt_ref, q_ref, kn_ref, vn_ref, (lq1_ref, lk1_ref, lq2_ref, lk2_ref), sub_ref, ck_hbm,
                        cv_hbm, o_ref, kbuf, vbuf, sems, bias_s, m_s, l_s, acc_s, n_groups)

    @pl.when(i == 0)
    def _():
        dec.init_bias()
        for g in range(ahead):
            dec.start(0, g)

    dec.begin_sequence()
    phases = _channel_phases(*chan_refs, y_ref)
    per_group = -(-len(phases) // n_groups)
    for g in range(n_groups):
        nxt = g + ahead
        if nxt < n_groups:
            dec.start(i, nxt)
        else:
            pl.when(i + 1 < n_seq)(functools.partial(dec.start, i + 1, nxt - n_groups))
        dec.wait(i, g)
        scores = dec.scores(g)
        for phase in phases[g * per_group:(g + 1) * per_group]:
            phase()
        dec.fold(g, scores)
    dec.finish()


def _channel_and_decode(x, r, dn, p, chan_w, page_table, dq, dk_new, dv_new, cache_k, cache_v, lams, sub_w):
    m = x.shape[0]
    db, ds = dq.shape[:2]
    tm = m // db
    assert tm * db == m and tm % 256 == 0, (m, db)
    n_groups = page_table.shape[1] // DECODE_PAGES
    assert n_groups * DECODE_PAGES == page_table.shape[1] and n_groups % DECODE_SLOTS == 0
    chan_in, chan_out = _channel_specs(tm, lambda i, pt: (i, 0))
    tok = pl.BlockSpec((1, ds, HALF), lambda i, pt: (i, 0, 0))
    new = pl.BlockSpec((None, ds, DIFF_HEADS, HEAD_DIM), lambda i, pt: (i, 0, 0, 0))
    page_rows = PAGE * DIFF_HEADS
    cache_k, cache_v = (c.reshape(-1, page_rows, HEAD_DIM) for c in (cache_k, cache_v))
    hbm = pl.BlockSpec(memory_space=pl.ANY)
    ring = pltpu.VMEM((DECODE_SLOTS, DECODE_PAGES, page_rows, HEAD_DIM), F32)
    state = pltpu.VMEM((DIFF_HEADS * 2 * ds, LANES), F32)
    bias = pltpu.VMEM((DIFF_HEADS * 2 * ds, page_rows), F32)
    return pl.pallas_call(
        functools.partial(_channel_decode_kernel, n_seq=db, n_groups=n_groups),
        grid_spec=pltpu.PrefetchScalarGridSpec(
            num_scalar_prefetch=1,
            grid=(db,),
            in_specs=chan_in + [tok, new, new] + [_const_spec((1, QK_DIM))] * 4 + [_const_spec((1, HALF)), hbm, hbm],
            out_specs=[chan_out, tok],
            scratch_shapes=[ring, ring, pltpu.SemaphoreType.DMA((2, DECODE_SLOTS)), bias, state, state, state],
        ),
        out_shape=[jax.ShapeDtypeStruct((m, D_MODEL), F32), jax.ShapeDtypeStruct((db, ds, HALF), BF16)],
        compiler_params=pltpu.CompilerParams(dimension_semantics=("arbitrary",),
                                             vmem_limit_bytes=FUSED_VMEM_LIMIT),
        name="channel_and_decode",
    )(page_table, x, r, dn, p, *chan_w, dq, dk_new, dv_new, *lams, sub_w, cache_k, cache_v)


def kernel(x_prompt, x_sample, cache_k, cache_v, state_ret, page_table, p_prompt, p_sample, ln1_w, w_in, q_norm_w, k_norm_w, lambda_q1, lambda_k1, lambda_q2, lambda_k2, ret_gn_w, ret_gn_b, diff_subln_w, w_o, ln2_w, w_ffn_in, w_ffn_out, ln_ple_w, w_ple_gate, w_ple_proj):
    depth = w_in.shape[0]
    assert depth == 1, "single-layer trunk"
    b, s, _ = x_prompt.shape
    db, ds, _ = x_sample.shape

    row = lambda a: a[0].reshape(1, -1)
    wb = lambda a: a[0].astype(BF16)
    qn_w = jnp.tile(row(q_norm_w), (1, HALF // QK_DIM))
    kn_w = jnp.tile(row(k_norm_w), (1, HALF // QK_DIM))
    lams = (row(lambda_q1), row(lambda_k1), row(lambda_q2), row(lambda_k2))
    proj_w = (row(ln1_w), wb(w_in), qn_w, kn_w)
    chan_w = (wb(w_o), row(ln2_w), wb(w_ffn_in), wb(w_ffn_out), row(ln_ple_w), wb(w_ple_gate), wb(w_ple_proj))
    gn_w, gn_b, sub_w = row(ret_gn_w), row(ret_gn_b), row(diff_subln_w)

    xp = x_prompt.reshape(b * s, D_MODEL)
    dq, dk, dv, dkb, dvt, r, ret_fin = _project_and_retain(xp, *proj_w, gn_w, gn_b, b, s)
    dn = _prompt_diff_attention(dq, dkb, dvt, lams, sub_w, b, s)

    xs = x_sample.reshape(db * ds, D_MODEL)
    rq_s, rk_s, rv_s, rg_s, dq_s, dk_s, dv_s, _, _ = _project(xs, *proj_w, tm=db * ds)
    tok3 = lambda a: a.reshape(db, ds, HALF)
    r_s, ret_new = _sample_retention(tok3(rq_s), tok3(rk_s), tok3(rv_s), tok3(rg_s), state_ret[0], gn_w, gn_b)
    new4 = lambda a: a.reshape(db, ds, DIFF_HEADS, HEAD_DIM)
    y_prompt, dn_s = _channel_and_decode(xp, r, dn, p_prompt[0].reshape(b * s, PLE_DIM), chan_w, page_table,
                                         tok3(dq_s), new4(dk_s), new4(dv_s), cache_k, cache_v, lams, sub_w)
    y_sample = _channel(xs, r_s.reshape(db * ds, HALF), dn_s.reshape(db * ds, HALF),
                        p_sample[0].reshape(db * ds, PLE_DIM), chan_w, tm=db * ds)

    heads = lambda a, n: a.reshape(1, n, -1, DIFF_HEADS, HEAD_DIM)
    return (y_prompt.reshape(b, s, D_MODEL), y_sample.reshape(db, ds, D_MODEL),
            heads(dk, b), heads(dv, b), ret_fin[None],
            heads(dk_s, db), heads(dv_s, db), ret_new[None])
```

```python
import functools
import math

import jax
import jax.numpy as jnp
from jax import lax
from jax.experimental import pallas as pl
from jax.experimental.pallas import tpu as pltpu

F32 = jnp.float32
BF16 = jnp.bfloat16

D_MODEL = 1024
RET_HEADS = 4
HEAD_DIM = 128
DIFF_HEADS = 4
QK_DIM = 64
HALF = RET_HEADS * HEAD_DIM
PROJ_WIDTH = 7 * HALF
D_FF = 2816
PLE_DIM = 256
RET_CHUNK = 128
PAGE = 128
EPS = 1e-6
LAM_INIT = 0.8 - 0.6 * math.exp(-0.3 * 0)
LOG_G = tuple(math.log1p(-(2.0 ** (-5.0 - h))) for h in range(RET_HEADS))
SLOPES = tuple(2.0 ** (-8.0 / DIFF_HEADS * (h + 1)) for h in range(DIFF_HEADS))
LOG2E = math.log2(math.e)
NEG_BIG = -1e30

LANES = 128
VMEM_LIMIT = 56 * 1024 * 1024
FUSED_VMEM_LIMIT = 61 * 1024 * 1024

FF_CHUNK = 256
RET_SEQS_PER_STEP = 4
DIFF_T = 512
DECODE_PAGES = 16
DECODE_CHAINS = 2
DECODE_SLOTS = 2


def _sigmoid(x):
    return 1.0 / (1.0 + jnp.exp(-x))


def _rms(x, w):
    return x * lax.rsqrt(jnp.mean(x * x, axis=-1, keepdims=True) + EPS) * w


def _dot(a, b):
    return jnp.dot(a, b, preferred_element_type=F32)


def _dot_nt(a, b):
    return lax.dot_general(a, b, (((1,), (1,)), ((), ())), preferred_element_type=F32)


def _lambda(lq1, lk1, lq2, lk2):
    a = jnp.exp(jnp.sum(lq1 * lk1, axis=-1, keepdims=True))
    b = jnp.exp(jnp.sum(lq2 * lk2, axis=-1, keepdims=True))
    return a - b + LAM_INIT


def _const_spec(shape):
    nd = len(shape)
    return pl.BlockSpec(shape, lambda *_: (0,) * nd, pipeline_mode=pl.Buffered(1))


def _seg_rms(y, w):
    lo_mask = lax.broadcasted_iota(jnp.int32, (1, LANES), 1) < QK_DIM
    outs = []
    for g in range(HALF // LANES):
        yg = y[:, g * LANES:(g + 1) * LANES]
        t = yg * yg
        lo = jnp.sum(jnp.where(lo_mask, t, 0.0), axis=-1, keepdims=True)
        hi = jnp.sum(jnp.where(lo_mask, 0.0, t), axis=-1, keepdims=True)
        ms = jnp.where(lo_mask, lo, hi) * (1.0 / QK_DIM)
        outs.append(yg * lax.rsqrt(ms + EPS))
    return jnp.concatenate(outs, axis=-1) * w


def _store_cache_format(ref, y):
    for h in range(DIFF_HEADS):
        ref[pl.ds(h, y.shape[0], stride=DIFF_HEADS), :] = y[:, h * HEAD_DIM:(h + 1) * HEAD_DIM]


def _projection(x_ref, ln1_ref, w_ref):
    xn = _rms(x_ref[...], ln1_ref[...]).astype(BF16)
    return lambda i: _dot(xn, w_ref[:, i * HALF:(i + 1) * HALF])


def _retention_operands(col):
    return col(0).astype(BF16), (col(1) * (HEAD_DIM ** -0.5)).astype(BF16), col(2).astype(BF16), col(3)


def _diff_projection_steps(col, qn_ref, kn_ref, dq_ref, dk_ref, dv_ref, dkb_ref, dvt_ref):
    def q_step():
        dq = _seg_rms(col(4), qn_ref[...])
        dq_ref[...] = (dq * (LOG2E * QK_DIM ** -0.5)).astype(BF16)

    def k_step():
        dk = _seg_rms(col(5), kn_ref[...])
        dkb_ref[...] = dk.astype(BF16)
        _store_cache_format(dk_ref, dk)

    def v_step():
        dv = col(6)
        dvt_ref[0] = dv.T.astype(BF16)
        _store_cache_format(dv_ref, dv)

    return [q_step, k_step, v_step]


def _proj_kernel(x_ref, ln1_ref, w_ref, qn_ref, kn_ref,
                 rq_ref, rk_ref, rv_ref, rg_ref, dq_ref, dk_ref, dv_ref, dkb_ref, dvt_ref):
    col = _projection(x_ref, ln1_ref, w_ref)
    for step in _diff_projection_steps(col, qn_ref, kn_ref, dq_ref, dk_ref, dv_ref, dkb_ref, dvt_ref):
        step()
    rq_ref[...], rk_ref[...], rv_ref[...], rg_ref[...] = _retention_operands(col)


def _proj_retention_kernel(x_ref, ln1_ref, w_ref, qn_ref, kn_ref, gnw_ref, gnb_ref,
                           dq_ref, dk_ref, dv_ref, dkb_ref, dvt_ref, r_ref, fin_ref, state, *, tiles_per_seq):
    i = pl.program_id(0)

    @pl.when(i % tiles_per_seq == 0)
    def _():
        state[...] = jnp.zeros_like(state)

    col = _projection(x_ref, ln1_ref, w_ref)
    for step in _diff_projection_steps(col, qn_ref, kn_ref, dq_ref, dk_ref, dv_ref, dkb_ref, dvt_ref):
        step()
    _retention_tile(*_retention_operands(col), gnw_ref, gnb_ref, state, r_ref)

    @pl.when((i + 1) % tiles_per_seq == 0)
    def _():
        fin_ref[0] = state[...]


def _proj_specs(m, tm):
    tok = lambda width: pl.BlockSpec((tm, width), lambda i: (i, 0))
    half = lambda dt: jax.ShapeDtypeStruct((m, HALF), dt)
    cache_fmt = pl.BlockSpec((tm * DIFF_HEADS, HEAD_DIM), lambda i: (i, 0))
    cache_shape = jax.ShapeDtypeStruct((m * DIFF_HEADS, HEAD_DIM), F32)
    in_specs = [tok(D_MODEL), _const_spec((1, D_MODEL)), _const_spec((D_MODEL, PROJ_WIDTH)),
                _const_spec((1, HALF)), _const_spec((1, HALF))]
    diff_specs = [tok(HALF), cache_fmt, cache_fmt, tok(HALF), pl.BlockSpec((1, HALF, tm), lambda i: (i, 0, 0))]
    diff_shapes = [half(BF16), cache_shape, cache_shape, half(BF16),
                   jax.ShapeDtypeStruct((m // tm, HALF, tm), BF16)]
    return tok, half, in_specs, diff_specs, diff_shapes


def _project(x, ln1_w, w_in, qn_w, kn_w, tm):
    m = x.shape[0]
    tok, half, in_specs, diff_specs, diff_shapes = _proj_specs(m, tm)
    return pl.pallas_call(
        _proj_kernel,
        grid=(m // tm,),
        in_specs=in_specs,
        out_specs=[tok(HALF)] * 4 + diff_specs,
        out_shape=[half(BF16), half(BF16), half(BF16), half(F32)] + diff_shapes,
        compiler_params=pltpu.CompilerParams(dimension_semantics=("parallel",),
                                             vmem_limit_bytes=VMEM_LIMIT),
        name="proj",
    )(x, ln1_w, w_in, qn_w, kn_w)


def _project_and_retain(x, ln1_w, w_in, qn_w, kn_w, gn_w, gn_b, batch, seq):
    m, tm = x.shape[0], DIFF_T
    tok, half, in_specs, diff_specs, diff_shapes = _proj_specs(m, tm)
    tiles_per_seq = seq // tm
    state = (RET_HEADS, HEAD_DIM, HEAD_DIM)
    return pl.pallas_call(
        functools.partial(_proj_retention_kernel, tiles_per_seq=tiles_per_seq),
        grid=(m // tm,),
        in_specs=in_specs + [_const_spec((1, HALF)), _const_spec((1, HALF))],
        out_specs=diff_specs + [tok(HALF), pl.BlockSpec((1,) + state, lambda i: (i // tiles_per_seq, 0, 0, 0))],
        out_shape=diff_shapes + [half(BF16), jax.ShapeDtypeStruct((batch,) + state, F32)],
        scratch_shapes=[pltpu.VMEM(state, F32)],
        compiler_params=pltpu.CompilerParams(dimension_semantics=("arbitrary",),
                                             vmem_limit_bytes=VMEM_LIMIT),
        name="proj_retention",
    )(x, ln1_w, w_in, qn_w, kn_w, gn_w, gn_b)


def _group_norm_gate(o, g, gn_w, gn_b):
    mu = jnp.mean(o, axis=-1, keepdims=True)
    d = o - mu
    var = jnp.mean(d * d, axis=-1, keepdims=True)
    r = d * lax.rsqrt(var + EPS) * gn_w + gn_b
    return r * (g * _sigmoid(g))


def _retention_tile(rq, rk, rv, rg, gnw_ref, gnb_ref, state, r_ref):
    n = RET_CHUNK
    row = lax.broadcasted_iota(jnp.int32, (n, n), 0).astype(F32)
    col = lax.broadcasted_iota(jnp.int32, (n, n), 1).astype(F32)
    rel = row - col
    heads = [slice(h * HEAD_DIM, (h + 1) * HEAD_DIM) for h in range(RET_HEADS)]
    decay = [jnp.where(rel >= 0, jnp.exp(lg * jnp.maximum(rel, 0.0)), 0.0) for lg in LOG_G]
    cross_w = [jnp.exp(lg * (row + 1.0)) for lg in LOG_G]
    k_w = [jnp.exp(lg * (n - 1.0 - row)) for lg in LOG_G]
    for c in range(rq.shape[0] // n):
        rows = slice(c * n, (c + 1) * n)
        first = []
        for h, hs in enumerate(heads):
            q, k, v = rq[rows, hs], rk[rows, hs], rv[rows, hs]
            st = state[h]
            first.append((_dot_nt(q, k), _dot(q, st.astype(BF16)), v))
            kw_t = (k.astype(F32) * k_w[h]).T.astype(BF16)
            state[h] = math.exp(LOG_G[h] * n) * st + _dot(kw_t, v)
        for h, hs in enumerate(heads):
            s, cross, v = first[h]
            o = _dot((s * decay[h]).astype(BF16), v) + cross * cross_w[h]
            r = _group_norm_gate(o, rg[rows, hs], gnw_ref[:, hs], gnb_ref[:, hs])
            r_ref[rows, hs] = r.astype(BF16)


def _ret_step_kernel(rq_ref, rk_ref, rv_ref, rg_ref, st_ref, gnw_ref, gnb_ref, r_ref, new_ref):
    n = rq_ref.shape[1]
    row = lax.broadcasted_iota(jnp.int32, (n, LANES), 0).astype(F32)
    for b, h in [(b, h) for b in range(rq_ref.shape[0]) for h in range(RET_HEADS)]:
        hs = slice(h * HEAD_DIM, (h + 1) * HEAD_DIM)
        lg = LOG_G[h]
        q = rq_ref[b, :, hs].astype(F32)
        k = rk_ref[b, :, hs].astype(F32)
        v = rv_ref[b, :, hs].astype(F32)
        st = st_ref[b, h]
        q_pad = jnp.concatenate([q, jnp.zeros((16 - n, HEAD_DIM), F32)], axis=0).astype(BF16)
        o = _dot(q_pad, st.astype(BF16))[:n] * jnp.exp(lg * (row + 1.0))
        new = math.exp(lg * n) * st
        k_t = jnp.concatenate([k, jnp.zeros((8 - n, HEAD_DIM), F32)], axis=0).T
        for m in range(n):
            s_m = jnp.sum(q * k[m:m + 1, :], axis=-1, keepdims=True)
            decay = jnp.where(row >= m, jnp.exp(lg * jnp.maximum(row - m, 0.0)), 0.0)
            o = o + (s_m * decay) * v[m:m + 1, :]
            new = new + math.exp(lg * (n - 1.0 - m)) * (k_t[:, m:m + 1] * v[m:m + 1, :])
        new_ref[b, h] = new
        r = _group_norm_gate(o, rg_ref[b, :, hs], gnw_ref[:, hs], gnb_ref[:, hs])
        r_ref[b, :, hs] = r.astype(BF16)


def _sample_retention(rq, rk, rv, rg, state, gn_w, gn_b):
    db, ds = rq.shape[:2]
    nb = RET_SEQS_PER_STEP
    tok = pl.BlockSpec((nb, ds, HALF), lambda b: (b, 0, 0))
    st = pl.BlockSpec((nb, RET_HEADS, HEAD_DIM, HEAD_DIM), lambda b: (b, 0, 0, 0))
    return pl.pallas_call(
        _ret_step_kernel,
        grid=(db // nb,),
        in_specs=[tok, tok, tok, tok, st, _const_spec((1, HALF)), _const_spec((1, HALF))],
        out_specs=[tok, st],
        out_shape=[jax.ShapeDtypeStruct((db, ds, HALF), BF16),
                   jax.ShapeDtypeStruct(state.shape, F32)],
        compiler_params=pltpu.CompilerParams(dimension_semantics=("parallel",),
                                             vmem_limit_bytes=VMEM_LIMIT),
        name="sample_retention",
    )(rq, rk, rv, rg, state, gn_w, gn_b)


def _split_components(q):
    lo_mask = lax.broadcasted_iota(jnp.int32, q.shape, 1) < QK_DIM
    zero = jnp.zeros_like(q)
    return jnp.concatenate([jnp.where(lo_mask, q, zero), jnp.where(lo_mask, zero, q)], axis=0)


def _sub_norm(acc, l, lam, w):
    t = acc.shape[0] // 2
    o = acc / l
    d = o[:t] - lam * o[t:]
    return d * lax.rsqrt(jnp.mean(d * d, axis=-1, keepdims=True) + EPS) * w * (1.0 - LAM_INIT)


def _diff_kernel(q_ref, k_ref, vt_ref, lq1_ref, lk1_ref, lq2_ref, lk2_ref, sub_ref, o_ref):
    i = pl.program_id(1)
    t = DIFF_T
    lam = _lambda(lq1_ref[...], lk1_ref[...], lq2_ref[...], lk2_ref[...])
    k_idx = lax.broadcasted_iota(jnp.int32, (t, LANES), 0).astype(F32)
    key = lax.broadcasted_iota(jnp.int32, (t, 2 * t), 0)
    col = lax.broadcasted_iota(jnp.int32, (t, 2 * t), 1)
    causal = key <= jnp.where(col >= t, col - t, col)
    heads = [slice(h * HEAD_DIM, (h + 1) * HEAD_DIM) for h in range(DIFF_HEADS)]
    qqs = [_split_components(q_ref[:, hs]) for hs in heads]

    def block(j, carry, masked):
        start = pl.multiple_of(j * t, t)
        k_pos = k_idx + (j * t).astype(F32)
        scores = [_dot_nt(k_ref[pl.ds(start, t), hs], qqs[h]) for h, hs in enumerate(heads)]

        def softmax(h):
            m, l, _ = carry[h]
            bias = (SLOPES[h] * LOG2E) * k_pos
            s = scores[h] + jnp.concatenate([bias] * (2 * t // LANES), axis=1)
            if masked:
                s = jnp.where(causal, s, -jnp.inf)
            m_new = jnp.maximum(m, jnp.max(s, axis=0, keepdims=True))
            alpha = jnp.exp2(m - m_new)
            p = jnp.exp2(s - m_new)
            return m_new, alpha * l + jnp.sum(p, axis=0, keepdims=True), alpha, p.astype(BF16)

        def values(h, soft):
            m_new, l, alpha, p = soft
            return m_new, l, alpha * carry[h][2] + _dot(vt_ref[j, heads[h], :], p)

        soft = [softmax(h) for h in range(DIFF_HEADS)]
        return tuple(values(h, soft[h]) for h in range(DIFF_HEADS))

    init = tuple((jnp.full((1, 2 * t), NEG_BIG, F32), jnp.zeros((1, 2 * t), F32),
                  jnp.zeros((HEAD_DIM, 2 * t), F32)) for _ in heads)
    carry = lax.fori_loop(0, i, functools.partial(block, masked=False), init)
    carry = block(i, carry, masked=True)
    for (_, l, acc), hs in zip(carry, heads):
        o = acc / l
        d = o[:, :t] - lam * o[:, t:]
        dn = d * lax.rsqrt(jnp.mean(d * d, axis=0, keepdims=True) + EPS)
        o_ref[:, hs] = (dn.T * sub_ref[:, hs] * (1.0 - LAM_INIT)).astype(BF16)


def _prompt_diff_attention(dq, dk, dvt, lams, sub_w, batch, seq):
    nq = seq // DIFF_T
    qspec = pl.BlockSpec((DIFF_T, HALF), lambda b, i: (b * nq + i, 0))
    kspec = pl.BlockSpec((seq, HALF), lambda b, i: (b, 0))
    vspec = pl.BlockSpec((nq, HALF, DIFF_T), lambda b, i: (b, 0, 0))
    lspec = _const_spec((1, QK_DIM))
    return pl.pallas_call(
        _diff_kernel,
        grid=(batch, nq),
        in_specs=[qspec, kspec, vspec, lspec, lspec, lspec, lspec, _const_spec((1, HALF))],
        out_specs=qspec,
        out_shape=jax.ShapeDtypeStruct((batch * seq, HALF), BF16),
        compiler_params=pltpu.CompilerParams(dimension_semantics=("parallel", "arbitrary"),
                                             vmem_limit_bytes=VMEM_LIMIT),
        name="prompt_diff_attention",
    )(dq, dk, dvt, *lams, sub_w)


class _PagedDecoder:
    def __init__(self, pt_ref, q_ref, kn_ref, vn_ref, lam_refs, sub_ref, ck_hbm, cv_hbm, o_ref,
                 kbuf, vbuf, sems, bias_s, m_s, l_s, acc_s, n_groups):
        self.pt_ref, self.q_ref, self.kn_ref, self.vn_ref = pt_ref, q_ref, kn_ref, vn_ref
        self.lam_refs, self.sub_ref, self.ck_hbm, self.cv_hbm, self.o_ref = lam_refs, sub_ref, ck_hbm, cv_hbm, o_ref
        self.kbuf, self.vbuf, self.sems = kbuf, vbuf, sems
        self.bias_s, self.m_s, self.l_s, self.acc_s = bias_s, m_s, l_s, acc_s
        self.n = q_ref.shape[1]
        self.rows = DIFF_HEADS * 2 * self.n
        self.page_rows = PAGE * DIFF_HEADS
        self.past = n_groups * DECODE_PAGES * PAGE
        r1 = lax.broadcasted_iota(jnp.int32, (self.rows, 1), 0)
        self.head = r1 // (2 * self.n)
        self.qi = r1 % self.n
        self.slope = jnp.zeros((self.rows, 1), F32)
        for h in range(DIFF_HEADS):
            self.slope = jnp.where(self.head == h, SLOPES[h] * LOG2E, self.slope)

    def _copies(self, seq, group):
        slot = group % DECODE_SLOTS
        copies = []
        for p in range(DECODE_PAGES):
            page = self.pt_ref[seq, group * DECODE_PAGES + p]
            copies.append(pltpu.make_async_copy(self.ck_hbm.at[page], self.kbuf.at[slot, p], self.sems.at[0, slot]))
            copies.append(pltpu.make_async_copy(self.cv_hbm.at[page], self.vbuf.at[slot, p], self.sems.at[1, slot]))
        return copies

    def start(self, seq, group):
        for c in self._copies(seq, group):
            c.start()

    def wait(self, seq, group):
        for c in self._copies(seq, group):
            c.wait()

    def init_bias(self):
        col = lax.broadcasted_iota(jnp.int32, self.bias_s.shape, 1)
        dist = (self.past + self.qi - col // DIFF_HEADS).astype(F32)
        self.bias_s[...] = jnp.where(col % DIFF_HEADS == self.head, -self.slope * dist, -jnp.inf)

    def begin_sequence(self):
        self.m_s[...] = jnp.full_like(self.m_s, NEG_BIG)
        self.l_s[...] = jnp.zeros_like(self.l_s)
        self.acc_s[...] = jnp.zeros_like(self.acc_s)
        q = self.q_ref[0]
        self.qq = jnp.concatenate([_split_components(q[:, h * HEAD_DIM:(h + 1) * HEAD_DIM].astype(F32))
                                   for h in range(DIFF_HEADS)], axis=0).astype(BF16)

    def scores(self, g):
        slot, per_chain = g % DECODE_SLOTS, DECODE_PAGES // DECODE_CHAINS
        return [jnp.concatenate([_dot_nt(self.qq, self.kbuf[slot, p].astype(BF16))
                                 for p in range(c * per_chain, (c + 1) * per_chain)], axis=-1)
                for c in range(DECODE_CHAINS)]

    def fold(self, g, scores):
        slot, page_rows = g % DECODE_SLOTS, self.page_rows
        per_chain = DECODE_PAGES // DECODE_CHAINS
        chains = []
        for c, s in enumerate(scores):
            pages = range(c * per_chain, (c + 1) * per_chain)
            s = s + jnp.concatenate([self.bias_s[...] + self.slope * float((g * DECODE_PAGES + p) * PAGE)
                                     for p in pages], axis=-1)
            m_c = jnp.max(s, axis=-1, keepdims=True)
            pb = jnp.exp2(s - m_c)
            l_c = jnp.sum(pb, axis=-1, keepdims=True)
            pb = pb.astype(BF16)
            pv_c = _dot(pb[:, :page_rows], self.vbuf[slot, pages[0]].astype(BF16))
            for idx, pg in enumerate(pages[1:], start=1):
                pv_c = pv_c + _dot(pb[:, idx * page_rows:(idx + 1) * page_rows], self.vbuf[slot, pg].astype(BF16))
            chains.append((m_c, l_c, pv_c))
        m = self.m_s[:, :1]
        m_new = m
        for m_c, _, _ in chains:
            m_new = jnp.maximum(m_new, m_c)
        alpha = jnp.exp2(m - m_new)
        l_new = alpha * self.l_s[:, :1]
        acc = alpha * self.acc_s[...]
        for m_c, l_c, pv_c in chains:
            w = jnp.exp2(m_c - m_new)
            l_new = l_new + w * l_c
            acc = acc + w * pv_c
        self.m_s[...] = jnp.broadcast_to(m_new, (self.rows, LANES))
        self.l_s[...] = jnp.broadcast_to(l_new, (self.rows, LANES))
        self.acc_s[...] = acc

    def finish(self):
        n = self.n
        lam = _lambda(*(ref[...] for ref in self.lam_refs))
        r8 = lax.broadcasted_iota(jnp.int32, (2 * n, 1), 0)
        qi = jnp.where(r8 >= n, r8 - n, r8)
        for h in range(DIFF_HEADS):
            hs = slice(h * HEAD_DIM, (h + 1) * HEAD_DIM)
            hr = slice(h * 2 * n, (h + 1) * 2 * n)
            qq = _split_components(self.q_ref[0, :, hs].astype(F32))
            kn = self.kn_ref[:, h, :]
            vn = self.vn_ref[:, h, :]
            m, l, acc = self.m_s[hr, :1], self.l_s[hr, :1], self.acc_s[hr, :]
            s_cols = []
            for t in range(n):
                s_t = jnp.sum(qq * kn[t:t + 1, :], axis=-1, keepdims=True)
                s_t = s_t - SLOPES[h] * LOG2E * (qi - t).astype(F32)
                s_cols.append(jnp.where(qi >= t, s_t, -jnp.inf))
            m_new = m
            for s_t in s_cols:
                m_new = jnp.maximum(m_new, s_t)
            alpha = jnp.exp2(m - m_new)
            l = alpha * l
            acc = alpha * acc
            for t in range(n):
                p_t = jnp.exp2(s_cols[t] - m_new)
                l = l + p_t
                acc = acc + p_t * vn[t:t + 1, :]
            self.o_ref[0, :, hs] = _sub_norm(acc, l, lam, self.sub_ref[:, hs]).astype(BF16)


def _channel_phases(x_ref, r_ref, dn_ref, p_ref, wo_ref, ln2_ref, wfi_ref, wfo_ref, lnp_ref, wpg_ref, wpp_ref,
                    y_ref):
    st = {}

    def attention_out():
        mixed = jnp.concatenate([r_ref[...], dn_ref[...]], axis=-1)
        st["h"] = x_ref[...] + _dot(mixed, wo_ref[...])
        st["hn"] = _rms(st["h"], ln2_ref[...]).astype(BF16)

    def ffn_chunk(c):
        g = _dot(st["hn"], wfi_ref[:, c * FF_CHUNK:(c + 1) * FF_CHUNK])
        u = _dot(st["hn"], wfi_ref[:, D_FF + c * FF_CHUNK:D_FF + (c + 1) * FF_CHUNK])
        act = (g * _sigmoid(g) * u).astype(BF16)
        part = _dot(act, wfo_ref[c * FF_CHUNK:(c + 1) * FF_CHUNK, :])
        st["ff"] = part if c == 0 else st["ff"] + part

    def embedding_gate():
        h = st["h"] + st["ff"]
        gate = _sigmoid(_dot(_rms(h, lnp_ref[...]).astype(BF16), wpg_ref[...]))
        y_ref[...] = h + gate * _dot(p_ref[...].astype(BF16), wpp_ref[...])

    return ([attention_out] + [functools.partial(ffn_chunk, c) for c in range(D_FF // FF_CHUNK)]
            + [embedding_gate])


def _channel_kernel(*refs):
    for phase in _channel_phases(*refs):
        phase()


def _channel_specs(tm, index):
    tok = lambda width: pl.BlockSpec((tm, width), index)
    return [tok(D_MODEL), tok(HALF), tok(HALF), tok(PLE_DIM),
            _const_spec((D_MODEL, D_MODEL)), _const_spec((1, D_MODEL)),
            _const_spec((D_MODEL, 2 * D_FF)), _const_spec((D_FF, D_MODEL)),
            _const_spec((1, D_MODEL)), _const_spec((D_MODEL, D_MODEL)),
            _const_spec((PLE_DIM, D_MODEL))], tok(D_MODEL)


def _channel(x, r, dn, p, chan_w, tm):
    m = x.shape[0]
    in_specs, out_spec = _channel_specs(tm, lambda i: (i, 0))
    return pl.pallas_call(
        _channel_kernel,
        grid=(m // tm,),
        in_specs=in_specs,
        out_specs=out_spec,
        out_shape=jax.ShapeDtypeStruct((m, D_MODEL), F32),
        compiler_params=pltpu.CompilerParams(dimension_semantics=("parallel",),
                                             vmem_limit_bytes=VMEM_LIMIT),
        name="channel",
    )(x, r, dn, p, *chan_w)


def _channel_decode_kernel(pt_ref, x_ref, r_ref, dn_ref, p_ref, wo_ref, ln2_ref, wfi_ref, wfo_ref, lnp_ref, wpg_ref,
                           wpp_ref, q_ref, kn_ref, vn_ref, lq1_ref, lk1_ref, lq2_ref, lk2_ref, sub_ref, ck_hbm,
                           cv_hbm, y_ref, o_ref, kbuf, vbuf, sems, bias_s, m_s, l_s, acc_s, *, n_seq, n_groups):
    i = pl.program_id(0)
    ahead = DECODE_SLOTS - 1
    chan_refs = (x_ref, r_ref, dn_ref, p_ref, wo_ref, ln2_ref, wfi_ref, wfo_ref, lnp_ref, wpg_ref, wpp_ref)
    dec = _PagedDecoder(pt_ref, q_ref, kn_ref, vn_ref, (lq1_ref, lk1_ref, lq2_ref, lk2_ref), sub_ref, ck_hbm,
                        cv_hbm, o_ref, kbuf, vbuf, sems, bias_s, m_s, l_s, acc_s, n_groups)

    @pl.when(i == 0)
    def _():
        dec.init_bias()
        for g in range(ahead):
            dec.start(0, g)

    dec.begin_sequence()
    phases = _channel_phases(*chan_refs, y_ref)
    for g in range(n_groups):
        nxt = g + ahead
        if nxt < n_groups:
            dec.start(i, nxt)
        else:
            pl.when(i + 1 < n_seq)(functools.partial(dec.start, i + 1, nxt - n_groups))
        dec.wait(i, g)
        scores = dec.scores(g)
        for phase in phases[g * len(phases) // n_groups:(g + 1) * len(phases) // n_groups]:
            phase()
        dec.fold(g, scores)
    dec.finish()


def _channel_and_decode(x, r, dn, p, chan_w, page_table, dq, dk_new, dv_new, cache_k, cache_v, lams, sub_w):
    m = x.shape[0]
    db, ds = dq.shape[:2]
    tm = m // db
    assert tm * db == m and tm % 256 == 0, (m, db)
    n_groups = page_table.shape[1] // DECODE_PAGES
    assert n_groups * DECODE_PAGES == page_table.shape[1] and n_groups % DECODE_SLOTS == 0
    chan_in, chan_out = _channel_specs(tm, lambda i, pt: (i, 0))
    tok = pl.BlockSpec((1, ds, HALF), lambda i, pt: (i, 0, 0))
    new = pl.BlockSpec((None, ds, DIFF_HEADS, HEAD_DIM), lambda i, pt: (i, 0, 0, 0))
    page_rows = PAGE * DIFF_HEADS
    cache_k, cache_v = (c.reshape(-1, page_rows, HEAD_DIM) for c in (cache_k, cache_v))
    hbm = pl.BlockSpec(memory_space=pl.ANY)
    ring = pltpu.VMEM((DECODE_SLOTS, DECODE_PAGES, page_rows, HEAD_DIM), F32)
    state = pltpu.VMEM((DIFF_HEADS * 2 * ds, LANES), F32)
    bias = pltpu.VMEM((DIFF_HEADS * 2 * ds, page_rows), F32)
    return pl.pallas_call(
        functools.partial(_channel_decode_kernel, n_seq=db, n_groups=n_groups),
        grid_spec=pltpu.PrefetchScalarGridSpec(
            num_scalar_prefetch=1,
            grid=(db,),
            in_specs=chan_in + [tok, new, new] + [_const_spec((1, QK_DIM))] * 4 + [_const_spec((1, HALF)), hbm, hbm],
            out_specs=[chan_out, tok],
            scratch_shapes=[ring, ring, pltpu.SemaphoreType.DMA((2, DECODE_SLOTS)), bias, state, state, state],
        ),
        out_shape=[jax.ShapeDtypeStruct((m, D_MODEL), F32), jax.ShapeDtypeStruct((db, ds, HALF), BF16)],
        compiler_params=pltpu.CompilerParams(dimension_semantics=("arbitrary",),
                                             vmem_limit_bytes=FUSED_VMEM_LIMIT),
        name="channel_and_decode",
    )(page_table, x, r, dn, p, *chan_w, dq, dk_new, dv_new, *lams, sub_w, cache_k, cache_v)


def kernel(x_prompt, x_sample, cache_k, cache_v, state_ret, page_table, p_prompt, p_sample, ln1_w, w_in, q_norm_w, k_norm_w, lambda_q1, lambda_k1, lambda_q2, lambda_k2, ret_gn_w, ret_gn_b, diff_subln_w, w_o, ln2_w, w_ffn_in, w_ffn_out, ln_ple_w, w_ple_gate, w_ple_proj):
    depth = w_in.shape[0]
    assert depth == 1, "single-layer trunk"
    b, s, _ = x_prompt.shape
    db, ds, _ = x_sample.shape

    row = lambda a: a[0].reshape(1, -1)
    wb = lambda a: a[0].astype(BF16)
    qn_w = jnp.tile(row(q_norm_w), (1, HALF // QK_DIM))
    kn_w = jnp.tile(row(k_norm_w), (1, HALF // QK_DIM))
    lams = (row(lambda_q1), row(lambda_k1), row(lambda_q2), row(lambda_k2))
    proj_w = (row(ln1_w), wb(w_in), qn_w, kn_w)
    chan_w = (wb(w_o), row(ln2_w), wb(w_ffn_in), wb(w_ffn_out), row(ln_ple_w), wb(w_ple_gate), wb(w_ple_proj))
    gn_w, gn_b, sub_w = row(ret_gn_w), row(ret_gn_b), row(diff_subln_w)

    xp = x_prompt.reshape(b * s, D_MODEL)
    dq, dk, dv, dkb, dvt, r, ret_fin = _project_and_retain(xp, *proj_w, gn_w, gn_b, b, s)
    dn = _prompt_diff_attention(dq, dkb, dvt, lams, sub_w, b, s)

    xs = x_sample.reshape(db * ds, D_MODEL)
    rq_s, rk_s, rv_s, rg_s, dq_s, dk_s, dv_s, _, _ = _project(xs, *proj_w, tm=db * ds)
    tok3 = lambda a: a.reshape(db, ds, HALF)
    r_s, ret_new = _sample_retention(tok3(rq_s), tok3(rk_s), tok3(rv_s), tok3(rg_s), state_ret[0], gn_w, gn_b)
    new4 = lambda a: a.reshape(db, ds, DIFF_HEADS, HEAD_DIM)
    y_prompt, dn_s = _channel_and_decode(xp, r, dn, p_prompt[0].reshape(b * s, PLE_DIM), chan_w, page_table,
                                         tok3(dq_s), new4(dk_s), new4(dv_s), cache_k, cache_v, lams, sub_w)
    y_sample = _channel(xs, r_s.reshape(db * ds, HALF), dn_s.reshape(db * ds, HALF),
                        p_sample[0].reshape(db * ds, PLE_DIM), chan_w, tm=db * ds)

    heads = lambda a, n: a.reshape(1, n, -1, DIFF_HEADS, HEAD_DIM)
    return (y_prompt.reshape(b, s, D_MODEL), y_sample.reshape(db, ds, D_MODEL),
            heads(dk, b), heads(dv, b), ret_fin[None],
            heads(dk_s, db), heads(dv_s, db), ret_new[None])
```

```python
import functools
import math

import jax
import jax.numpy as jnp
from jax import lax
from jax.experimental import pallas as pl
from jax.experimental.pallas import tpu as pltpu

F32 = jnp.float32
BF16 = jnp.bfloat16

D_MODEL = 1024
RET_HEADS = 4
HEAD_DIM = 128
DIFF_HEADS = 4
QK_DIM = 64
HALF = RET_HEADS * HEAD_DIM
PROJ_WIDTH = 7 * HALF
D_FF = 2816
PLE_DIM = 256
RET_CHUNK = 128
PAGE = 128
EPS = 1e-6
LAM_INIT = 0.8 - 0.6 * math.exp(-0.3 * 0)
LOG_G = tuple(math.log1p(-(2.0 ** (-5.0 - h))) for h in range(RET_HEADS))
SLOPES = tuple(2.0 ** (-8.0 / DIFF_HEADS * (h + 1)) for h in range(DIFF_HEADS))
LOG2E = math.log2(math.e)
NEG_BIG = -1e30

LANES = 128
VMEM_LIMIT = 56 * 1024 * 1024
FUSED_VMEM_LIMIT = 61 * 1024 * 1024

FF_CHUNK = 256
RET_SEQS_PER_STEP = 4
DIFF_T = 512
DECODE_PAGES = 16
DECODE_CHAINS = 2
DECODE_SLOTS = 2


def _sigmoid(x):
    return 1.0 / (1.0 + jnp.exp(-x))


def _rms(x, w):
    return x * lax.rsqrt(jnp.mean(x * x, axis=-1, keepdims=True) + EPS) * w


def _dot(a, b):
    return jnp.dot(a, b, preferred_element_type=F32)


def _dot_nt(a, b):
    return lax.dot_general(a, b, (((1,), (1,)), ((), ())), preferred_element_type=F32)


def _lambda(lq1, lk1, lq2, lk2):
    a = jnp.exp(jnp.sum(lq1 * lk1, axis=-1, keepdims=True))
    b = jnp.exp(jnp.sum(lq2 * lk2, axis=-1, keepdims=True))
    return a - b + LAM_INIT


def _const_spec(shape):
    nd = len(shape)
    return pl.BlockSpec(shape, lambda *_: (0,) * nd, pipeline_mode=pl.Buffered(1))


def _seg_rms(y, w):
    lo_mask = lax.broadcasted_iota(jnp.int32, (1, LANES), 1) < QK_DIM
    outs = []
    for g in range(HALF // LANES):
        yg = y[:, g * LANES:(g + 1) * LANES]
        t = yg * yg
        lo = jnp.sum(jnp.where(lo_mask, t, 0.0), axis=-1, keepdims=True)
        hi = jnp.sum(jnp.where(lo_mask, 0.0, t), axis=-1, keepdims=True)
        ms = jnp.where(lo_mask, lo, hi) * (1.0 / QK_DIM)
        outs.append(yg * lax.rsqrt(ms + EPS))
    return jnp.concatenate(outs, axis=-1) * w


def _store_cache_format(ref, y):
    for h in range(DIFF_HEADS):
        ref[pl.ds(h, y.shape[0], stride=DIFF_HEADS), :] = y[:, h * HEAD_DIM:(h + 1) * HEAD_DIM]


def _projection(x_ref, ln1_ref, w_ref):
    xn = _rms(x_ref[...], ln1_ref[...]).astype(BF16)
    return lambda i: _dot(xn, w_ref[:, i * HALF:(i + 1) * HALF])


def _retention_operands(col):
    return col(0).astype(BF16), (col(1) * (HEAD_DIM ** -0.5)).astype(BF16), col(2).astype(BF16), col(3)


def _diff_projection_steps(col, qn_ref, kn_ref, dq_ref, dk_ref, dv_ref, dkb_ref, dvt_ref):
    def q_step():
        dq = _seg_rms(col(4), qn_ref[...])
        dq_ref[...] = (dq * (LOG2E * QK_DIM ** -0.5)).astype(BF16)

    def k_step():
        dk = _seg_rms(col(5), kn_ref[...])
        dkb_ref[...] = dk.astype(BF16)
        _store_cache_format(dk_ref, dk)

    def v_step():
        dv = col(6)
        dvt_ref[0] = dv.T.astype(BF16)
        _store_cache_format(dv_ref, dv)

    return [q_step, k_step, v_step]


def _proj_kernel(x_ref, ln1_ref, w_ref, qn_ref, kn_ref,
                 rq_ref, rk_ref, rv_ref, rg_ref, dq_ref, dk_ref, dv_ref, dkb_ref, dvt_ref):
    col = _projection(x_ref, ln1_ref, w_ref)
    for step in _diff_projection_steps(col, qn_ref, kn_ref, dq_ref, dk_ref, dv_ref, dkb_ref, dvt_ref):
        step()
    rq_ref[...], rk_ref[...], rv_ref[...], rg_ref[...] = _retention_operands(col)


def _proj_retention_kernel(x_ref, ln1_ref, w_ref, qn_ref, kn_ref, gnw_ref, gnb_ref,
                           dq_ref, dk_ref, dv_ref, dkb_ref, dvt_ref, r_ref, fin_ref, state, *, tiles_per_seq):
    i = pl.program_id(0)

    @pl.when(i % tiles_per_seq == 0)
    def _():
        state[...] = jnp.zeros_like(state)

    col = _projection(x_ref, ln1_ref, w_ref)
    for step in _diff_projection_steps(col, qn_ref, kn_ref, dq_ref, dk_ref, dv_ref, dkb_ref, dvt_ref):
        step()
    _retention_tile(*_retention_operands(col), gnw_ref, gnb_ref, state, r_ref)

    @pl.when((i + 1) % tiles_per_seq == 0)
    def _():
        fin_ref[0] = state[...]


def _proj_specs(m, tm):
    tok = lambda width: pl.BlockSpec((tm, width), lambda i: (i, 0))
    half = lambda dt: jax.ShapeDtypeStruct((m, HALF), dt)
    cache_fmt = pl.BlockSpec((tm * DIFF_HEADS, HEAD_DIM), lambda i: (i, 0))
    cache_shape = jax.ShapeDtypeStruct((m * DIFF_HEADS, HEAD_DIM), F32)
    in_specs = [tok(D_MODEL), _const_spec((1, D_MODEL)), _const_spec((D_MODEL, PROJ_WIDTH)),
                _const_spec((1, HALF)), _const_spec((1, HALF))]
    diff_specs = [tok(HALF), cache_fmt, cache_fmt, tok(HALF), pl.BlockSpec((1, HALF, tm), lambda i: (i, 0, 0))]
    diff_shapes = [half(BF16), cache_shape, cache_shape, half(BF16),
                   jax.ShapeDtypeStruct((m // tm, HALF, tm), BF16)]
    return tok, half, in_specs, diff_specs, diff_shapes


def _project(x, ln1_w, w_in, qn_w, kn_w, tm):
    m = x.shape[0]
    tok, half, in_specs, diff_specs, diff_shapes = _proj_specs(m, tm)
    return pl.pallas_call(
        _proj_kernel,
        grid=(m // tm,),
        in_specs=in_specs,
        out_specs=[tok(HALF)] * 4 + diff_specs,
        out_shape=[half(BF16), half(BF16), half(BF16), half(F32)] + diff_shapes,
        compiler_params=pltpu.CompilerParams(dimension_semantics=("parallel",),
                                             vmem_limit_bytes=VMEM_LIMIT),
        name="proj",
    )(x, ln1_w, w_in, qn_w, kn_w)


def _project_and_retain(x, ln1_w, w_in, qn_w, kn_w, gn_w, gn_b, batch, seq):
    m, tm = x.shape[0], DIFF_T
    tok, half, in_specs, diff_specs, diff_shapes = _proj_specs(m, tm)
    tiles_per_seq = seq // tm
    state = (RET_HEADS, HEAD_DIM, HEAD_DIM)
    return pl.pallas_call(
        functools.partial(_proj_retention_kernel, tiles_per_seq=tiles_per_seq),
        grid=(m // tm,),
        in_specs=in_specs + [_const_spec((1, HALF)), _const_spec((1, HALF))],
        out_specs=diff_specs + [tok(HALF), pl.BlockSpec((1,) + state, lambda i: (i // tiles_per_seq, 0, 0, 0))],
        out_shape=diff_shapes + [half(BF16), jax.ShapeDtypeStruct((batch,) + state, F32)],
        scratch_shapes=[pltpu.VMEM(state, F32)],
        compiler_params=pltpu.CompilerParams(dimension_semantics=("arbitrary",),
                                             vmem_limit_bytes=VMEM_LIMIT),
        name="proj_retention",
    )(x, ln1_w, w_in, qn_w, kn_w, gn_w, gn_b)


def _group_norm_gate(o, g, gn_w, gn_b):
    mu = jnp.mean(o, axis=-1, keepdims=True)
    d = o - mu
    var = jnp.mean(d * d, axis=-1, keepdims=True)
    r = d * lax.rsqrt(var + EPS) * gn_w + gn_b
    return r * (g * _sigmoid(g))


def _retention_tile(rq, rk, rv, rg, gnw_ref, gnb_ref, state, r_ref):
    n = RET_CHUNK
    row = lax.broadcasted_iota(jnp.int32, (n, n), 0).astype(F32)
    col = lax.broadcasted_iota(jnp.int32, (n, n), 1).astype(F32)
    rel = row - col
    heads = [slice(h * HEAD_DIM, (h + 1) * HEAD_DIM) for h in range(RET_HEADS)]
    decay = [jnp.where(rel >= 0, jnp.exp(lg * jnp.maximum(rel, 0.0)), 0.0) for lg in LOG_G]
    cross_w = [jnp.exp(lg * (row + 1.0)) for lg in LOG_G]
    k_w = [jnp.exp(lg * (n - 1.0 - row)) for lg in LOG_G]
    for c in range(rq.shape[0] // n):
        rows = slice(c * n, (c + 1) * n)
        first = []
        for h, hs in enumerate(heads):
            q, k, v = rq[rows, hs], rk[rows, hs], rv[rows, hs]
            st = state[h]
            first.append((_dot_nt(q, k), _dot(q, st.astype(BF16)), v))
            kw_t = (k.astype(F32) * k_w[h]).T.astype(BF16)
            state[h] = math.exp(LOG_G[h] * n) * st + _dot(kw_t, v)
        for h, hs in enumerate(heads):
            s, cross, v = first[h]
            o = _dot((s * decay[h]).astype(BF16), v) + cross * cross_w[h]
            r = _group_norm_gate(o, rg[rows, hs], gnw_ref[:, hs], gnb_ref[:, hs])
            r_ref[rows, hs] = r.astype(BF16)


def _ret_step_kernel(rq_ref, rk_ref, rv_ref, rg_ref, st_ref, gnw_ref, gnb_ref, r_ref, new_ref):
    n = rq_ref.shape[1]
    row = lax.broadcasted_iota(jnp.int32, (n, LANES), 0).astype(F32)
    for b, h in [(b, h) for b in range(rq_ref.shape[0]) for h in range(RET_HEADS)]:
        hs = slice(h * HEAD_DIM, (h + 1) * HEAD_DIM)
        lg = LOG_G[h]
        q = rq_ref[b, :, hs].astype(F32)
        k = rk_ref[b, :, hs].astype(F32)
        v = rv_ref[b, :, hs].astype(F32)
        st = st_ref[b, h]
        q_pad = jnp.concatenate([q, jnp.zeros((16 - n, HEAD_DIM), F32)], axis=0).astype(BF16)
        o = _dot(q_pad, st.astype(BF16))[:n] * jnp.exp(lg * (row + 1.0))
        new = math.exp(lg * n) * st
        k_t = jnp.concatenate([k, jnp.zeros((8 - n, HEAD_DIM), F32)], axis=0).T
        for m in range(n):
            s_m = jnp.sum(q * k[m:m + 1, :], axis=-1, keepdims=True)
            decay = jnp.where(row >= m, jnp.exp(lg * jnp.maximum(row - m, 0.0)), 0.0)
            o = o + (s_m * decay) * v[m:m + 1, :]
            new = new + math.exp(lg * (n - 1.0 - m)) * (k_t[:, m:m + 1] * v[m:m + 1, :])
        new_ref[b, h] = new
        r = _group_norm_gate(o, rg_ref[b, :, hs], gnw_ref[:, hs], gnb_ref[:, hs])
        r_ref[b, :, hs] = r.astype(BF16)


def _sample_retention(rq, rk, rv, rg, state, gn_w, gn_b):
    db, ds = rq.shape[:2]
    nb = RET_SEQS_PER_STEP
    tok = pl.BlockSpec((nb, ds, HALF), lambda b: (b, 0, 0))
    st = pl.BlockSpec((nb, RET_HEADS, HEAD_DIM, HEAD_DIM), lambda b: (b, 0, 0, 0))
    return pl.pallas_call(
        _ret_step_kernel,
        grid=(db // nb,),
        in_specs=[tok, tok, tok, tok, st, _const_spec((1, HALF)), _const_spec((1, HALF))],
        out_specs=[tok, st],
        out_shape=[jax.ShapeDtypeStruct((db, ds, HALF), BF16),
                   jax.ShapeDtypeStruct(state.shape, F32)],
        compiler_params=pltpu.CompilerParams(dimension_semantics=("parallel",),
                                             vmem_limit_bytes=VMEM_LIMIT),
        name="sample_retention",
    )(rq, rk, rv, rg, state, gn_w, gn_b)


def _split_components(q):
    lo_mask = lax.broadcasted_iota(jnp.int32, q.shape, 1) < QK_DIM
    zero = jnp.zeros_like(q)
    return jnp.concatenate([jnp.where(lo_mask, q, zero), jnp.where(lo_mask, zero, q)], axis=0)


def _sub_norm(acc, l, lam, w):
    t = acc.shape[0] // 2
    o = acc / l
    d = o[:t] - lam * o[t:]
    return d * lax.rsqrt(jnp.mean(d * d, axis=-1, keepdims=True) + EPS) * w * (1.0 - LAM_INIT)


def _diff_kernel(q_ref, k_ref, vt_ref, lq1_ref, lk1_ref, lq2_ref, lk2_ref, sub_ref, o_ref):
    i = pl.program_id(1)
    t = DIFF_T
    lam = _lambda(lq1_ref[...], lk1_ref[...], lq2_ref[...], lk2_ref[...])
    k_idx = lax.broadcasted_iota(jnp.int32, (t, LANES), 0).astype(F32)
    key = lax.broadcasted_iota(jnp.int32, (t, 2 * t), 0)
    col = lax.broadcasted_iota(jnp.int32, (t, 2 * t), 1)
    causal = key <= jnp.where(col >= t, col - t, col)
    heads = [slice(h * HEAD_DIM, (h + 1) * HEAD_DIM) for h in range(DIFF_HEADS)]
    qqs = [_split_components(q_ref[:, hs]) for hs in heads]

    def block(j, carry, masked):
        start = pl.multiple_of(j * t, t)
        k_pos = k_idx + (j * t).astype(F32)
        scores = [_dot_nt(k_ref[pl.ds(start, t), hs], qqs[h]) for h, hs in enumerate(heads)]

        def softmax(h):
            m, l, _ = carry[h]
            bias = (SLOPES[h] * LOG2E) * k_pos
            s = scores[h] + jnp.concatenate([bias] * (2 * t // LANES), axis=1)
            if masked:
                s = jnp.where(causal, s, -jnp.inf)
            m_new = jnp.maximum(m, jnp.max(s, axis=0, keepdims=True))
            alpha = jnp.exp2(m - m_new)
            p = jnp.exp2(s - m_new)
            return m_new, alpha * l + jnp.sum(p, axis=0, keepdims=True), alpha, p.astype(BF16)

        def values(h, soft):
            m_new, l, alpha, p = soft
            return m_new, l, alpha * carry[h][2] + _dot(vt_ref[j, heads[h], :], p)

        soft = [softmax(h) for h in range(DIFF_HEADS)]
        return tuple(values(h, soft[h]) for h in range(DIFF_HEADS))

    init = tuple((jnp.full((1, 2 * t), NEG_BIG, F32), jnp.zeros((1, 2 * t), F32),
                  jnp.zeros((HEAD_DIM, 2 * t), F32)) for _ in heads)
    carry = lax.fori_loop(0, i, functools.partial(block, masked=False), init)
    carry = block(i, carry, masked=True)
    for (_, l, acc), hs in zip(carry, heads):
        o = acc / l
        d = o[:, :t] - lam * o[:, t:]
        dn = d * lax.rsqrt(jnp.mean(d * d, axis=0, keepdims=True) + EPS)
        o_ref[:, hs] = (dn.T * sub_ref[:, hs] * (1.0 - LAM_INIT)).astype(BF16)


def _prompt_diff_attention(dq, dk, dvt, lams, sub_w, batch, seq):
    nq = seq // DIFF_T
    qspec = pl.BlockSpec((DIFF_T, HALF), lambda b, i: (b * nq + i, 0))
    kspec = pl.BlockSpec((seq, HALF), lambda b, i: (b, 0))
    vspec = pl.BlockSpec((nq, HALF, DIFF_T), lambda b, i: (b, 0, 0))
    lspec = _const_spec((1, QK_DIM))
    return pl.pallas_call(
        _diff_kernel,
        grid=(batch, nq),
        in_specs=[qspec, kspec, vspec, lspec, lspec, lspec, lspec, _const_spec((1, HALF))],
        out_specs=qspec,
        out_shape=jax.ShapeDtypeStruct((batch * seq, HALF), BF16),
        compiler_params=pltpu.CompilerParams(dimension_semantics=("parallel", "arbitrary"),
                                             vmem_limit_bytes=VMEM_LIMIT),
        name="prompt_diff_attention",
    )(dq, dk, dvt, *lams, sub_w)


class _PagedDecoder:
    def __init__(self, pt_ref, q_ref, kn_ref, vn_ref, lam_refs, sub_ref, ck_hbm, cv_hbm, o_ref,
                 kbuf, vbuf, sems, bias_s, m_s, l_s, acc_s, n_groups):
        self.pt_ref, self.q_ref, self.kn_ref, self.vn_ref = pt_ref, q_ref, kn_ref, vn_ref
        self.lam_refs, self.sub_ref, self.ck_hbm, self.cv_hbm, self.o_ref = lam_refs, sub_ref, ck_hbm, cv_hbm, o_ref
        self.kbuf, self.vbuf, self.sems = kbuf, vbuf, sems
        self.bias_s, self.m_s, self.l_s, self.acc_s = bias_s, m_s, l_s, acc_s
        self.n = q_ref.shape[1]
        self.rows = DIFF_HEADS * 2 * self.n
        self.page_rows = PAGE * DIFF_HEADS
        self.past = n_groups * DECODE_PAGES * PAGE
        r1 = lax.broadcasted_iota(jnp.int32, (self.rows, 1), 0)
        self.head = r1 // (2 * self.n)
        self.qi = r1 % self.n
        self.slope = jnp.zeros((self.rows, 1), F32)
        for h in range(DIFF_HEADS):
            self.slope = jnp.where(self.head == h, SLOPES[h] * LOG2E, self.slope)

    def _copies(self, seq, group):
        slot = group % DECODE_SLOTS
        copies = []
        for p in range(DECODE_PAGES):
            page = self.pt_ref[seq, group * DECODE_PAGES + p]
            copies.append(pltpu.make_async_copy(self.ck_hbm.at[page], self.kbuf.at[slot, p], self.sems.at[0, slot]))
            copies.append(pltpu.make_async_copy(self.cv_hbm.at[page], self.vbuf.at[slot, p], self.sems.at[1, slot]))
        return copies

    def start(self, seq, group):
        for c in self._copies(seq, group):
            c.start()

    def wait(self, seq, group):
        for c in self._copies(seq, group):
            c.wait()

    def init_bias(self):
        col = lax.broadcasted_iota(jnp.int32, self.bias_s.shape, 1)
        dist = (self.past + self.qi - col // DIFF_HEADS).astype(F32)
        self.bias_s[...] = jnp.where(col % DIFF_HEADS == self.head, -self.slope * dist, -jnp.inf)

    def begin_sequence(self):
        self.m_s[...] = jnp.full_like(self.m_s, NEG_BIG)
        self.l_s[...] = jnp.zeros_like(self.l_s)
        self.acc_s[...] = jnp.zeros_like(self.acc_s)
        q = self.q_ref[0]
        self.qq = jnp.concatenate([_split_components(q[:, h * HEAD_DIM:(h + 1) * HEAD_DIM].astype(F32))
                                   for h in range(DIFF_HEADS)], axis=0).astype(BF16)

    def scores(self, g):
        slot, per_chain = g % DECODE_SLOTS, DECODE_PAGES // DECODE_CHAINS
        return [jnp.concatenate([_dot_nt(self.qq, self.kbuf[slot, p].astype(BF16))
                                 for p in range(c * per_chain, (c + 1) * per_chain)], axis=-1)
                for c in range(DECODE_CHAINS)]

    def fold(self, g, scores):
        slot, page_rows = g % DECODE_SLOTS, self.page_rows
        per_chain = DECODE_PAGES // DECODE_CHAINS
        chains = []
        for c, s in enumerate(scores):
            pages = range(c * per_chain, (c + 1) * per_chain)
            s = s + jnp.concatenate([self.bias_s[...] + self.slope * float((g * DECODE_PAGES + p) * PAGE)
                                     for p in pages], axis=-1)
            m_c = jnp.max(s, axis=-1, keepdims=True)
            pb = jnp.exp2(s - m_c)
            l_c = jnp.sum(pb, axis=-1, keepdims=True)
            pb = pb.astype(BF16)
            pv_c = _dot(pb[:, :page_rows], self.vbuf[slot, pages[0]].astype(BF16))
            for idx, pg in enumerate(pages[1:], start=1):
                pv_c = pv_c + _dot(pb[:, idx * page_rows:(idx + 1) * page_rows], self.vbuf[slot, pg].astype(BF16))
            chains.append((m_c, l_c, pv_c))
        m = self.m_s[:, :1]
        m_new = m
        for m_c, _, _ in chains:
            m_new = jnp.maximum(m_new, m_c)
        alpha = jnp.exp2(m - m_new)
        l_new = alpha * self.l_s[:, :1]
        acc = alpha * self.acc_s[...]
        for m_c, l_c, pv_c in chains:
            w = jnp.exp2(m_c - m_new)
            l_new = l_new + w * l_c
            acc = acc + w * pv_c
        self.m_s[...] = jnp.broadcast_to(m_new, (self.rows, LANES))
        self.l_s[...] = jnp.broadcast_to(l_new, (self.rows, LANES))
        self.acc_s[...] = acc

    def finish(self):
        n = self.n
        lam = _lambda(*(ref[...] for ref in self.lam_refs))
        r8 = lax.broadcasted_iota(jnp.int32, (2 * n, 1), 0)
        qi = jnp.where(r8 >= n, r8 - n, r8)
        for h in range(DIFF_HEADS):
            hs = slice(h * HEAD_DIM, (h + 1) * HEAD_DIM)
            hr = slice(h * 2 * n, (h + 1) * 2 * n)
            qq = _split_components(self.q_ref[0, :, hs].astype(F32))
            kn = self.kn_ref[:, h, :]
            vn = self.vn_ref[:, h, :]
            m, l, acc = self.m_s[hr, :1], self.l_s[hr, :1], self.acc_s[hr, :]
            s_cols = []
            for t in range(n):
                s_t = jnp.sum(qq * kn[t:t + 1, :], axis=-1, keepdims=True)
                s_t = s_t - SLOPES[h] * LOG2E * (qi - t).astype(F32)
                s_cols.append(jnp.where(qi >= t, s_t, -jnp.inf))
            m_new = m
            for s_t in s_cols:
                m_new = jnp.maximum(m_new, s_t)
            alpha = jnp.exp2(m - m_new)
            l = alpha * l
            acc = alpha * acc
            for t in range(n):
                p_t = jnp.exp2(s_cols[t] - m_new)
                l = l + p_t
                acc = acc + p_t * vn[t:t + 1, :]
            self.o_ref[0, :, hs] = _sub_norm(acc, l, lam, self.sub_ref[:, hs]).astype(BF16)


def _channel_phases(x_ref, r_ref, dn_ref, p_ref, wo_ref, ln2_ref, wfi_ref, wfo_ref, lnp_ref, wpg_ref, wpp_ref,
                    y_ref):
    st = {}

    def attention_out():
        mixed = jnp.concatenate([r_ref[...], dn_ref[...]], axis=-1)
        st["h"] = x_ref[...] + _dot(mixed, wo_ref[...])
        st["hn"] = _rms(st["h"], ln2_ref[...]).astype(BF16)

    def ffn_chunk(c):
        g = _dot(st["hn"], wfi_ref[:, c * FF_CHUNK:(c + 1) * FF_CHUNK])
        u = _dot(st["hn"], wfi_ref[:, D_FF + c * FF_CHUNK:D_FF + (c + 1) * FF_CHUNK])
        act = (g * _sigmoid(g) * u).astype(BF16)
        part = _dot(act, wfo_ref[c * FF_CHUNK:(c + 1) * FF_CHUNK, :])
        st["ff"] = part if c == 0 else st["ff"] + part

    def embedding_gate():
        h = st["h"] + st["ff"]
        gate = _sigmoid(_dot(_rms(h, lnp_ref[...]).astype(BF16), wpg_ref[...]))
        y_ref[...] = h + gate * _dot(p_ref[...].astype(BF16), wpp_ref[...])

    return ([attention_out] + [functools.partial(ffn_chunk, c) for c in range(D_FF // FF_CHUNK)]
            + [embedding_gate])


def _channel_kernel(*refs):
    for phase in _channel_phases(*refs):
        phase()


def _channel_specs(tm, index):
    tok = lambda width: pl.BlockSpec((tm, width), index)
    return [tok(D_MODEL), tok(HALF), tok(HALF), tok(PLE_DIM),
            _const_spec((D_MODEL, D_MODEL)), _const_spec((1, D_MODEL)),
            _const_spec((D_MODEL, 2 * D_FF)), _const_spec((D_FF, D_MODEL)),
            _const_spec((1, D_MODEL)), _const_spec((D_MODEL, D_MODEL)),
            _const_spec((PLE_DIM, D_MODEL))], tok(D_MODEL)


def _channel(x, r, dn, p, chan_w, tm):
    m = x.shape[0]
    in_specs, out_spec = _channel_specs(tm, lambda i: (i, 0))
    return pl.pallas_call(
        _channel_kernel,
        grid=(m // tm,),
        in_specs=in_specs,
        out_specs=out_spec,
        out_shape=jax.ShapeDtypeStruct((m, D_MODEL), F32),
        compiler_params=pltpu.CompilerParams(dimension_semantics=("parallel",),
                                             vmem_limit_bytes=VMEM_LIMIT),
        name="channel",
    )(x, r, dn, p, *chan_w)


def _channel_decode_kernel(pt_ref, x_ref, r_ref, dn_ref, p_ref, wo_ref, ln2_ref, wfi_ref, wfo_ref, lnp_ref, wpg_ref,
                           wpp_ref, q_ref, kn_ref, vn_ref, lq1_ref, lk1_ref, lq2_ref, lk2_ref, sub_ref, ck_hbm,
                           cv_hbm, y_ref, o_ref, kbuf, vbuf, sems, bias_s, m_s, l_s, acc_s, *, n_seq, n_groups):
    i = pl.program_id(0)
    ahead = DECODE_SLOTS - 1
    chan_refs = (x_ref, r_ref, dn_ref, p_ref, wo_ref, ln2_ref, wfi_ref, wfo_ref, lnp_ref, wpg_ref, wpp_ref)
    dec = _PagedDecoder(pt_ref, q_ref, kn_ref, vn_ref, (lq1_ref, lk1_ref, lq2_ref, lk2_ref), sub_ref, ck_hbm,
                        cv_hbm, o_ref, kbuf, vbuf, sems, bias_s, m_s, l_s, acc_s, n_groups)

    @pl.when(i == 0)
    def _():
        dec.init_bias()
        for g in range(ahead):
            dec.start(0, g)

    dec.begin_sequence()
    phases = _channel_phases(*chan_refs, y_ref)
    per_group = -(-len(phases) // n_groups)
    for g in range(n_groups):
        nxt = g + ahead
        if nxt < n_groups:
            dec.start(i, nxt)
        else:
            pl.when(i + 1 < n_seq)(functools.partial(dec.start, i + 1, nxt - n_groups))
        dec.wait(i, g)
        scores = dec.scores(g)
        for phase in phases[g * per_group:(g + 1) * per_group]:
            phase()
        dec.fold(g, scores)
    dec.finish()


def _channel_and_decode(x, r, dn, p, chan_w, page_table, dq, dk_new, dv_new, cache_k, cache_v, lams, sub_w):
    m = x.shape[0]
    db, ds = dq.shape[:2]
    tm = m // db
    assert tm * db == m and tm % 256 == 0, (m, db)
    n_groups = page_table.shape[1] // DECODE_PAGES
    assert n_groups * DECODE_PAGES == page_table.shape[1] and n_groups % DECODE_SLOTS == 0
    chan_in, chan_out = _channel_specs(tm, lambda i, pt: (i, 0))
    tok = pl.BlockSpec((1, ds, HALF), lambda i, pt: (i, 0, 0))
    new = pl.BlockSpec((None, ds, DIFF_HEADS, HEAD_DIM), lambda i, pt: (i, 0, 0, 0))
    page_rows = PAGE * DIFF_HEADS
    cache_k, cache_v = (c.reshape(-1, page_rows, HEAD_DIM) for c in (cache_k, cache_v))
    hbm = pl.BlockSpec(memory_space=pl.ANY)
    ring = pltpu.VMEM((DECODE_SLOTS, DECODE_PAGES, page_rows, HEAD_DIM), F32)
    state = pltpu.VMEM((DIFF_HEADS * 2 * ds, LANES), F32)
    bias = pltpu.VMEM((DIFF_HEADS * 2 * ds, page_rows), F32)
    return pl.pallas_call(
        functools.partial(_channel_decode_kernel, n_seq=db, n_groups=n_groups),
        grid_spec=pltpu.PrefetchScalarGridSpec(
            num_scalar_prefetch=1,
            grid=(db,),
            in_specs=chan_in + [tok, new, new] + [_const_spec((1, QK_DIM))] * 4 + [_const_spec((1, HALF)), hbm, hbm],
            out_specs=[chan_out, tok],
            scratch_shapes=[ring, ring, pltpu.SemaphoreType.DMA((2, DECODE_SLOTS)), bias, state, state, state],
        ),
        out_shape=[jax.ShapeDtypeStruct((m, D_MODEL), F32), jax.ShapeDtypeStruct((db, ds, HALF), BF16)],
        compiler_params=pltpu.CompilerParams(dimension_semantics=("arbitrary",),
                                             vmem_limit_bytes=FUSED_VMEM_LIMIT),
        name="channel_and_decode",
    )(page_table, x, r, dn, p, *chan_w, dq, dk_new, dv_new, *lams, sub_w, cache_k, cache_v)


def kernel(x_prompt, x_sample, cache_k, cache_v, state_ret, page_table, p_prompt, p_sample, ln1_w, w_in, q_norm_w, k_norm_w, lambda_q1, lambda_k1, lambda_q2, lambda_k2, ret_gn_w, ret_gn_b, diff_subln_w, w_o, ln2_w, w_ffn_in, w_ffn_out, ln_ple_w, w_ple_gate, w_ple_proj):
    depth = w_in.shape[0]
    assert depth == 1, "single-layer trunk"
    b, s, _ = x_prompt.shape
    db, ds, _ = x_sample.shape

    row = lambda a: a[0].reshape(1, -1)
    wb = lambda a: a[0].astype(BF16)
    qn_w = jnp.tile(row(q_norm_w), (1, HALF // QK_DIM))
    kn_w = jnp.tile(row(k_norm_w), (1, HALF // QK_DIM))
    lams = (row(lambda_q1), row(lambda_k1), row(lambda_q2), row(lambda_k2))
    proj_w = (row(ln1_w), wb(w_in), qn_w, kn_w)
    chan_w = (wb(w_o), row(ln2_w), wb(w_ffn_in), wb(w_ffn_out), row(ln_ple_w), wb(w_ple_gate), wb(w_ple_proj))
    gn_w, gn_b, sub_w = row(ret_gn_w), row(ret_gn_b), row(diff_subln_w)

    xp = x_prompt.reshape(b * s, D_MODEL)
    dq, dk, dv, dkb, dvt, r, ret_fin = _project_and_retain(xp, *proj_w, gn_w, gn_b, b, s)
    dn = _prompt_diff_attention(dq, dkb, dvt, lams, sub_w, b, s)

    xs = x_sample.reshape(db * ds, D_MODEL)
    rq_s, rk_s, rv_s, rg_s, dq_s, dk_s, dv_s, _, _ = _project(xs, *proj_w, tm=db * ds)
    tok3 = lambda a: a.reshape(db, ds, HALF)
    r_s, ret_new = _sample_retention(tok3(rq_s), tok3(rk_s), tok3(rv_s), tok3(rg_s), state_ret[0], gn_w, gn_b)
    new4 = lambda a: a.reshape(db, ds, DIFF_HEADS, HEAD_DIM)
    y_prompt, dn_s = _channel_and_decode(xp, r, dn, p_prompt[0].reshape(b * s, PLE_DIM), chan_w, page_table,
                                         tok3(dq_s), new4(dk_s), new4(dv_s), cache_k, cache_v, lams, sub_w)
    y_sample = _channel(xs, r_s.reshape(db * ds, HALF), dn_s.reshape(db * ds, HALF),
                        p_sample[0].reshape(db * ds, PLE_DIM), chan_w, tm=db * ds)

    heads = lambda a, n: a.reshape(1, n, -1, DIFF_HEADS, HEAD_DIM)
    return (y_prompt.reshape(b, s, D_MODEL), y_sample.reshape(db, ds, D_MODEL),
            heads(dk, b), heads(dv, b), ret_fin[None],
            heads(dk_s, db), heads(dv_s, db), ret_new[None])
```

```python
import functools
import math

import jax
import jax.numpy as jnp
from jax import lax
from jax.experimental import pallas as pl
from jax.experimental.pallas import tpu as pltpu

F32 = jnp.float32
BF16 = jnp.bfloat16

D_MODEL = 1024
RET_HEADS = 4
HEAD_DIM = 128
DIFF_HEADS = 4
QK_DIM = 64
HALF = RET_HEADS * HEAD_DIM
PROJ_WIDTH = 7 * HALF
D_FF = 2816
PLE_DIM = 256
RET_CHUNK = 128
PAGE = 128
EPS = 1e-6
LAM_INIT = 0.8 - 0.6 * math.exp(-0.3 * 0)
LOG_G = tuple(math.log1p(-(2.0 ** (-5.0 - h))) for h in range(RET_HEADS))
SLOPES = tuple(2.0 ** (-8.0 / DIFF_HEADS * (h + 1)) for h in range(DIFF_HEADS))
LOG2E = math.log2(math.e)
NEG_BIG = -1e30

LANES = 128
VMEM_LIMIT = 56 * 1024 * 1024
FUSED_VMEM_LIMIT = 61 * 1024 * 1024

FF_CHUNK = 256
RET_SEQS_PER_STEP = 4
DIFF_T = 512
DECODE_PAGES = 16
DECODE_CHAINS = 2
CHANNEL_PHASES_PER_GROUP = 3
DECODE_SLOTS = 2


def _sigmoid(x):
    return 1.0 / (1.0 + jnp.exp(-x))


def _rms(x, w):
    return x * lax.rsqrt(jnp.mean(x * x, axis=-1, keepdims=True) + EPS) * w


def _dot(a, b):
    return jnp.dot(a, b, preferred_element_type=F32)


def _dot_nt(a, b):
    return lax.dot_general(a, b, (((1,), (1,)), ((), ())), preferred_element_type=F32)


def _lambda(lq1, lk1, lq2, lk2):
    a = jnp.exp(jnp.sum(lq1 * lk1, axis=-1, keepdims=True))
    b = jnp.exp(jnp.sum(lq2 * lk2, axis=-1, keepdims=True))
    return a - b + LAM_INIT


def _const_spec(shape):
    nd = len(shape)
    return pl.BlockSpec(shape, lambda *_: (0,) * nd, pipeline_mode=pl.Buffered(1))


def _seg_rms(y, w):
    lo_mask = lax.broadcasted_iota(jnp.int32, (1, LANES), 1) < QK_DIM
    outs = []
    for g in range(HALF // LANES):
        yg = y[:, g * LANES:(g + 1) * LANES]
        t = yg * yg
        lo = jnp.sum(jnp.where(lo_mask, t, 0.0), axis=-1, keepdims=True)
        hi = jnp.sum(jnp.where(lo_mask, 0.0, t), axis=-1, keepdims=True)
        ms = jnp.where(lo_mask, lo, hi) * (1.0 / QK_DIM)
        outs.append(yg * lax.rsqrt(ms + EPS))
    return jnp.concatenate(outs, axis=-1) * w


def _store_cache_format(ref, y):
    for h in range(DIFF_HEADS):
        ref[pl.ds(h, y.shape[0], stride=DIFF_HEADS), :] = y[:, h * HEAD_DIM:(h + 1) * HEAD_DIM]


def _projection(x_ref, ln1_ref, w_ref):
    xn = _rms(x_ref[...], ln1_ref[...]).astype(BF16)
    return lambda i: _dot(xn, w_ref[:, i * HALF:(i + 1) * HALF])


def _retention_operands(col):
    return col(0).astype(BF16), (col(1) * (HEAD_DIM ** -0.5)).astype(BF16), col(2).astype(BF16), col(3)


def _diff_projection_steps(col, qn_ref, kn_ref, dq_ref, dk_ref, dv_ref, dkb_ref, dvt_ref):
    def q_step():
        dq = _seg_rms(col(4), qn_ref[...])
        dq_ref[...] = (dq * (LOG2E * QK_DIM ** -0.5)).astype(BF16)

    def k_step():
        dk = _seg_rms(col(5), kn_ref[...])
        dkb_ref[...] = dk.astype(BF16)
        _store_cache_format(dk_ref, dk)

    def v_step():
        dv = col(6)
        dvt_ref[0] = dv.T.astype(BF16)
        _store_cache_format(dv_ref, dv)

    return [q_step, k_step, v_step]


def _proj_kernel(x_ref, ln1_ref, w_ref, qn_ref, kn_ref,
                 rq_ref, rk_ref, rv_ref, rg_ref, dq_ref, dk_ref, dv_ref, dkb_ref, dvt_ref):
    col = _projection(x_ref, ln1_ref, w_ref)
    for step in _diff_projection_steps(col, qn_ref, kn_ref, dq_ref, dk_ref, dv_ref, dkb_ref, dvt_ref):
        step()
    rq_ref[...], rk_ref[...], rv_ref[...], rg_ref[...] = _retention_operands(col)


def _proj_retention_kernel(x_ref, ln1_ref, w_ref, qn_ref, kn_ref, gnw_ref, gnb_ref,
                           dq_ref, dk_ref, dv_ref, dkb_ref, dvt_ref, r_ref, fin_ref, state, *, tiles_per_seq):
    i = pl.program_id(0)

    @pl.when(i % tiles_per_seq == 0)
    def _():
        state[...] = jnp.zeros_like(state)

    col = _projection(x_ref, ln1_ref, w_ref)
    for step in _diff_projection_steps(col, qn_ref, kn_ref, dq_ref, dk_ref, dv_ref, dkb_ref, dvt_ref):
        step()
    _retention_tile(*_retention_operands(col), gnw_ref, gnb_ref, state, r_ref)

    @pl.when((i + 1) % tiles_per_seq == 0)
    def _():
        fin_ref[0] = state[...]


def _proj_specs(m, tm):
    tok = lambda width: pl.BlockSpec((tm, width), lambda i: (i, 0))
    half = lambda dt: jax.ShapeDtypeStruct((m, HALF), dt)
    cache_fmt = pl.BlockSpec((tm * DIFF_HEADS, HEAD_DIM), lambda i: (i, 0))
    cache_shape = jax.ShapeDtypeStruct((m * DIFF_HEADS, HEAD_DIM), F32)
    in_specs = [tok(D_MODEL), _const_spec((1, D_MODEL)), _const_spec((D_MODEL, PROJ_WIDTH)),
                _const_spec((1, HALF)), _const_spec((1, HALF))]
    diff_specs = [tok(HALF), cache_fmt, cache_fmt, tok(HALF), pl.BlockSpec((1, HALF, tm), lambda i: (i, 0, 0))]
    diff_shapes = [half(BF16), cache_shape, cache_shape, half(BF16),
                   jax.ShapeDtypeStruct((m // tm, HALF, tm), BF16)]
    return tok, half, in_specs, diff_specs, diff_shapes


def _project(x, ln1_w, w_in, qn_w, kn_w, tm):
    m = x.shape[0]
    tok, half, in_specs, diff_specs, diff_shapes = _proj_specs(m, tm)
    return pl.pallas_call(
        _proj_kernel,
        grid=(m // tm,),
        in_specs=in_specs,
        out_specs=[tok(HALF)] * 4 + diff_specs,
        out_shape=[half(BF16), half(BF16), half(BF16), half(F32)] + diff_shapes,
        compiler_params=pltpu.CompilerParams(dimension_semantics=("parallel",),
                                             vmem_limit_bytes=VMEM_LIMIT),
        name="proj",
    )(x, ln1_w, w_in, qn_w, kn_w)


def _project_and_retain(x, ln1_w, w_in, qn_w, kn_w, gn_w, gn_b, batch, seq):
    m, tm = x.shape[0], DIFF_T
    tok, half, in_specs, diff_specs, diff_shapes = _proj_specs(m, tm)
    tiles_per_seq = seq // tm
    state = (RET_HEADS, HEAD_DIM, HEAD_DIM)
    return pl.pallas_call(
        functools.partial(_proj_retention_kernel, tiles_per_seq=tiles_per_seq),
        grid=(m // tm,),
        in_specs=in_specs + [_const_spec((1, HALF)), _const_spec((1, HALF))],
        out_specs=diff_specs + [tok(HALF), pl.BlockSpec((1,) + state, lambda i: (i // tiles_per_seq, 0, 0, 0))],
        out_shape=diff_shapes + [half(BF16), jax.ShapeDtypeStruct((batch,) + state, F32)],
        scratch_shapes=[pltpu.VMEM(state, F32)],
        compiler_params=pltpu.CompilerParams(dimension_semantics=("arbitrary",),
                                             vmem_limit_bytes=VMEM_LIMIT),
        name="proj_retention",
    )(x, ln1_w, w_in, qn_w, kn_w, gn_w, gn_b)


def _group_norm_gate(o, g, gn_w, gn_b):
    mu = jnp.mean(o, axis=-1, keepdims=True)
    d = o - mu
    var = jnp.mean(d * d, axis=-1, keepdims=True)
    r = d * lax.rsqrt(var + EPS) * gn_w + gn_b
    return r * (g * _sigmoid(g))


def _retention_tile(rq, rk, rv, rg, gnw_ref, gnb_ref, state, r_ref):
    n = RET_CHUNK
    row = lax.broadcasted_iota(jnp.int32, (n, n), 0).astype(F32)
    col = lax.broadcasted_iota(jnp.int32, (n, n), 1).astype(F32)
    rel = row - col
    heads = [slice(h * HEAD_DIM, (h + 1) * HEAD_DIM) for h in range(RET_HEADS)]
    decay = [jnp.where(rel >= 0, jnp.exp(lg * jnp.maximum(rel, 0.0)), 0.0) for lg in LOG_G]
    cross_w = [jnp.exp(lg * (row + 1.0)) for lg in LOG_G]
    k_w = [jnp.exp(lg * (n - 1.0 - row)) for lg in LOG_G]
    for c in range(rq.shape[0] // n):
        rows = slice(c * n, (c + 1) * n)
        first = []
        for h, hs in enumerate(heads):
            q, k, v = rq[rows, hs], rk[rows, hs], rv[rows, hs]
            st = state[h]
            first.append((_dot_nt(q, k), _dot(q, st.astype(BF16)), v))
            kw_t = (k.astype(F32) * k_w[h]).T.astype(BF16)
            state[h] = math.exp(LOG_G[h] * n) * st + _dot(kw_t, v)
        for h, hs in enumerate(heads):
            s, cross, v = first[h]
            o = _dot((s * decay[h]).astype(BF16), v) + cross * cross_w[h]
            r = _group_norm_gate(o, rg[rows, hs], gnw_ref[:, hs], gnb_ref[:, hs])
            r_ref[rows, hs] = r.astype(BF16)


def _ret_step_kernel(rq_ref, rk_ref, rv_ref, rg_ref, st_ref, gnw_ref, gnb_ref, r_ref, new_ref):
    n = rq_ref.shape[1]
    row = lax.broadcasted_iota(jnp.int32, (n, LANES), 0).astype(F32)
    for b, h in [(b, h) for b in range(rq_ref.shape[0]) for h in range(RET_HEADS)]:
        hs = slice(h * HEAD_DIM, (h + 1) * HEAD_DIM)
        lg = LOG_G[h]
        q = rq_ref[b, :, hs].astype(F32)
        k = rk_ref[b, :, hs].astype(F32)
        v = rv_ref[b, :, hs].astype(F32)
        st = st_ref[b, h]
        q_pad = jnp.concatenate([q, jnp.zeros((16 - n, HEAD_DIM), F32)], axis=0).astype(BF16)
        o = _dot(q_pad, st.astype(BF16))[:n] * jnp.exp(lg * (row + 1.0))
        new = math.exp(lg * n) * st
        k_t = jnp.concatenate([k, jnp.zeros((8 - n, HEAD_DIM), F32)], axis=0).T
        for m in range(n):
            s_m = jnp.sum(q * k[m:m + 1, :], axis=-1, keepdims=True)
            decay = jnp.where(row >= m, jnp.exp(lg * jnp.maximum(row - m, 0.0)), 0.0)
            o = o + (s_m * decay) * v[m:m + 1, :]
            new = new + math.exp(lg * (n - 1.0 - m)) * (k_t[:, m:m + 1] * v[m:m + 1, :])
        new_ref[b, h] = new
        r = _group_norm_gate(o, rg_ref[b, :, hs], gnw_ref[:, hs], gnb_ref[:, hs])
        r_ref[b, :, hs] = r.astype(BF16)


def _sample_retention(rq, rk, rv, rg, state, gn_w, gn_b):
    db, ds = rq.shape[:2]
    nb = RET_SEQS_PER_STEP
    tok = pl.BlockSpec((nb, ds, HALF), lambda b: (b, 0, 0))
    st = pl.BlockSpec((nb, RET_HEADS, HEAD_DIM, HEAD_DIM), lambda b: (b, 0, 0, 0))
    return pl.pallas_call(
        _ret_step_kernel,
        grid=(db // nb,),
        in_specs=[tok, tok, tok, tok, st, _const_spec((1, HALF)), _const_spec((1, HALF))],
        out_specs=[tok, st],
        out_shape=[jax.ShapeDtypeStruct((db, ds, HALF), BF16),
                   jax.ShapeDtypeStruct(state.shape, F32)],
        compiler_params=pltpu.CompilerParams(dimension_semantics=("parallel",),
                                             vmem_limit_bytes=VMEM_LIMIT),
        name="sample_retention",
    )(rq, rk, rv, rg, state, gn_w, gn_b)


def _split_components(q):
    lo_mask = lax.broadcasted_iota(jnp.int32, q.shape, 1) < QK_DIM
    zero = jnp.zeros_like(q)
    return jnp.concatenate([jnp.where(lo_mask, q, zero), jnp.where(lo_mask, zero, q)], axis=0)


def _sub_norm(acc, l, lam, w):
    t = acc.shape[0] // 2
    o = acc / l
    d = o[:t] - lam * o[t:]
    return d * lax.rsqrt(jnp.mean(d * d, axis=-1, keepdims=True) + EPS) * w * (1.0 - LAM_INIT)


def _diff_kernel(q_ref, k_ref, vt_ref, lq1_ref, lk1_ref, lq2_ref, lk2_ref, sub_ref, o_ref):
    i = pl.program_id(1)
    t = DIFF_T
    lam = _lambda(lq1_ref[...], lk1_ref[...], lq2_ref[...], lk2_ref[...])
    k_idx = lax.broadcasted_iota(jnp.int32, (t, LANES), 0).astype(F32)
    key = lax.broadcasted_iota(jnp.int32, (t, 2 * t), 0)
    col = lax.broadcasted_iota(jnp.int32, (t, 2 * t), 1)
    causal = key <= jnp.where(col >= t, col - t, col)
    heads = [slice(h * HEAD_DIM, (h + 1) * HEAD_DIM) for h in range(DIFF_HEADS)]
    qqs = [_split_components(q_ref[:, hs]) for hs in heads]

    def block(j, carry, masked):
        start = pl.multiple_of(j * t, t)
        k_pos = k_idx + (j * t).astype(F32)
        scores = [_dot_nt(k_ref[pl.ds(start, t), hs], qqs[h]) for h, hs in enumerate(heads)]

        def softmax(h):
            m, l, _ = carry[h]
            bias = (SLOPES[h] * LOG2E) * k_pos
            s = scores[h] + jnp.concatenate([bias] * (2 * t // LANES), axis=1)
            if masked:
                s = jnp.where(causal, s, -jnp.inf)
            m_new = jnp.maximum(m, jnp.max(s, axis=0, keepdims=True))
            alpha = jnp.exp2(m - m_new)
            p = jnp.exp2(s - m_new)
            return m_new, alpha * l + jnp.sum(p, axis=0, keepdims=True), alpha, p.astype(BF16)

        def values(h, soft):
            m_new, l, alpha, p = soft
            return m_new, l, alpha * carry[h][2] + _dot(vt_ref[j, heads[h], :], p)

        soft = [softmax(h) for h in range(DIFF_HEADS)]
        return tuple(values(h, soft[h]) for h in range(DIFF_HEADS))

    init = tuple((jnp.full((1, 2 * t), NEG_BIG, F32), jnp.zeros((1, 2 * t), F32),
                  jnp.zeros((HEAD_DIM, 2 * t), F32)) for _ in heads)
    carry = lax.fori_loop(0, i, functools.partial(block, masked=False), init)
    carry = block(i, carry, masked=True)
    for (_, l, acc), hs in zip(carry, heads):
        o = acc / l
        d = o[:, :t] - lam * o[:, t:]
        dn = d * lax.rsqrt(jnp.mean(d * d, axis=0, keepdims=True) + EPS)
        o_ref[:, hs] = (dn.T * sub_ref[:, hs] * (1.0 - LAM_INIT)).astype(BF16)


def _prompt_diff_attention(dq, dk, dvt, lams, sub_w, batch, seq):
    nq = seq // DIFF_T
    qspec = pl.BlockSpec((DIFF_T, HALF), lambda b, i: (b * nq + i, 0))
    kspec = pl.BlockSpec((seq, HALF), lambda b, i: (b, 0))
    vspec = pl.BlockSpec((nq, HALF, DIFF_T), lambda b, i: (b, 0, 0))
    lspec = _const_spec((1, QK_DIM))
    return pl.pallas_call(
        _diff_kernel,
        grid=(batch, nq),
        in_specs=[qspec, kspec, vspec, lspec, lspec, lspec, lspec, _const_spec((1, HALF))],
        out_specs=qspec,
        out_shape=jax.ShapeDtypeStruct((batch * seq, HALF), BF16),
        compiler_params=pltpu.CompilerParams(dimension_semantics=("parallel", "arbitrary"),
                                             vmem_limit_bytes=VMEM_LIMIT),
        name="prompt_diff_attention",
    )(dq, dk, dvt, *lams, sub_w)


class _PagedDecoder:
    def __init__(self, pt_ref, q_ref, kn_ref, vn_ref, lam_refs, sub_ref, ck_hbm, cv_hbm, o_ref,
                 kbuf, vbuf, sems, bias_s, m_s, l_s, acc_s, n_groups):
        self.pt_ref, self.q_ref, self.kn_ref, self.vn_ref = pt_ref, q_ref, kn_ref, vn_ref
        self.lam_refs, self.sub_ref, self.ck_hbm, self.cv_hbm, self.o_ref = lam_refs, sub_ref, ck_hbm, cv_hbm, o_ref
        self.kbuf, self.vbuf, self.sems = kbuf, vbuf, sems
        self.bias_s, self.m_s, self.l_s, self.acc_s = bias_s, m_s, l_s, acc_s
        self.n = q_ref.shape[1]
        self.rows = DIFF_HEADS * 2 * self.n
        self.page_rows = PAGE * DIFF_HEADS
        self.past = n_groups * DECODE_PAGES * PAGE
        r1 = lax.broadcasted_iota(jnp.int32, (self.rows, 1), 0)
        self.head = r1 // (2 * self.n)
        self.qi = r1 % self.n
        self.slope = jnp.zeros((self.rows, 1), F32)
        for h in range(DIFF_HEADS):
            self.slope = jnp.where(self.head == h, SLOPES[h] * LOG2E, self.slope)

    def _copies(self, seq, group):
        slot = group % DECODE_SLOTS
        copies = []
        for p in range(DECODE_PAGES):
            page = self.pt_ref[seq, group * DECODE_PAGES + p]
            copies.append(pltpu.make_async_copy(self.ck_hbm.at[page], self.kbuf.at[slot, p], self.sems.at[0, slot]))
            copies.append(pltpu.make_async_copy(self.cv_hbm.at[page], self.vbuf.at[slot, p], self.sems.at[1, slot]))
        return copies

    def start(self, seq, group):
        for c in self._copies(seq, group):
            c.start()

    def wait(self, seq, group):
        for c in self._copies(seq, group):
            c.wait()

    def init_bias(self):
        col = lax.broadcasted_iota(jnp.int32, self.bias_s.shape, 1)
        dist = (self.past + self.qi - col // DIFF_HEADS).astype(F32)
        self.bias_s[...] = jnp.where(col % DIFF_HEADS == self.head, -self.slope * dist, -jnp.inf)

    def begin_sequence(self):
        self.m_s[...] = jnp.full_like(self.m_s, NEG_BIG)
        self.l_s[...] = jnp.zeros_like(self.l_s)
        self.acc_s[...] = jnp.zeros_like(self.acc_s)
        q = self.q_ref[0]
        self.qq = jnp.concatenate([_split_components(q[:, h * HEAD_DIM:(h + 1) * HEAD_DIM].astype(F32))
                                   for h in range(DIFF_HEADS)], axis=0).astype(BF16)

    def scores(self, g):
        slot, per_chain = g % DECODE_SLOTS, DECODE_PAGES // DECODE_CHAINS
        return [jnp.concatenate([_dot_nt(self.qq, self.kbuf[slot, p].astype(BF16))
                                 for p in range(c * per_chain, (c + 1) * per_chain)], axis=-1)
                for c in range(DECODE_CHAINS)]

    def fold(self, g, scores):
        slot, page_rows = g % DECODE_SLOTS, self.page_rows
        per_chain = DECODE_PAGES // DECODE_CHAINS
        chains = []
        for c, s in enumerate(scores):
            pages = range(c * per_chain, (c + 1) * per_chain)
            s = s + jnp.concatenate([self.bias_s[...] + self.slope * float((g * DECODE_PAGES + p) * PAGE)
                                     for p in pages], axis=-1)
            m_c = jnp.max(s, axis=-1, keepdims=True)
            pb = jnp.exp2(s - m_c)
            l_c = jnp.sum(pb, axis=-1, keepdims=True)
            pb = pb.astype(BF16)
            pv_c = _dot(pb[:, :page_rows], self.vbuf[slot, pages[0]].astype(BF16))
            for idx, pg in enumerate(pages[1:], start=1):
                pv_c = pv_c + _dot(pb[:, idx * page_rows:(idx + 1) * page_rows], self.vbuf[slot, pg].astype(BF16))
            chains.append((m_c, l_c, pv_c))
        m = self.m_s[:, :1]
        m_new = m
        for m_c, _, _ in chains:
            m_new = jnp.maximum(m_new, m_c)
        alpha = jnp.exp2(m - m_new)
        l_new = alpha * self.l_s[:, :1]
        acc = alpha * self.acc_s[...]
        for m_c, l_c, pv_c in chains:
            w = jnp.exp2(m_c - m_new)
            l_new = l_new + w * l_c
            acc = acc + w * pv_c
        self.m_s[...] = jnp.broadcast_to(m_new, (self.rows, LANES))
        self.l_s[...] = jnp.broadcast_to(l_new, (self.rows, LANES))
        self.acc_s[...] = acc

    def finish(self):
        n = self.n
        lam = _lambda(*(ref[...] for ref in self.lam_refs))
        r8 = lax.broadcasted_iota(jnp.int32, (2 * n, 1), 0)
        qi = jnp.where(r8 >= n, r8 - n, r8)
        for h in range(DIFF_HEADS):
            hs = slice(h * HEAD_DIM, (h + 1) * HEAD_DIM)
            hr = slice(h * 2 * n, (h + 1) * 2 * n)
            qq = _split_components(self.q_ref[0, :, hs].astype(F32))
            kn = self.kn_ref[:, h, :]
            vn = self.vn_ref[:, h, :]
            m, l, acc = self.m_s[hr, :1], self.l_s[hr, :1], self.acc_s[hr, :]
            s_cols = []
            for t in range(n):
                s_t = jnp.sum(qq * kn[t:t + 1, :], axis=-1, keepdims=True)
                s_t = s_t - SLOPES[h] * LOG2E * (qi - t).astype(F32)
                s_cols.append(jnp.where(qi >= t, s_t, -jnp.inf))
            m_new = m
            for s_t in s_cols:
                m_new = jnp.maximum(m_new, s_t)
            alpha = jnp.exp2(m - m_new)
            l = alpha * l
            acc = alpha * acc
            for t in range(n):
                p_t = jnp.exp2(s_cols[t] - m_new)
                l = l + p_t
                acc = acc + p_t * vn[t:t + 1, :]
            self.o_ref[0, :, hs] = _sub_norm(acc, l, lam, self.sub_ref[:, hs]).astype(BF16)


def _channel_phases(x_ref, r_ref, dn_ref, p_ref, wo_ref, ln2_ref, wfi_ref, wfo_ref, lnp_ref, wpg_ref, wpp_ref,
                    y_ref):
    st = {}

    def attention_out():
        mixed = jnp.concatenate([r_ref[...], dn_ref[...]], axis=-1)
        st["h"] = x_ref[...] + _dot(mixed, wo_ref[...])
        st["hn"] = _rms(st["h"], ln2_ref[...]).astype(BF16)

    def ffn_chunk(c):
        g = _dot(st["hn"], wfi_ref[:, c * FF_CHUNK:(c + 1) * FF_CHUNK])
        u = _dot(st["hn"], wfi_ref[:, D_FF + c * FF_CHUNK:D_FF + (c + 1) * FF_CHUNK])
        act = (g * _sigmoid(g) * u).astype(BF16)
        part = _dot(act, wfo_ref[c * FF_CHUNK:(c + 1) * FF_CHUNK, :])
        st["ff"] = part if c == 0 else st["ff"] + part

    def embedding_gate():
        h = st["h"] + st["ff"]
        gate = _sigmoid(_dot(_rms(h, lnp_ref[...]).astype(BF16), wpg_ref[...]))
        y_ref[...] = h + gate * _dot(p_ref[...].astype(BF16), wpp_ref[...])

    return ([attention_out] + [functools.partial(ffn_chunk, c) for c in range(D_FF // FF_CHUNK)]
            + [embedding_gate])


def _channel_kernel(*refs):
    for phase in _channel_phases(*refs):
        phase()


def _channel_specs(tm, index):
    tok = lambda width: pl.BlockSpec((tm, width), index)
    return [tok(D_MODEL), tok(HALF), tok(HALF), tok(PLE_DIM),
            _const_spec((D_MODEL, D_MODEL)), _const_spec((1, D_MODEL)),
            _const_spec((D_MODEL, 2 * D_FF)), _const_spec((D_FF, D_MODEL)),
            _const_spec((1, D_MODEL)), _const_spec((D_MODEL, D_MODEL)),
            _const_spec((PLE_DIM, D_MODEL))], tok(D_MODEL)


def _channel(x, r, dn, p, chan_w, tm):
    m = x.shape[0]
    in_specs, out_spec = _channel_specs(tm, lambda i: (i, 0))
    return pl.pallas_call(
        _channel_kernel,
        grid=(m // tm,),
        in_specs=in_specs,
        out_specs=out_spec,
        out_shape=jax.ShapeDtypeStruct((m, D_MODEL), F32),
        compiler_params=pltpu.CompilerParams(dimension_semantics=("parallel",),
                                             vmem_limit_bytes=VMEM_LIMIT),
        name="channel",
    )(x, r, dn, p, *chan_w)


def _channel_decode_kernel(pt_ref, x_ref, r_ref, dn_ref, p_ref, wo_ref, ln2_ref, wfi_ref, wfo_ref, lnp_ref, wpg_ref,
                           wpp_ref, q_ref, kn_ref, vn_ref, lq1_ref, lk1_ref, lq2_ref, lk2_ref, sub_ref, ck_hbm,
                           cv_hbm, y_ref, o_ref, kbuf, vbuf, sems, bias_s, m_s, l_s, acc_s, *, n_seq, n_groups):
    i = pl.program_id(0)
    ahead = DECODE_SLOTS - 1
    chan_refs = (x_ref, r_ref, dn_ref, p_ref, wo_ref, ln2_ref, wfi_ref, wfo_ref, lnp_ref, wpg_ref, wpp_ref)
    dec = _PagedDecoder(pt_ref, q_ref, kn_ref, vn_ref, (lq1_ref, lk1_ref, lq2_ref, lk2_ref), sub_ref, ck_hbm,
                        cv_hbm, o_ref, kbuf, vbuf, sems, bias_s, m_s, l_s, acc_s, n_groups)

    @pl.when(i == 0)
    def _():
        dec.init_bias()
        for g in range(ahead):
            dec.start(0, g)

    dec.begin_sequence()
    phases = _channel_phases(*chan_refs, y_ref)
    per_group = CHANNEL_PHASES_PER_GROUP
    for g in range(n_groups):
        nxt = g + ahead
        if nxt < n_groups:
            dec.start(i, nxt)
        else:
            pl.when(i + 1 < n_seq)(functools.partial(dec.start, i + 1, nxt - n_groups))
        dec.wait(i, g)
        scores = dec.scores(g)
        for phase in phases[g * per_group:(g + 1) * per_group]:
            phase()
        dec.fold(g, scores)
    dec.finish()


def _channel_and_decode(x, r, dn, p, chan_w, page_table, dq, dk_new, dv_new, cache_k, cache_v, lams, sub_w):
    m = x.shape[0]
    db, ds = dq.shape[:2]
    tm = m // db
    assert tm * db == m and tm % 256 == 0, (m, db)
    n_groups = page_table.shape[1] // DECODE_PAGES
    assert n_groups * DECODE_PAGES == page_table.shape[1] and n_groups % DECODE_SLOTS == 0
    chan_in, chan_out = _channel_specs(tm, lambda i, pt: (i, 0))
    tok = pl.BlockSpec((1, ds, HALF), lambda i, pt: (i, 0, 0))
    new = pl.BlockSpec((None, ds, DIFF_HEADS, HEAD_DIM), lambda i, pt: (i, 0, 0, 0))
    page_rows = PAGE * DIFF_HEADS
    cache_k, cache_v = (c.reshape(-1, page_rows, HEAD_DIM) for c in (cache_k, cache_v))
    hbm = pl.BlockSpec(memory_space=pl.ANY)
    ring = pltpu.VMEM((DECODE_SLOTS, DECODE_PAGES, page_rows, HEAD_DIM), F32)
    state = pltpu.VMEM((DIFF_HEADS * 2 * ds, LANES), F32)
    bias = pltpu.VMEM((DIFF_HEADS * 2 * ds, page_rows), F32)
    return pl.pallas_call(
        functools.partial(_channel_decode_kernel, n_seq=db, n_groups=n_groups),
        grid_spec=pltpu.PrefetchScalarGridSpec(
            num_scalar_prefetch=1,
            grid=(db,),
            in_specs=chan_in + [tok, new, new] + [_const_spec((1, QK_DIM))] * 4 + [_const_spec((1, HALF)), hbm, hbm],
            out_specs=[chan_out, tok],
            scratch_shapes=[ring, ring, pltpu.SemaphoreType.DMA((2, DECODE_SLOTS)), bias, state, state, state],
        ),
        out_shape=[jax.ShapeDtypeStruct((m, D_MODEL), F32), jax.ShapeDtypeStruct((db, ds, HALF), BF16)],
        compiler_params=pltpu.CompilerParams(dimension_semantics=("arbitrary",),
                                             vmem_limit_bytes=FUSED_VMEM_LIMIT),
        name="channel_and_decode",
    )(page_table, x, r, dn, p, *chan_w, dq, dk_new, dv_new, *lams, sub_w, cache_k, cache_v)


def kernel(x_prompt, x_sample, cache_k, cache_v, state_ret, page_table, p_prompt, p_sample, ln1_w, w_in, q_norm_w, k_norm_w, lambda_q1, lambda_k1, lambda_q2, lambda_k2, ret_gn_w, ret_gn_b, diff_subln_w, w_o, ln2_w, w_ffn_in, w_ffn_out, ln_ple_w, w_ple_gate, w_ple_proj):
    depth = w_in.shape[0]
    assert depth == 1, "single-layer trunk"
    b, s, _ = x_prompt.shape
    db, ds, _ = x_sample.shape

    row = lambda a: a[0].reshape(1, -1)
    wb = lambda a: a[0].astype(BF16)
    qn_w = jnp.tile(row(q_norm_w), (1, HALF // QK_DIM))
    kn_w = jnp.tile(row(k_norm_w), (1, HALF // QK_DIM))
    lams = (row(lambda_q1), row(lambda_k1), row(lambda_q2), row(lambda_k2))
    proj_w = (row(ln1_w), wb(w_in), qn_w, kn_w)
    chan_w = (wb(w_o), row(ln2_w), wb(w_ffn_in), wb(w_ffn_out), row(ln_ple_w), wb(w_ple_gate), wb(w_ple_proj))
    gn_w, gn_b, sub_w = row(ret_gn_w), row(ret_gn_b), row(diff_subln_w)

    xp = x_prompt.reshape(b * s, D_MODEL)
    dq, dk, dv, dkb, dvt, r, ret_fin = _project_and_retain(xp, *proj_w, gn_w, gn_b, b, s)
    dn = _prompt_diff_attention(dq, dkb, dvt, lams, sub_w, b, s)

    xs = x_sample.reshape(db * ds, D_MODEL)
    rq_s, rk_s, rv_s, rg_s, dq_s, dk_s, dv_s, _, _ = _project(xs, *proj_w, tm=db * ds)
    tok3 = lambda a: a.reshape(db, ds, HALF)
    r_s, ret_new = _sample_retention(tok3(rq_s), tok3(rk_s), tok3(rv_s), tok3(rg_s), state_ret[0], gn_w, gn_b)
    new4 = lambda a: a.reshape(db, ds, DIFF_HEADS, HEAD_DIM)
    y_prompt, dn_s = _channel_and_decode(xp, r, dn, p_prompt[0].reshape(b * s, PLE_DIM), chan_w, page_table,
                                         tok3(dq_s), new4(dk_s), new4(dv_s), cache_k, cache_v, lams, sub_w)
    y_sample = _channel(xs, r_s.reshape(db * ds, HALF), dn_s.reshape(db * ds, HALF),
                        p_sample[0].reshape(db * ds, PLE_DIM), chan_w, tm=db * ds)

    heads = lambda a, n: a.reshape(1, n, -1, DIFF_HEADS, HEAD_DIM)
    return (y_prompt.reshape(b, s, D_MODEL), y_sample.reshape(db, ds, D_MODEL),
            heads(dk, b), heads(dv, b), ret_fin[None],
            heads(dk_s, db), heads(dv_s, db), ret_new[None])
```

```python
import functools
import math

import jax
import jax.numpy as jnp
from jax import lax
from jax.experimental import pallas as pl
from jax.experimental.pallas import tpu as pltpu

F32 = jnp.float32
BF16 = jnp.bfloat16

D_MODEL = 1024
RET_HEADS = 4
HEAD_DIM = 128
DIFF_HEADS = 4
QK_DIM = 64
HALF = RET_HEADS * HEAD_DIM
PROJ_WIDTH = 7 * HALF
D_FF = 2816
PLE_DIM = 256
RET_CHUNK = 128
PAGE = 128
EPS = 1e-6
LAM_INIT = 0.8 - 0.6 * math.exp(-0.3 * 0)
LOG_G = tuple(math.log1p(-(2.0 ** (-5.0 - h))) for h in range(RET_HEADS))
SLOPES = tuple(2.0 ** (-8.0 / DIFF_HEADS * (h + 1)) for h in range(DIFF_HEADS))
LOG2E = math.log2(math.e)
NEG_BIG = -1e30

LANES = 128
VMEM_LIMIT = 56 * 1024 * 1024
FUSED_VMEM_LIMIT = 61 * 1024 * 1024

FF_CHUNK = 256
RET_SEQS_PER_STEP = 4
DIFF_T = 512
DECODE_PAGES = 16
DECODE_CHAINS = 2
DECODE_SLOTS = 2


def _sigmoid(x):
    return 1.0 / (1.0 + jnp.exp(-x))


def _rms(x, w):
    return x * lax.rsqrt(jnp.mean(x * x, axis=-1, keepdims=True) + EPS) * w


def _dot(a, b):
    return jnp.dot(a, b, preferred_element_type=F32)


def _dot_nt(a, b):
    return lax.dot_general(a, b, (((1,), (1,)), ((), ())), preferred_element_type=F32)


def _lambda(lq1, lk1, lq2, lk2):
    a = jnp.exp(jnp.sum(lq1 * lk1, axis=-1, keepdims=True))
    b = jnp.exp(jnp.sum(lq2 * lk2, axis=-1, keepdims=True))
    return a - b + LAM_INIT


def _const_spec(shape):
    nd = len(shape)
    return pl.BlockSpec(shape, lambda *_: (0,) * nd, pipeline_mode=pl.Buffered(1))


def _seg_rms(y, w):
    lo_mask = lax.broadcasted_iota(jnp.int32, (1, LANES), 1) < QK_DIM
    outs = []
    for g in range(HALF // LANES):
        yg = y[:, g * LANES:(g + 1) * LANES]
        t = yg * yg
        lo = jnp.sum(jnp.where(lo_mask, t, 0.0), axis=-1, keepdims=True)
        hi = jnp.sum(jnp.where(lo_mask, 0.0, t), axis=-1, keepdims=True)
        ms = jnp.where(lo_mask, lo, hi) * (1.0 / QK_DIM)
        outs.append(yg * lax.rsqrt(ms + EPS))
    return jnp.concatenate(outs, axis=-1) * w


def _store_cache_format(ref, y):
    for h in range(DIFF_HEADS):
        ref[pl.ds(h, y.shape[0], stride=DIFF_HEADS), :] = y[:, h * HEAD_DIM:(h + 1) * HEAD_DIM]


def _projection(x_ref, ln1_ref, w_ref):
    xn = _rms(x_ref[...], ln1_ref[...]).astype(BF16)
    return lambda i: _dot(xn, w_ref[:, i * HALF:(i + 1) * HALF])


def _retention_operands(col):
    return col(0).astype(BF16), (col(1) * (HEAD_DIM ** -0.5)).astype(BF16), col(2).astype(BF16), col(3)


def _diff_projection_steps(col, qn_ref, kn_ref, dq_ref, dk_ref, dv_ref, dkb_ref, dvt_ref):
    def q_step():
        dq = _seg_rms(col(4), qn_ref[...])
        dq_ref[...] = (dq * (LOG2E * QK_DIM ** -0.5)).astype(BF16)

    def k_step():
        dk = _seg_rms(col(5), kn_ref[...])
        dkb_ref[...] = dk.astype(BF16)
        _store_cache_format(dk_ref, dk)

    def v_step():
        dv = col(6)
        dvt_ref[0] = dv.T.astype(BF16)
        _store_cache_format(dv_ref, dv)

    return [q_step, k_step, v_step]


def _proj_kernel(x_ref, ln1_ref, w_ref, qn_ref, kn_ref,
                 rq_ref, rk_ref, rv_ref, rg_ref, dq_ref, dk_ref, dv_ref, dkb_ref, dvt_ref):
    col = _projection(x_ref, ln1_ref, w_ref)
    for step in _diff_projection_steps(col, qn_ref, kn_ref, dq_ref, dk_ref, dv_ref, dkb_ref, dvt_ref):
        step()
    rq_ref[...], rk_ref[...], rv_ref[...], rg_ref[...] = _retention_operands(col)


def _proj_retention_kernel(x_ref, ln1_ref, w_ref, qn_ref, kn_ref, gnw_ref, gnb_ref,
                           dq_ref, dk_ref, dv_ref, dkb_ref, dvt_ref, r_ref, fin_ref, state, *, tiles_per_seq):
    i = pl.program_id(0)

    @pl.when(i % tiles_per_seq == 0)
    def _():
        state[...] = jnp.zeros_like(state)

    col = _projection(x_ref, ln1_ref, w_ref)
    for step in _diff_projection_steps(col, qn_ref, kn_ref, dq_ref, dk_ref, dv_ref, dkb_ref, dvt_ref):
        step()
    _retention_tile(*_retention_operands(col), gnw_ref, gnb_ref, state, r_ref)

    @pl.when((i + 1) % tiles_per_seq == 0)
    def _():
        fin_ref[0] = state[...]


def _proj_specs(m, tm):
    tok = lambda width: pl.BlockSpec((tm, width), lambda i: (i, 0))
    half = lambda dt: jax.ShapeDtypeStruct((m, HALF), dt)
    cache_fmt = pl.BlockSpec((tm * DIFF_HEADS, HEAD_DIM), lambda i: (i, 0))
    cache_shape = jax.ShapeDtypeStruct((m * DIFF_HEADS, HEAD_DIM), F32)
    in_specs = [tok(D_MODEL), _const_spec((1, D_MODEL)), _const_spec((D_MODEL, PROJ_WIDTH)),
                _const_spec((1, HALF)), _const_spec((1, HALF))]
    diff_specs = [tok(HALF), cache_fmt, cache_fmt, tok(HALF), pl.BlockSpec((1, HALF, tm), lambda i: (i, 0, 0))]
    diff_shapes = [half(BF16), cache_shape, cache_shape, half(BF16),
                   jax.ShapeDtypeStruct((m // tm, HALF, tm), BF16)]
    return tok, half, in_specs, diff_specs, diff_shapes


def _project(x, ln1_w, w_in, qn_w, kn_w, tm):
    m = x.shape[0]
    tok, half, in_specs, diff_specs, diff_shapes = _proj_specs(m, tm)
    return pl.pallas_call(
        _proj_kernel,
        grid=(m // tm,),
        in_specs=in_specs,
        out_specs=[tok(HALF)] * 4 + diff_specs,
        out_shape=[half(BF16), half(BF16), half(BF16), half(F32)] + diff_shapes,
        compiler_params=pltpu.CompilerParams(dimension_semantics=("parallel",),
                                             vmem_limit_bytes=VMEM_LIMIT),
        name="proj",
    )(x, ln1_w, w_in, qn_w, kn_w)


def _project_and_retain(x, ln1_w, w_in, qn_w, kn_w, gn_w, gn_b, batch, seq):
    m, tm = x.shape[0], DIFF_T
    tok, half, in_specs, diff_specs, diff_shapes = _proj_specs(m, tm)
    tiles_per_seq = seq // tm
    state = (RET_HEADS, HEAD_DIM, HEAD_DIM)
    return pl.pallas_call(
        functools.partial(_proj_retention_kernel, tiles_per_seq=tiles_per_seq),
        grid=(m // tm,),
        in_specs=in_specs + [_const_spec((1, HALF)), _const_spec((1, HALF))],
        out_specs=diff_specs + [tok(HALF), pl.BlockSpec((1,) + state, lambda i: (i // tiles_per_seq, 0, 0, 0))],
        out_shape=diff_shapes + [half(BF16), jax.ShapeDtypeStruct((batch,) + state, F32)],
        scratch_shapes=[pltpu.VMEM(state, F32)],
        compiler_params=pltpu.CompilerParams(dimension_semantics=("arbitrary",),
                                             vmem_limit_bytes=VMEM_LIMIT),
        name="proj_retention",
    )(x, ln1_w, w_in, qn_w, kn_w, gn_w, gn_b)


def _group_norm_gate(o, g, gn_w, gn_b):
    mu = jnp.mean(o, axis=-1, keepdims=True)
    d = o - mu
    var = jnp.mean(d * d, axis=-1, keepdims=True)
    r = d * lax.rsqrt(var + EPS) * gn_w + gn_b
    return r * (g * _sigmoid(g))


def _retention_tile(rq, rk, rv, rg, gnw_ref, gnb_ref, state, r_ref):
    n = RET_CHUNK
    row = lax.broadcasted_iota(jnp.int32, (n, n), 0).astype(F32)
    col = lax.broadcasted_iota(jnp.int32, (n, n), 1).astype(F32)
    rel = row - col
    heads = [slice(h * HEAD_DIM, (h + 1) * HEAD_DIM) for h in range(RET_HEADS)]
    decay = [jnp.where(rel >= 0, jnp.exp(lg * jnp.maximum(rel, 0.0)), 0.0) for lg in LOG_G]
    cross_w = [jnp.exp(lg * (row + 1.0)) for lg in LOG_G]
    k_w = [jnp.exp(lg * (n - 1.0 - row)) for lg in LOG_G]
    for c in range(rq.shape[0] // n):
        rows = slice(c * n, (c + 1) * n)
        first = []
        for h, hs in enumerate(heads):
            q, k, v = rq[rows, hs], rk[rows, hs], rv[rows, hs]
            st = state[h]
            first.append((_dot_nt(q, k), _dot(q, st.astype(BF16)), v))
            kw_t = (k.astype(F32) * k_w[h]).T.astype(BF16)
            state[h] = math.exp(LOG_G[h] * n) * st + _dot(kw_t, v)
        for h, hs in enumerate(heads):
            s, cross, v = first[h]
            o = _dot((s * decay[h]).astype(BF16), v) + cross * cross_w[h]
            r = _group_norm_gate(o, rg[rows, hs], gnw_ref[:, hs], gnb_ref[:, hs])
            r_ref[rows, hs] = r.astype(BF16)


def _ret_step_kernel(rq_ref, rk_ref, rv_ref, rg_ref, st_ref, gnw_ref, gnb_ref, r_ref, new_ref):
    n = rq_ref.shape[1]
    row = lax.broadcasted_iota(jnp.int32, (n, LANES), 0).astype(F32)
    for b, h in [(b, h) for b in range(rq_ref.shape[0]) for h in range(RET_HEADS)]:
        hs = slice(h * HEAD_DIM, (h + 1) * HEAD_DIM)
        lg = LOG_G[h]
        q = rq_ref[b, :, hs].astype(F32)
        k = rk_ref[b, :, hs].astype(F32)
        v = rv_ref[b, :, hs].astype(F32)
        st = st_ref[b, h]
        q_pad = jnp.concatenate([q, jnp.zeros((16 - n, HEAD_DIM), F32)], axis=0).astype(BF16)
        o = _dot(q_pad, st.astype(BF16))[:n] * jnp.exp(lg * (row + 1.0))
        new = math.exp(lg * n) * st
        k_t = jnp.concatenate([k, jnp.zeros((8 - n, HEAD_DIM), F32)], axis=0).T
        for m in range(n):
            s_m = jnp.sum(q * k[m:m + 1, :], axis=-1, keepdims=True)
            decay = jnp.where(row >= m, jnp.exp(lg * jnp.maximum(row - m, 0.0)), 0.0)
            o = o + (s_m * decay) * v[m:m + 1, :]
            new = new + math.exp(lg * (n - 1.0 - m)) * (k_t[:, m:m + 1] * v[m:m + 1, :])
        new_ref[b, h] = new
        r = _group_norm_gate(o, rg_ref[b, :, hs], gnw_ref[:, hs], gnb_ref[:, hs])
        r_ref[b, :, hs] = r.astype(BF16)


def _sample_retention(rq, rk, rv, rg, state, gn_w, gn_b):
    db, ds = rq.shape[:2]
    nb = RET_SEQS_PER_STEP
    tok = pl.BlockSpec((nb, ds, HALF), lambda b: (b, 0, 0))
    st = pl.BlockSpec((nb, RET_HEADS, HEAD_DIM, HEAD_DIM), lambda b: (b, 0, 0, 0))
    return pl.pallas_call(
        _ret_step_kernel,
        grid=(db // nb,),
        in_specs=[tok, tok, tok, tok, st, _const_spec((1, HALF)), _const_spec((1, HALF))],
        out_specs=[tok, st],
        out_shape=[jax.ShapeDtypeStruct((db, ds, HALF), BF16),
                   jax.ShapeDtypeStruct(state.shape, F32)],
        compiler_params=pltpu.CompilerParams(dimension_semantics=("parallel",),
                                             vmem_limit_bytes=VMEM_LIMIT),
        name="sample_retention",
    )(rq, rk, rv, rg, state, gn_w, gn_b)


def _split_components(q):
    lo_mask = lax.broadcasted_iota(jnp.int32, q.shape, 1) < QK_DIM
    zero = jnp.zeros_like(q)
    return jnp.concatenate([jnp.where(lo_mask, q, zero), jnp.where(lo_mask, zero, q)], axis=0)


def _sub_norm(acc, l, lam, w):
    t = acc.shape[0] // 2
    o = acc / l
    d = o[:t] - lam * o[t:]
    return d * lax.rsqrt(jnp.mean(d * d, axis=-1, keepdims=True) + EPS) * w * (1.0 - LAM_INIT)


def _diff_kernel(q_ref, k_ref, vt_ref, lq1_ref, lk1_ref, lq2_ref, lk2_ref, sub_ref, o_ref):
    i = pl.program_id(1)
    t = DIFF_T
    lam = _lambda(lq1_ref[...], lk1_ref[...], lq2_ref[...], lk2_ref[...])
    k_idx = lax.broadcasted_iota(jnp.int32, (t, LANES), 0).astype(F32)
    key = lax.broadcasted_iota(jnp.int32, (t, 2 * t), 0)
    col = lax.broadcasted_iota(jnp.int32, (t, 2 * t), 1)
    causal = key <= jnp.where(col >= t, col - t, col)
    heads = [slice(h * HEAD_DIM, (h + 1) * HEAD_DIM) for h in range(DIFF_HEADS)]
    qqs = [_split_components(q_ref[:, hs]) for hs in heads]

    def block(j, carry, masked):
        start = pl.multiple_of(j * t, t)
        k_pos = k_idx + (j * t).astype(F32)
        scores = [_dot_nt(k_ref[pl.ds(start, t), hs], qqs[h]) for h, hs in enumerate(heads)]

        def softmax(h):
            m, l, _ = carry[h]
            bias = (SLOPES[h] * LOG2E) * k_pos
            s = scores[h] + jnp.concatenate([bias] * (2 * t // LANES), axis=1)
            if masked:
                s = jnp.where(causal, s, -jnp.inf)
            m_new = jnp.maximum(m, jnp.max(s, axis=0, keepdims=True))
            alpha = jnp.exp2(m - m_new)
            p = jnp.exp2(s - m_new)
            return m_new, alpha * l + jnp.sum(p, axis=0, keepdims=True), alpha, p.astype(BF16)

        def values(h, soft):
            m_new, l, alpha, p = soft
            return m_new, l, alpha * carry[h][2] + _dot(vt_ref[j, heads[h], :], p)

        soft = [softmax(h) for h in range(DIFF_HEADS)]
        return tuple(values(h, soft[h]) for h in range(DIFF_HEADS))

    init = tuple((jnp.full((1, 2 * t), NEG_BIG, F32), jnp.zeros((1, 2 * t), F32),
                  jnp.zeros((HEAD_DIM, 2 * t), F32)) for _ in heads)
    carry = lax.fori_loop(0, i, functools.partial(block, masked=False), init)
    carry = block(i, carry, masked=True)
    for (_, l, acc), hs in zip(carry, heads):
        o = acc / l
        d = o[:, :t] - lam * o[:, t:]
        dn = d * lax.rsqrt(jnp.mean(d * d, axis=0, keepdims=True) + EPS)
        o_ref[:, hs] = (dn.T * sub_ref[:, hs] * (1.0 - LAM_INIT)).astype(BF16)


def _prompt_diff_attention(dq, dk, dvt, lams, sub_w, batch, seq):
    nq = seq // DIFF_T
    qspec = pl.BlockSpec((DIFF_T, HALF), lambda b, i: (b * nq + i, 0))
    kspec = pl.BlockSpec((seq, HALF), lambda b, i: (b, 0))
    vspec = pl.BlockSpec((nq, HALF, DIFF_T), lambda b, i: (b, 0, 0))
    lspec = _const_spec((1, QK_DIM))
    return pl.pallas_call(
        _diff_kernel,
        grid=(batch, nq),
        in_specs=[qspec, kspec, vspec, lspec, lspec, lspec, lspec, _const_spec((1, HALF))],
        out_specs=qspec,
        out_shape=jax.ShapeDtypeStruct((batch * seq, HALF), BF16),
        compiler_params=pltpu.CompilerParams(dimension_semantics=("parallel", "arbitrary"),
                                             vmem_limit_bytes=VMEM_LIMIT),
        name="prompt_diff_attention",
    )(dq, dk, dvt, *lams, sub_w)


class _PagedDecoder:
    def __init__(self, pt_ref, q_ref, kn_ref, vn_ref, lam_refs, sub_ref, ck_hbm, cv_hbm, o_ref,
                 kbuf, vbuf, sems, bias_s, m_s, l_s, acc_s, n_groups):
        self.pt_ref, self.q_ref, self.kn_ref, self.vn_ref = pt_ref, q_ref, kn_ref, vn_ref
        self.lam_refs, self.sub_ref, self.ck_hbm, self.cv_hbm, self.o_ref = lam_refs, sub_ref, ck_hbm, cv_hbm, o_ref
        self.kbuf, self.vbuf, self.sems = kbuf, vbuf, sems
        self.bias_s, self.m_s, self.l_s, self.acc_s = bias_s, m_s, l_s, acc_s
        self.n = q_ref.shape[1]
        self.rows = DIFF_HEADS * 2 * self.n
        self.page_rows = PAGE * DIFF_HEADS
        self.past = n_groups * DECODE_PAGES * PAGE
        r1 = lax.broadcasted_iota(jnp.int32, (self.rows, 1), 0)
        self.head = r1 // (2 * self.n)
        self.qi = r1 % self.n
        self.slope = jnp.zeros((self.rows, 1), F32)
        for h in range(DIFF_HEADS):
            self.slope = jnp.where(self.head == h, SLOPES[h] * LOG2E, self.slope)

    def _copies(self, seq, group):
        slot = group % DECODE_SLOTS
        copies = []
        for p in range(DECODE_PAGES):
            page = self.pt_ref[seq, group * DECODE_PAGES + p]
            copies.append(pltpu.make_async_copy(self.ck_hbm.at[page], self.kbuf.at[slot, p], self.sems.at[0, slot]))
            copies.append(pltpu.make_async_copy(self.cv_hbm.at[page], self.vbuf.at[slot, p], self.sems.at[1, slot]))
        return copies

    def start(self, seq, group):
        for c in self._copies(seq, group):
            c.start()

    def wait(self, seq, group):
        for c in self._copies(seq, group):
            c.wait()

    def init_bias(self):
        col = lax.broadcasted_iota(jnp.int32, self.bias_s.shape, 1)
        dist = (self.past + self.qi - col // DIFF_HEADS).astype(F32)
        self.bias_s[...] = jnp.where(col % DIFF_HEADS == self.head, -self.slope * dist, -jnp.inf)

    def begin_sequence(self):
        self.m_s[...] = jnp.full_like(self.m_s, NEG_BIG)
        self.l_s[...] = jnp.zeros_like(self.l_s)
        self.acc_s[...] = jnp.zeros_like(self.acc_s)
        q = self.q_ref[0]
        self.qq = jnp.concatenate([_split_components(q[:, h * HEAD_DIM:(h + 1) * HEAD_DIM].astype(F32))
                                   for h in range(DIFF_HEADS)], axis=0).astype(BF16)

    def scores(self, g):
        slot, per_chain = g % DECODE_SLOTS, DECODE_PAGES // DECODE_CHAINS
        return [jnp.concatenate([_dot_nt(self.qq, self.kbuf[slot, p].astype(BF16))
                                 for p in range(c * per_chain, (c + 1) * per_chain)], axis=-1)
                for c in range(DECODE_CHAINS)]

    def fold(self, g, scores):
        slot, page_rows = g % DECODE_SLOTS, self.page_rows
        per_chain = DECODE_PAGES // DECODE_CHAINS
        chains = []
        for c, s in enumerate(scores):
            pages = range(c * per_chain, (c + 1) * per_chain)
            s = s + jnp.concatenate([self.bias_s[...] + self.slope * float((g * DECODE_PAGES + p) * PAGE)
                                     for p in pages], axis=-1)
            m_c = jnp.max(s, axis=-1, keepdims=True)
            pb = jnp.exp2(s - m_c)
            l_c = jnp.sum(pb, axis=-1, keepdims=True)
            pb = pb.astype(BF16)
            pv_c = _dot(pb[:, :page_rows], self.vbuf[slot, pages[0]].astype(BF16))
            for idx, pg in enumerate(pages[1:], start=1):
                pv_c = pv_c + _dot(pb[:, idx * page_rows:(idx + 1) * page_rows], self.vbuf[slot, pg].astype(BF16))
            chains.append((m_c, l_c, pv_c))
        m = self.m_s[:, :1]
        m_new = m
        for m_c, _, _ in chains:
            m_new = jnp.maximum(m_new, m_c)
        alpha = jnp.exp2(m - m_new)
        l_new = alpha * self.l_s[:, :1]
        acc = alpha * self.acc_s[...]
        for m_c, l_c, pv_c in chains:
            w = jnp.exp2(m_c - m_new)
            l_new = l_new + w * l_c
            acc = acc + w * pv_c
        self.m_s[...] = jnp.broadcast_to(m_new, (self.rows, LANES))
        self.l_s[...] = jnp.broadcast_to(l_new, (self.rows, LANES))
        self.acc_s[...] = acc

    def finish(self):
        n = self.n
        lam = _lambda(*(ref[...] for ref in self.lam_refs))
        r8 = lax.broadcasted_iota(jnp.int32, (2 * n, 1), 0)
        qi = jnp.where(r8 >= n, r8 - n, r8)
        for h in range(DIFF_HEADS):
            hs = slice(h * HEAD_DIM, (h + 1) * HEAD_DIM)
            hr = slice(h * 2 * n, (h + 1) * 2 * n)
            qq = _split_components(self.q_ref[0, :, hs].astype(F32))
            kn = self.kn_ref[:, h, :]
            vn = self.vn_ref[:, h, :]
            m, l, acc = self.m_s[hr, :1], self.l_s[hr, :1], self.acc_s[hr, :]
            s_cols = []
            for t in range(n):
                s_t = jnp.sum(qq * kn[t:t + 1, :], axis=-1, keepdims=True)
                s_t = s_t - SLOPES[h] * LOG2E * (qi - t).astype(F32)
                s_cols.append(jnp.where(qi >= t, s_t, -jnp.inf))
            m_new = m
            for s_t in s_cols:
                m_new = jnp.maximum(m_new, s_t)
            alpha = jnp.exp2(m - m_new)
            l = alpha * l
            acc = alpha * acc
            for t in range(n):
                p_t = jnp.exp2(s_cols[t] - m_new)
                l = l + p_t
                acc = acc + p_t * vn[t:t + 1, :]
            self.o_ref[0, :, hs] = _sub_norm(acc, l, lam, self.sub_ref[:, hs]).astype(BF16)


def _channel_phases(x_ref, r_ref, dn_ref, p_ref, wo_ref, ln2_ref, wfi_ref, wfo_ref, lnp_ref, wpg_ref, wpp_ref,
                    y_ref):
    st = {}

    def attention_out():
        mixed = jnp.concatenate([r_ref[...], dn_ref[...]], axis=-1)
        st["h"] = x_ref[...] + _dot(mixed, wo_ref[...])
        st["hn"] = _rms(st["h"], ln2_ref[...]).astype(BF16)

    def ffn_chunk(c):
        g = _dot(st["hn"], wfi_ref[:, c * FF_CHUNK:(c + 1) * FF_CHUNK])
        u = _dot(st["hn"], wfi_ref[:, D_FF + c * FF_CHUNK:D_FF + (c + 1) * FF_CHUNK])
        act = (g * _sigmoid(g) * u).astype(BF16)
        part = _dot(act, wfo_ref[c * FF_CHUNK:(c + 1) * FF_CHUNK, :])
        st["ff"] = part if c == 0 else st["ff"] + part

    def embedding_gate():
        h = st["h"] + st["ff"]
        gate = _sigmoid(_dot(_rms(h, lnp_ref[...]).astype(BF16), wpg_ref[...]))
        y_ref[...] = h + gate * _dot(p_ref[...].astype(BF16), wpp_ref[...])

    return ([attention_out] + [functools.partial(ffn_chunk, c) for c in range(D_FF // FF_CHUNK)]
            + [embedding_gate])


def _channel_kernel(*refs):
    for phase in _channel_phases(*refs):
        phase()


def _channel_specs(tm, index):
    tok = lambda width: pl.BlockSpec((tm, width), index)
    return [tok(D_MODEL), tok(HALF), tok(HALF), tok(PLE_DIM),
            _const_spec((D_MODEL, D_MODEL)), _const_spec((1, D_MODEL)),
            _const_spec((D_MODEL, 2 * D_FF)), _const_spec((D_FF, D_MODEL)),
            _const_spec((1, D_MODEL)), _const_spec((D_MODEL, D_MODEL)),
            _const_spec((PLE_DIM, D_MODEL))], tok(D_MODEL)


def _channel(x, r, dn, p, chan_w, tm):
    m = x.shape[0]
    in_specs, out_spec = _channel_specs(tm, lambda i: (i, 0))
    return pl.pallas_call(
        _channel_kernel,
        grid=(m // tm,),
        in_specs=in_specs,
        out_specs=out_spec,
        out_shape=jax.ShapeDtypeStruct((m, D_MODEL), F32),
        compiler_params=pltpu.CompilerParams(dimension_semantics=("parallel",),
                                             vmem_limit_bytes=VMEM_LIMIT),
        name="channel",
    )(x, r, dn, p, *chan_w)


def _channel_decode_kernel(pt_ref, x_ref, r_ref, dn_ref, p_ref, wo_ref, ln2_ref, wfi_ref, wfo_ref, lnp_ref, wpg_ref,
                           wpp_ref, q_ref, kn_ref, vn_ref, lq1_ref, lk1_ref, lq2_ref, lk2_ref, sub_ref, ck_hbm,
                           cv_hbm, y_ref, o_ref, kbuf, vbuf, sems, bias_s, m_s, l_s, acc_s, *, n_seq, n_groups):
    i = pl.program_id(0)
    ahead = DECODE_SLOTS - 1
    chan_refs = (x_ref, r_ref, dn_ref, p_ref, wo_ref, ln2_ref, wfi_ref, wfo_ref, lnp_ref, wpg_ref, wpp_ref)
    dec = _PagedDecoder(pt_ref, q_ref, kn_ref, vn_ref, (lq1_ref, lk1_ref, lq2_ref, lk2_ref), sub_ref, ck_hbm,
                        cv_hbm, o_ref, kbuf, vbuf, sems, bias_s, m_s, l_s, acc_s, n_groups)

    @pl.when(i == 0)
    def _():
        dec.init_bias()
        for g in range(ahead):
            dec.start(0, g)

    dec.begin_sequence()
    phases = _channel_phases(*chan_refs, y_ref)
    per_group = -(-len(phases) // n_groups)
    for g in range(n_groups):
        nxt = g + ahead
        if nxt < n_groups:
            dec.start(i, nxt)
        else:
            pl.when(i + 1 < n_seq)(functools.partial(dec.start, i + 1, nxt - n_groups))
        dec.wait(i, g)
        scores = dec.scores(g)
        for phase in phases[g * per_group:(g + 1) * per_group]:
            phase()
        dec.fold(g, scores)
    dec.finish()


def _channel_and_decode(x, r, dn, p, chan_w, page_table, dq, dk_new, dv_new, cache_k, cache_v, lams, sub_w):
    m = x.shape[0]
    db, ds = dq.shape[:2]
    tm = m // db
    assert tm * db == m and tm % 256 == 0, (m, db)
    n_groups = page_table.shape[1] // DECODE_PAGES
    assert n_groups * DECODE_PAGES == page_table.shape[1] and n_groups % DECODE_SLOTS == 0
    chan_in, chan_out = _channel_specs(tm, lambda i, pt: (i, 0))
    tok = pl.BlockSpec((1, ds, HALF), lambda i, pt: (i, 0, 0))
    new = pl.BlockSpec((None, ds, DIFF_HEADS, HEAD_DIM), lambda i, pt: (i, 0, 0, 0))
    page_rows = PAGE * DIFF_HEADS
    cache_k, cache_v = (c.reshape(-1, page_rows, HEAD_DIM) for c in (cache_k, cache_v))
    hbm = pl.BlockSpec(memory_space=pl.ANY)
    ring = pltpu.VMEM((DECODE_SLOTS, DECODE_PAGES, page_rows, HEAD_DIM), F32)
    state = pltpu.VMEM((DIFF_HEADS * 2 * ds, LANES), F32)
    bias = pltpu.VMEM((DIFF_HEADS * 2 * ds, page_rows), F32)
    return pl.pallas_call(
        functools.partial(_channel_decode_kernel, n_seq=db, n_groups=n_groups),
        grid_spec=pltpu.PrefetchScalarGridSpec(
            num_scalar_prefetch=1,
            grid=(db,),
            in_specs=chan_in + [tok, new, new] + [_const_spec((1, QK_DIM))] * 4 + [_const_spec((1, HALF)), hbm, hbm],
            out_specs=[chan_out, tok],
            scratch_shapes=[ring, ring, pltpu.SemaphoreType.DMA((2, DECODE_SLOTS)), bias, state, state, state],
        ),
        out_shape=[jax.ShapeDtypeStruct((m, D_MODEL), F32), jax.ShapeDtypeStruct((db, ds, HALF), BF16)],
        compiler_params=pltpu.CompilerParams(dimension_semantics=("arbitrary",),
                                             vmem_limit_bytes=FUSED_VMEM_LIMIT),
        name="channel_and_decode",
    )(page_table, x, r, dn, p, *chan_w, dq, dk_new, dv_new, *lams, sub_w, cache_k, cache_v)


def kernel(x_prompt, x_sample, cache_k, cache_v, state_ret, page_table, p_prompt, p_sample, ln1_w, w_in, q_norm_w, k_norm_w, lambda_q1, lambda_k1, lambda_q2, lambda_k2, ret_gn_w, ret_gn_b, diff_subln_w, w_o, ln2_w, w_ffn_in, w_ffn_out, ln_ple_w, w_ple_gate, w_ple_proj):
    depth = w_in.shape[0]
    assert depth == 1, "single-layer trunk"
    b, s, _ = x_prompt.shape
    db, ds, _ = x_sample.shape

    row = lambda a: a[0].reshape(1, -1)
    wb = lambda a: a[0].astype(BF16)
    qn_w = jnp.tile(row(q_norm_w), (1, HALF // QK_DIM))
    kn_w = jnp.tile(row(k_norm_w), (1, HALF // QK_DIM))
    lams = (row(lambda_q1), row(lambda_k1), row(lambda_q2), row(lambda_k2))
    proj_w = (row(ln1_w), wb(w_in), qn_w, kn_w)
    chan_w = (wb(w_o), row(ln2_w), wb(w_ffn_in), wb(w_ffn_out), row(ln_ple_w), wb(w_ple_gate), wb(w_ple_proj))
    gn_w, gn_b, sub_w = row(ret_gn_w), row(ret_gn_b), row(diff_subln_w)

    xp = x_prompt.reshape(b * s, D_MODEL)
    dq, dk, dv, dkb, dvt, r, ret_fin = _project_and_retain(xp, *proj_w, gn_w, gn_b, b, s)
    dn = _prompt_diff_attention(dq, dkb, dvt, lams, sub_w, b, s)

    xs = x_sample.reshape(db * ds, D_MODEL)
    rq_s, rk_s, rv_s, rg_s, dq_s, dk_s, dv_s, _, _ = _project(xs, *proj_w, tm=db * ds)
    tok3 = lambda a: a.reshape(db, ds, HALF)
    r_s, ret_new = _sample_retention(tok3(rq_s), tok3(rk_s), tok3(rv_s), tok3(rg_s), state_ret[0], gn_w, gn_b)
    new4 = lambda a: a.reshape(db, ds, DIFF_HEADS, HEAD_DIM)
    y_prompt, dn_s = _channel_and_decode(xp, r, dn, p_prompt[0].reshape(b * s, PLE_DIM), chan_w, page_table,
                                         tok3(dq_s), new4(dk_s), new4(dv_s), cache_k, cache_v, lams, sub_w)
    y_sample = _channel(xs, r_s.reshape(db * ds, HALF), dn_s.reshape(db * ds, HALF),
                        p_sample[0].reshape(db * ds, PLE_DIM), chan_w, tm=db * ds)

    heads = lambda a, n: a.reshape(1, n, -1, DIFF_HEADS, HEAD_DIM)
    return (y_prompt.reshape(b, s, D_MODEL), y_sample.reshape(db, ds, D_MODEL),
            heads(dk, b), heads(dv, b), ret_fin[None],
            heads(dk_s, db), heads(dv_s, db), ret_new[None])
```

```python
import functools
import math

import jax
import jax.numpy as jnp
from jax import lax
from jax.experimental import pallas as pl
from jax.experimental.pallas import tpu as pltpu

F32 = jnp.float32
BF16 = jnp.bfloat16

D_MODEL = 1024
RET_HEADS = 4
HEAD_DIM = 128
DIFF_HEADS = 4
QK_DIM = 64
HALF = RET_HEADS * HEAD_DIM
PROJ_WIDTH = 7 * HALF
D_FF = 2816
PLE_DIM = 256
RET_CHUNK = 128
PAGE = 128
EPS = 1e-6
LAM_INIT = 0.8 - 0.6 * math.exp(-0.3 * 0)
LOG_G = tuple(math.log1p(-(2.0 ** (-5.0 - h))) for h in range(RET_HEADS))
SLOPES = tuple(2.0 ** (-8.0 / DIFF_HEADS * (h + 1)) for h in range(DIFF_HEADS))
LOG2E = math.log2(math.e)
NEG_BIG = -1e30

LANES = 128
VMEM_LIMIT = 56 * 1024 * 1024
FUSED_VMEM_LIMIT = 61 * 1024 * 1024

FF_CHUNK = 256
RET_SEQS_PER_STEP = 4
DIFF_T = 512
DECODE_PAGES = 16
DECODE_CHAINS = 2
DECODE_SLOTS = 2


def _sigmoid(x):
    return 1.0 / (1.0 + jnp.exp(-x))


def _rms(x, w):
    return x * lax.rsqrt(jnp.mean(x * x, axis=-1, keepdims=True) + EPS) * w


def _dot(a, b):
    return jnp.dot(a, b, preferred_element_type=F32)


def _dot_nt(a, b):
    return lax.dot_general(a, b, (((1,), (1,)), ((), ())), preferred_element_type=F32)


def _lambda(lq1, lk1, lq2, lk2):
    a = jnp.exp(jnp.sum(lq1 * lk1, axis=-1, keepdims=True))
    b = jnp.exp(jnp.sum(lq2 * lk2, axis=-1, keepdims=True))
    return a - b + LAM_INIT


def _const_spec(shape):
    nd = len(shape)
    return pl.BlockSpec(shape, lambda *_: (0,) * nd, pipeline_mode=pl.Buffered(1))


def _seg_rms(y, w):
    lo_mask = lax.broadcasted_iota(jnp.int32, (1, LANES), 1) < QK_DIM
    outs = []
    for g in range(HALF // LANES):
        yg = y[:, g * LANES:(g + 1) * LANES]
        t = yg * yg
        lo = jnp.sum(jnp.where(lo_mask, t, 0.0), axis=-1, keepdims=True)
        hi = jnp.sum(jnp.where(lo_mask, 0.0, t), axis=-1, keepdims=True)
        ms = jnp.where(lo_mask, lo, hi) * (1.0 / QK_DIM)
        outs.append(yg * lax.rsqrt(ms + EPS))
    return jnp.concatenate(outs, axis=-1) * w


def _store_cache_format(ref, y):
    for h in range(DIFF_HEADS):
        ref[pl.ds(h, y.shape[0], stride=DIFF_HEADS), :] = y[:, h * HEAD_DIM:(h + 1) * HEAD_DIM]


def _projection(x_ref, ln1_ref, w_ref):
    xn = _rms(x_ref[...], ln1_ref[...]).astype(BF16)
    return lambda i: _dot(xn, w_ref[:, i * HALF:(i + 1) * HALF])


def _retention_operands(col):
    return col(0).astype(BF16), (col(1) * (HEAD_DIM ** -0.5)).astype(BF16), col(2).astype(BF16), col(3)


def _diff_projection_steps(col, qn_ref, kn_ref, dq_ref, dk_ref, dv_ref, dkb_ref, dvt_ref):
    def q_step():
        dq = _seg_rms(col(4), qn_ref[...])
        dq_ref[...] = (dq * (LOG2E * QK_DIM ** -0.5)).astype(BF16)

    def k_step():
        dk = _seg_rms(col(5), kn_ref[...])
        dkb_ref[...] = dk.astype(BF16)
        _store_cache_format(dk_ref, dk)

    def v_step():
        dv = col(6)
        dvt_ref[0] = dv.T.astype(BF16)
        _store_cache_format(dv_ref, dv)

    return [q_step, k_step, v_step]


def _proj_kernel(x_ref, ln1_ref, w_ref, qn_ref, kn_ref,
                 rq_ref, rk_ref, rv_ref, rg_ref, dq_ref, dk_ref, dv_ref, dkb_ref, dvt_ref):
    col = _projection(x_ref, ln1_ref, w_ref)
    for step in _diff_projection_steps(col, qn_ref, kn_ref, dq_ref, dk_ref, dv_ref, dkb_ref, dvt_ref):
        step()
    rq_ref[...], rk_ref[...], rv_ref[...], rg_ref[...] = _retention_operands(col)


def _proj_retention_kernel(x_ref, ln1_ref, w_ref, qn_ref, kn_ref, gnw_ref, gnb_ref,
                           dq_ref, dk_ref, dv_ref, dkb_ref, dvt_ref, r_ref, fin_ref, state, *, tiles_per_seq):
    i = pl.program_id(0)

    @pl.when(i % tiles_per_seq == 0)
    def _():
        state[...] = jnp.zeros_like(state)

    col = _projection(x_ref, ln1_ref, w_ref)
    for step in _diff_projection_steps(col, qn_ref, kn_ref, dq_ref, dk_ref, dv_ref, dkb_ref, dvt_ref):
        step()
    _retention_tile(*_retention_operands(col), gnw_ref, gnb_ref, state, r_ref)

    @pl.when((i + 1) % tiles_per_seq == 0)
    def _():
        fin_ref[0] = state[...]


def _proj_specs(m, tm):
    tok = lambda width: pl.BlockSpec((tm, width), lambda i: (i, 0))
    half = lambda dt: jax.ShapeDtypeStruct((m, HALF), dt)
    cache_fmt = pl.BlockSpec((tm * DIFF_HEADS, HEAD_DIM), lambda i: (i, 0))
    cache_shape = jax.ShapeDtypeStruct((m * DIFF_HEADS, HEAD_DIM), F32)
    in_specs = [tok(D_MODEL), _const_spec((1, D_MODEL)), _const_spec((D_MODEL, PROJ_WIDTH)),
                _const_spec((1, HALF)), _const_spec((1, HALF))]
    diff_specs = [tok(HALF), cache_fmt, cache_fmt, tok(HALF), pl.BlockSpec((1, HALF, tm), lambda i: (i, 0, 0))]
    diff_shapes = [half(BF16), cache_shape, cache_shape, half(BF16),
                   jax.ShapeDtypeStruct((m // tm, HALF, tm), BF16)]
    return tok, half, in_specs, diff_specs, diff_shapes


def _project(x, ln1_w, w_in, qn_w, kn_w, tm):
    m = x.shape[0]
    tok, half, in_specs, diff_specs, diff_shapes = _proj_specs(m, tm)
    return pl.pallas_call(
        _proj_kernel,
        grid=(m // tm,),
        in_specs=in_specs,
        out_specs=[tok(HALF)] * 4 + diff_specs,
        out_shape=[half(BF16), half(BF16), half(BF16), half(F32)] + diff_shapes,
        compiler_params=pltpu.CompilerParams(dimension_semantics=("parallel",),
                                             vmem_limit_bytes=VMEM_LIMIT),
        name="proj",
    )(x, ln1_w, w_in, qn_w, kn_w)


def _project_and_retain(x, ln1_w, w_in, qn_w, kn_w, gn_w, gn_b, batch, seq):
    m, tm = x.shape[0], DIFF_T
    tok, half, in_specs, diff_specs, diff_shapes = _proj_specs(m, tm)
    tiles_per_seq = seq // tm
    state = (RET_HEADS, HEAD_DIM, HEAD_DIM)
    return pl.pallas_call(
        functools.partial(_proj_retention_kernel, tiles_per_seq=tiles_per_seq),
        grid=(m // tm,),
        in_specs=in_specs + [_const_spec((1, HALF)), _const_spec((1, HALF))],
        out_specs=diff_specs + [tok(HALF), pl.BlockSpec((1,) + state, lambda i: (i // tiles_per_seq, 0, 0, 0))],
        out_shape=diff_shapes + [half(BF16), jax.ShapeDtypeStruct((batch,) + state, F32)],
        scratch_shapes=[pltpu.VMEM(state, F32)],
        compiler_params=pltpu.CompilerParams(dimension_semantics=("arbitrary",),
                                             vmem_limit_bytes=VMEM_LIMIT),
        name="proj_retention",
    )(x, ln1_w, w_in, qn_w, kn_w, gn_w, gn_b)


def _group_norm_gate(o, g, gn_w, gn_b):
    mu = jnp.mean(o, axis=-1, keepdims=True)
    d = o - mu
    var = jnp.mean(d * d, axis=-1, keepdims=True)
    r = d * lax.rsqrt(var + EPS) * gn_w + gn_b
    return r * (g * _sigmoid(g))


def _retention_tile(rq, rk, rv, rg, gnw_ref, gnb_ref, state, r_ref):
    n = RET_CHUNK
    row = lax.broadcasted_iota(jnp.int32, (n, n), 0).astype(F32)
    col = lax.broadcasted_iota(jnp.int32, (n, n), 1).astype(F32)
    rel = row - col
    heads = [slice(h * HEAD_DIM, (h + 1) * HEAD_DIM) for h in range(RET_HEADS)]
    decay = [jnp.where(rel >= 0, jnp.exp(lg * jnp.maximum(rel, 0.0)), 0.0) for lg in LOG_G]
    cross_w = [jnp.exp(lg * (row + 1.0)) for lg in LOG_G]
    k_w = [jnp.exp(lg * (n - 1.0 - row)) for lg in LOG_G]
    for c in range(rq.shape[0] // n):
        rows = slice(c * n, (c + 1) * n)
        first = []
        for h, hs in enumerate(heads):
            q, k, v = rq[rows, hs], rk[rows, hs], rv[rows, hs]
            st = state[h]
            first.append((_dot_nt(q, k), _dot(q, st.astype(BF16)), v))
            kw_t = (k.astype(F32) * k_w[h]).T.astype(BF16)
            state[h] = math.exp(LOG_G[h] * n) * st + _dot(kw_t, v)
        for h, hs in enumerate(heads):
            s, cross, v = first[h]
            o = _dot((s * decay[h]).astype(BF16), v) + cross * cross_w[h]
            r = _group_norm_gate(o, rg[rows, hs], gnw_ref[:, hs], gnb_ref[:, hs])
            r_ref[rows, hs] = r.astype(BF16)


def _ret_step_kernel(rq_ref, rk_ref, rv_ref, rg_ref, st_ref, gnw_ref, gnb_ref, r_ref, new_ref):
    n = rq_ref.shape[1]
    row = lax.broadcasted_iota(jnp.int32, (n, LANES), 0).astype(F32)
    for b, h in [(b, h) for b in range(rq_ref.shape[0]) for h in range(RET_HEADS)]:
        hs = slice(h * HEAD_DIM, (h + 1) * HEAD_DIM)
        lg = LOG_G[h]
        q = rq_ref[b, :, hs].astype(F32)
        k = rk_ref[b, :, hs].astype(F32)
        v = rv_ref[b, :, hs].astype(F32)
        st = st_ref[b, h]
        q_pad = jnp.concatenate([q, jnp.zeros((16 - n, HEAD_DIM), F32)], axis=0).astype(BF16)
        o = _dot(q_pad, st.astype(BF16))[:n] * jnp.exp(lg * (row + 1.0))
        new = math.exp(lg * n) * st
        k_t = jnp.concatenate([k, jnp.zeros((8 - n, HEAD_DIM), F32)], axis=0).T
        for m in range(n):
            s_m = jnp.sum(q * k[m:m + 1, :], axis=-1, keepdims=True)
            decay = jnp.where(row >= m, jnp.exp(lg * jnp.maximum(row - m, 0.0)), 0.0)
            o = o + (s_m * decay) * v[m:m + 1, :]
            new = new + math.exp(lg * (n - 1.0 - m)) * (k_t[:, m:m + 1] * v[m:m + 1, :])
        new_ref[b, h] = new
        r = _group_norm_gate(o, rg_ref[b, :, hs], gnw_ref[:, hs], gnb_ref[:, hs])
        r_ref[b, :, hs] = r.astype(BF16)


def _sample_retention(rq, rk, rv, rg, state, gn_w, gn_b):
    db, ds = rq.shape[:2]
    nb = RET_SEQS_PER_STEP
    tok = pl.BlockSpec((nb, ds, HALF), lambda b: (b, 0, 0))
    st = pl.BlockSpec((nb, RET_HEADS, HEAD_DIM, HEAD_DIM), lambda b: (b, 0, 0, 0))
    return pl.pallas_call(
        _ret_step_kernel,
        grid=(db // nb,),
        in_specs=[tok, tok, tok, tok, st, _const_spec((1, HALF)), _const_spec((1, HALF))],
        out_specs=[tok, st],
        out_shape=[jax.ShapeDtypeStruct((db, ds, HALF), BF16),
                   jax.ShapeDtypeStruct(state.shape, F32)],
        compiler_params=pltpu.CompilerParams(dimension_semantics=("parallel",),
                                             vmem_limit_bytes=VMEM_LIMIT),
        name="sample_retention",
    )(rq, rk, rv, rg, state, gn_w, gn_b)


def _split_components(q):
    lo_mask = lax.broadcasted_iota(jnp.int32, q.shape, 1) < QK_DIM
    zero = jnp.zeros_like(q)
    return jnp.concatenate([jnp.where(lo_mask, q, zero), jnp.where(lo_mask, zero, q)], axis=0)


def _sub_norm(acc, l, lam, w):
    t = acc.shape[0] // 2
    o = acc / l
    d = o[:t] - lam * o[t:]
    return d * lax.rsqrt(jnp.mean(d * d, axis=-1, keepdims=True) + EPS) * w * (1.0 - LAM_INIT)


def _diff_kernel(q_ref, k_ref, vt_ref, lq1_ref, lk1_ref, lq2_ref, lk2_ref, sub_ref, o_ref):
    i = pl.program_id(1)
    t = DIFF_T
    lam = _lambda(lq1_ref[...], lk1_ref[...], lq2_ref[...], lk2_ref[...])
    k_idx = lax.broadcasted_iota(jnp.int32, (t, LANES), 0).astype(F32)
    key = lax.broadcasted_iota(jnp.int32, (t, 2 * t), 0)
    col = lax.broadcasted_iota(jnp.int32, (t, 2 * t), 1)
    causal = key <= jnp.where(col >= t, col - t, col)
    heads = [slice(h * HEAD_DIM, (h + 1) * HEAD_DIM) for h in range(DIFF_HEADS)]
    qqs = [_split_components(q_ref[:, hs]) for hs in heads]

    def block(j, carry, masked):
        start = pl.multiple_of(j * t, t)
        k_pos = k_idx + (j * t).astype(F32)
        scores = [_dot_nt(k_ref[pl.ds(start, t), hs], qqs[h]) for h, hs in enumerate(heads)]

        def softmax(h):
            m, l, _ = carry[h]
            bias = (SLOPES[h] * LOG2E) * k_pos
            s = scores[h] + jnp.concatenate([bias] * (2 * t // LANES), axis=1)
            if masked:
                s = jnp.where(causal, s, -jnp.inf)
            m_new = jnp.maximum(m, jnp.max(s, axis=0, keepdims=True))
            alpha = jnp.exp2(m - m_new)
            p = jnp.exp2(s - m_new)
            return m_new, alpha * l + jnp.sum(p, axis=0, keepdims=True), alpha, p.astype(BF16)

        def values(h, soft):
            m_new, l, alpha, p = soft
            return m_new, l, alpha * carry[h][2] + _dot(vt_ref[j, heads[h], :], p)

        soft = [softmax(h) for h in range(DIFF_HEADS)]
        return tuple(values(h, soft[h]) for h in range(DIFF_HEADS))

    init = tuple((jnp.full((1, 2 * t), NEG_BIG, F32), jnp.zeros((1, 2 * t), F32),
                  jnp.zeros((HEAD_DIM, 2 * t), F32)) for _ in heads)
    carry = lax.fori_loop(0, i, functools.partial(block, masked=False), init)
    carry = block(i, carry, masked=True)
    for (_, l, acc), hs in zip(carry, heads):
        o = acc / l
        d = o[:, :t] - lam * o[:, t:]
        dn = d * lax.rsqrt(jnp.mean(d * d, axis=0, keepdims=True) + EPS)
        o_ref[:, hs] = (dn.T * sub_ref[:, hs] * (1.0 - LAM_INIT)).astype(BF16)


def _prompt_diff_attention(dq, dk, dvt, lams, sub_w, batch, seq):
    nq = seq // DIFF_T
    qspec = pl.BlockSpec((DIFF_T, HALF), lambda b, i: (b * nq + i, 0))
    kspec = pl.BlockSpec((seq, HALF), lambda b, i: (b, 0))
    vspec = pl.BlockSpec((nq, HALF, DIFF_T), lambda b, i: (b, 0, 0))
    lspec = _const_spec((1, QK_DIM))
    return pl.pallas_call(
        _diff_kernel,
        grid=(batch, nq),
        in_specs=[qspec, kspec, vspec, lspec, lspec, lspec, lspec, _const_spec((1, HALF))],
        out_specs=qspec,
        out_shape=jax.ShapeDtypeStruct((batch * seq, HALF), BF16),
        compiler_params=pltpu.CompilerParams(dimension_semantics=("parallel", "arbitrary"),
                                             vmem_limit_bytes=VMEM_LIMIT),
        name="prompt_diff_attention",
    )(dq, dk, dvt, *lams, sub_w)


class _PagedDecoder:
    def __init__(self, pt_ref, q_ref, kn_ref, vn_ref, lam_refs, sub_ref, ck_hbm, cv_hbm, o_ref,
                 kbuf, vbuf, sems, bias_s, m_s, l_s, acc_s, n_groups):
        self.pt_ref, self.q_ref, self.kn_ref, self.vn_ref = pt_ref, q_ref, kn_ref, vn_ref
        self.lam_refs, self.sub_ref, self.ck_hbm, self.cv_hbm, self.o_ref = lam_refs, sub_ref, ck_hbm, cv_hbm, o_ref
        self.kbuf, self.vbuf, self.sems = kbuf, vbuf, sems
        self.bias_s, self.m_s, self.l_s, self.acc_s = bias_s, m_s, l_s, acc_s
        self.n = q_ref.shape[1]
        self.rows = DIFF_HEADS * 2 * self.n
        self.page_rows = PAGE * DIFF_HEADS
        self.past = n_groups * DECODE_PAGES * PAGE
        r1 = lax.broadcasted_iota(jnp.int32, (self.rows, 1), 0)
        self.head = r1 // (2 * self.n)
        self.qi = r1 % self.n
        self.slope = jnp.zeros((self.rows, 1), F32)
        for h in range(DIFF_HEADS):
            self.slope = jnp.where(self.head == h, SLOPES[h] * LOG2E, self.slope)

    def _copies(self, seq, group):
        slot = group % DECODE_SLOTS
        copies = []
        for p in range(DECODE_PAGES):
            page = self.pt_ref[seq, group * DECODE_PAGES + p]
            copies.append(pltpu.make_async_copy(self.ck_hbm.at[page], self.kbuf.at[slot, p], self.sems.at[0, slot]))
            copies.append(pltpu.make_async_copy(self.cv_hbm.at[page], self.vbuf.at[slot, p], self.sems.at[1, slot]))
        return copies

    def start(self, seq, group):
        for idx, c in enumerate(self._copies(seq, group)):
            c.start(priority=idx % 2)

    def wait(self, seq, group):
        for c in self._copies(seq, group):
            c.wait()

    def init_bias(self):
        col = lax.broadcasted_iota(jnp.int32, self.bias_s.shape, 1)
        dist = (self.past + self.qi - col // DIFF_HEADS).astype(F32)
        self.bias_s[...] = jnp.where(col % DIFF_HEADS == self.head, -self.slope * dist, -jnp.inf)

    def begin_sequence(self):
        self.m_s[...] = jnp.full_like(self.m_s, NEG_BIG)
        self.l_s[...] = jnp.zeros_like(self.l_s)
        self.acc_s[...] = jnp.zeros_like(self.acc_s)
        q = self.q_ref[0]
        self.qq = jnp.concatenate([_split_components(q[:, h * HEAD_DIM:(h + 1) * HEAD_DIM].astype(F32))
                                   for h in range(DIFF_HEADS)], axis=0).astype(BF16)

    def scores(self, g):
        slot, per_chain = g % DECODE_SLOTS, DECODE_PAGES // DECODE_CHAINS
        return [jnp.concatenate([_dot_nt(self.qq, self.kbuf[slot, p].astype(BF16))
                                 for p in range(c * per_chain, (c + 1) * per_chain)], axis=-1)
                for c in range(DECODE_CHAINS)]

    def fold(self, g, scores):
        slot, page_rows = g % DECODE_SLOTS, self.page_rows
        per_chain = DECODE_PAGES // DECODE_CHAINS
        chains = []
        for c, s in enumerate(scores):
            pages = range(c * per_chain, (c + 1) * per_chain)
            s = s + jnp.concatenate([self.bias_s[...] + self.slope * float((g * DECODE_PAGES + p) * PAGE)
                                     for p in pages], axis=-1)
            m_c = jnp.max(s, axis=-1, keepdims=True)
            pb = jnp.exp2(s - m_c)
            l_c = jnp.sum(pb, axis=-1, keepdims=True)
            pb = pb.astype(BF16)
            pv_c = _dot(pb[:, :page_rows], self.vbuf[slot, pages[0]].astype(BF16))
            for idx, pg in enumerate(pages[1:], start=1):
                pv_c = pv_c + _dot(pb[:, idx * page_rows:(idx + 1) * page_rows], self.vbuf[slot, pg].astype(BF16))
            chains.append((m_c, l_c, pv_c))
        m = self.m_s[:, :1]
        m_new = m
        for m_c, _, _ in chains:
            m_new = jnp.maximum(m_new, m_c)
        alpha = jnp.exp2(m - m_new)
        l_new = alpha * self.l_s[:, :1]
        acc = alpha * self.acc_s[...]
        for m_c, l_c, pv_c in chains:
            w = jnp.exp2(m_c - m_new)
            l_new = l_new + w * l_c
            acc = acc + w * pv_c
        self.m_s[...] = jnp.broadcast_to(m_new, (self.rows, LANES))
        self.l_s[...] = jnp.broadcast_to(l_new, (self.rows, LANES))
        self.acc_s[...] = acc

    def finish(self):
        n = self.n
        lam = _lambda(*(ref[...] for ref in self.lam_refs))
        r8 = lax.broadcasted_iota(jnp.int32, (2 * n, 1), 0)
        qi = jnp.where(r8 >= n, r8 - n, r8)
        for h in range(DIFF_HEADS):
            hs = slice(h * HEAD_DIM, (h + 1) * HEAD_DIM)
            hr = slice(h * 2 * n, (h + 1) * 2 * n)
            qq = _split_components(self.q_ref[0, :, hs].astype(F32))
            kn = self.kn_ref[:, h, :]
            vn = self.vn_ref[:, h, :]
            m, l, acc = self.m_s[hr, :1], self.l_s[hr, :1], self.acc_s[hr, :]
            s_cols = []
            for t in range(n):
                s_t = jnp.sum(qq * kn[t:t + 1, :], axis=-1, keepdims=True)
                s_t = s_t - SLOPES[h] * LOG2E * (qi - t).astype(F32)
                s_cols.append(jnp.where(qi >= t, s_t, -jnp.inf))
            m_new = m
            for s_t in s_cols:
                m_new = jnp.maximum(m_new, s_t)
            alpha = jnp.exp2(m - m_new)
            l = alpha * l
            acc = alpha * acc
            for t in range(n):
                p_t = jnp.exp2(s_cols[t] - m_new)
                l = l + p_t
                acc = acc + p_t * vn[t:t + 1, :]
            self.o_ref[0, :, hs] = _sub_norm(acc, l, lam, self.sub_ref[:, hs]).astype(BF16)


def _channel_phases(x_ref, r_ref, dn_ref, p_ref, wo_ref, ln2_ref, wfi_ref, wfo_ref, lnp_ref, wpg_ref, wpp_ref,
                    y_ref):
    st = {}

    def attention_out():
        mixed = jnp.concatenate([r_ref[...], dn_ref[...]], axis=-1)
        st["h"] = x_ref[...] + _dot(mixed, wo_ref[...])
        st["hn"] = _rms(st["h"], ln2_ref[...]).astype(BF16)

    def ffn_chunk(c):
        g = _dot(st["hn"], wfi_ref[:, c * FF_CHUNK:(c + 1) * FF_CHUNK])
        u = _dot(st["hn"], wfi_ref[:, D_FF + c * FF_CHUNK:D_FF + (c + 1) * FF_CHUNK])
        act = (g * _sigmoid(g) * u).astype(BF16)
        part = _dot(act, wfo_ref[c * FF_CHUNK:(c + 1) * FF_CHUNK, :])
        st["ff"] = part if c == 0 else st["ff"] + part

    def embedding_gate():
        h = st["h"] + st["ff"]
        gate = _sigmoid(_dot(_rms(h, lnp_ref[...]).astype(BF16), wpg_ref[...]))
        y_ref[...] = h + gate * _dot(p_ref[...].astype(BF16), wpp_ref[...])

    return ([attention_out] + [functools.partial(ffn_chunk, c) for c in range(D_FF // FF_CHUNK)]
            + [embedding_gate])


def _channel_kernel(*refs):
    for phase in _channel_phases(*refs):
        phase()


def _channel_specs(tm, index):
    tok = lambda width: pl.BlockSpec((tm, width), index)
    return [tok(D_MODEL), tok(HALF), tok(HALF), tok(PLE_DIM),
            _const_spec((D_MODEL, D_MODEL)), _const_spec((1, D_MODEL)),
            _const_spec((D_MODEL, 2 * D_FF)), _const_spec((D_FF, D_MODEL)),
            _const_spec((1, D_MODEL)), _const_spec((D_MODEL, D_MODEL)),
            _const_spec((PLE_DIM, D_MODEL))], tok(D_MODEL)


def _channel(x, r, dn, p, chan_w, tm):
    m = x.shape[0]
    in_specs, out_spec = _channel_specs(tm, lambda i: (i, 0))
    return pl.pallas_call(
        _channel_kernel,
        grid=(m // tm,),
        in_specs=in_specs,
        out_specs=out_spec,
        out_shape=jax.ShapeDtypeStruct((m, D_MODEL), F32),
        compiler_params=pltpu.CompilerParams(dimension_semantics=("parallel",),
                                             vmem_limit_bytes=VMEM_LIMIT),
        name="channel",
    )(x, r, dn, p, *chan_w)


def _channel_decode_kernel(pt_ref, x_ref, r_ref, dn_ref, p_ref, wo_ref, ln2_ref, wfi_ref, wfo_ref, lnp_ref, wpg_ref,
                           wpp_ref, q_ref, kn_ref, vn_ref, lq1_ref, lk1_ref, lq2_ref, lk2_ref, sub_ref, ck_hbm,
                           cv_hbm, y_ref, o_ref, kbuf, vbuf, sems, bias_s, m_s, l_s, acc_s, *, n_seq, n_groups):
    i = pl.program_id(0)
    ahead = DECODE_SLOTS - 1
    chan_refs = (x_ref, r_ref, dn_ref, p_ref, wo_ref, ln2_ref, wfi_ref, wfo_ref, lnp_ref, wpg_ref, wpp_ref)
    dec = _PagedDecoder(pt_ref, q_ref, kn_ref, vn_ref, (lq1_ref, lk1_ref, lq2_ref, lk2_ref), sub_ref, ck_hbm,
                        cv_hbm, o_ref, kbuf, vbuf, sems, bias_s, m_s, l_s, acc_s, n_groups)

    @pl.when(i == 0)
    def _():
        dec.init_bias()
        for g in range(ahead):
            dec.start(0, g)

    dec.begin_sequence()
    phases = _channel_phases(*chan_refs, y_ref)
    per_group = -(-len(phases) // n_groups)
    for g in range(n_groups):
        nxt = g + ahead
        if nxt < n_groups:
            dec.start(i, nxt)
        else:
            pl.when(i + 1 < n_seq)(functools.partial(dec.start, i + 1, nxt - n_groups))
        dec.wait(i, g)
        scores = dec.scores(g)
        for phase in phases[g * per_group:(g + 1) * per_group]:
            phase()
        dec.fold(g, scores)
    dec.finish()


def _channel_and_decode(x, r, dn, p, chan_w, page_table, dq, dk_new, dv_new, cache_k, cache_v, lams, sub_w):
    m = x.shape[0]
    db, ds = dq.shape[:2]
    tm = m // db
    assert tm * db == m and tm % 256 == 0, (m, db)
    n_groups = page_table.shape[1] // DECODE_PAGES
    assert n_groups * DECODE_PAGES == page_table.shape[1] and n_groups % DECODE_SLOTS == 0
    chan_in, chan_out = _channel_specs(tm, lambda i, pt: (i, 0))
    tok = pl.BlockSpec((1, ds, HALF), lambda i, pt: (i, 0, 0))
    new = pl.BlockSpec((None, ds, DIFF_HEADS, HEAD_DIM), lambda i, pt: (i, 0, 0, 0))
    page_rows = PAGE * DIFF_HEADS
    cache_k, cache_v = (c.reshape(-1, page_rows, HEAD_DIM) for c in (cache_k, cache_v))
    hbm = pl.BlockSpec(memory_space=pl.ANY)
    ring = pltpu.VMEM((DECODE_SLOTS, DECODE_PAGES, page_rows, HEAD_DIM), F32)
    state = pltpu.VMEM((DIFF_HEADS * 2 * ds, LANES), F32)
    bias = pltpu.VMEM((DIFF_HEADS * 2 * ds, page_rows), F32)
    return pl.pallas_call(
        functools.partial(_channel_decode_kernel, n_seq=db, n_groups=n_groups),
        grid_spec=pltpu.PrefetchScalarGridSpec(
            num_scalar_prefetch=1,
            grid=(db,),
            in_specs=chan_in + [tok, new, new] + [_const_spec((1, QK_DIM))] * 4 + [_const_spec((1, HALF)), hbm, hbm],
            out_specs=[chan_out, tok],
            scratch_shapes=[ring, ring, pltpu.SemaphoreType.DMA((2, DECODE_SLOTS)), bias, state, state, state],
        ),
        out_shape=[jax.ShapeDtypeStruct((m, D_MODEL), F32), jax.ShapeDtypeStruct((db, ds, HALF), BF16)],
        compiler_params=pltpu.CompilerParams(dimension_semantics=("arbitrary",),
                                             vmem_limit_bytes=FUSED_VMEM_LIMIT),
        name="channel_and_decode",
    )(page_table, x, r, dn, p, *chan_w, dq, dk_new, dv_new, *lams, sub_w, cache_k, cache_v)


def kernel(x_prompt, x_sample, cache_k, cache_v, state_ret, page_table, p_prompt, p_sample, ln1_w, w_in, q_norm_w, k_norm_w, lambda_q1, lambda_k1, lambda_q2, lambda_k2, ret_gn_w, ret_gn_b, diff_subln_w, w_o, ln2_w, w_ffn_in, w_ffn_out, ln_ple_w, w_ple_gate, w_ple_proj):
    depth = w_in.shape[0]
    assert depth == 1, "single-layer trunk"
    b, s, _ = x_prompt.shape
    db, ds, _ = x_sample.shape

    row = lambda a: a[0].reshape(1, -1)
    wb = lambda a: a[0].astype(BF16)
    qn_w = jnp.tile(row(q_norm_w), (1, HALF // QK_DIM))
    kn_w = jnp.tile(row(k_norm_w), (1, HALF // QK_DIM))
    lams = (row(lambda_q1), row(lambda_k1), row(lambda_q2), row(lambda_k2))
    proj_w = (row(ln1_w), wb(w_in), qn_w, kn_w)
    chan_w = (wb(w_o), row(ln2_w), wb(w_ffn_in), wb(w_ffn_out), row(ln_ple_w), wb(w_ple_gate), wb(w_ple_proj))
    gn_w, gn_b, sub_w = row(ret_gn_w), row(ret_gn_b), row(diff_subln_w)

    xp = x_prompt.reshape(b * s, D_MODEL)
    dq, dk, dv, dkb, dvt, r, ret_fin = _project_and_retain(xp, *proj_w, gn_w, gn_b, b, s)
    dn = _prompt_diff_attention(dq, dkb, dvt, lams, sub_w, b, s)

    xs = x_sample.reshape(db * ds, D_MODEL)
    rq_s, rk_s, rv_s, rg_s, dq_s, dk_s, dv_s, _, _ = _project(xs, *proj_w, tm=db * ds)
    tok3 = lambda a: a.reshape(db, ds, HALF)
    r_s, ret_new = _sample_retention(tok3(rq_s), tok3(rk_s), tok3(rv_s), tok3(rg_s), state_ret[0], gn_w, gn_b)
    new4 = lambda a: a.reshape(db, ds, DIFF_HEADS, HEAD_DIM)
    y_prompt, dn_s = _channel_and_decode(xp, r, dn, p_prompt[0].reshape(b * s, PLE_DIM), chan_w, page_table,
                                         tok3(dq_s), new4(dk_s), new4(dv_s), cache_k, cache_v, lams, sub_w)
    y_sample = _channel(xs, r_s.reshape(db * ds, HALF), dn_s.reshape(db * ds, HALF),
                        p_sample[0].reshape(db * ds, PLE_DIM), chan_w, tm=db * ds)

    heads = lambda a, n: a.reshape(1, n, -1, DIFF_HEADS, HEAD_DIM)
    return (y_prompt.reshape(b, s, D_MODEL), y_sample.reshape(db, ds, D_MODEL),
            heads(dk, b), heads(dv, b), ret_fin[None],
            heads(dk_s, db), heads(dv_s, db), ret_new[None])
```
